```python
import math
import jax, jax.numpy as jnp
from jax import lax
import numpy as np

D_MODEL = 4096
BATCH = 2
SEQ = 8192
DEPTH = 4

N_MEM = 256
MEM_HEADS = 4
MEM_HEAD_DIM = 128
MEM_W = MEM_HEADS * MEM_HEAD_DIM
CONV_CH = 1536
CONV_K = 3
ATTN_GROUPS = ((128, 1), (512, 4), (2048, 16))
HEADS_PER_GROUP = 4
HEAD_DIM = 128
N_ATTN_HEADS = HEADS_PER_GROUP * len(ATTN_GROUPS)
ATTN_W = N_ATTN_HEADS * HEAD_DIM
ATTN_OUT = HEADS_PER_GROUP * HEAD_DIM
BLOCK = 128
N_BUCKETS = 32
MAX_DISTANCE = 2048
N_EXPERTS = 16
N_EXPERT_GROUPS = 4
EXPERTS_PER_GROUP = N_EXPERTS // N_EXPERT_GROUPS
TOP_K = 2
D_FF_EXPERT = 512
ALPHA = (2 * DEPTH) ** 0.25
BETA = (8 * DEPTH) ** -0.25
LN_EPS = 1e-5
NEG_INF = -1e30
IN_COLS = 3 * CONV_CH + 3 * ATTN_W + 2 * D_MODEL
SPLITS = (CONV_CH, 2 * CONV_CH, 3 * CONV_CH,
          3 * CONV_CH + ATTN_W, 3 * CONV_CH + 2 * ATTN_W, 3 * CONV_CH + 3 * ATTN_W,
          3 * CONV_CH + 3 * ATTN_W + D_MODEL)

kernel_name = "hybrid_conv_dilated_attn_memxattn_grouped_moe_deepnorm"


def layer_norm(x, g, b):
    xf = x.astype(jnp.float32)
    mu = jnp.mean(xf, axis=-1, keepdims=True)
    var = jnp.mean(jnp.square(xf - mu), axis=-1, keepdims=True)
    return ((xf - mu) * lax.rsqrt(var + LN_EPS) * g.astype(jnp.float32) + b.astype(jnp.float32)).astype(x.dtype)


def t5_bucket(dist):
    max_exact = N_BUCKETS // 2
    d = jnp.maximum(dist, 0)
    log_part = jnp.log(jnp.maximum(d, 1).astype(jnp.float32) / max_exact) / math.log(MAX_DISTANCE / max_exact)
    large = jnp.minimum(max_exact + (log_part * (N_BUCKETS - max_exact)).astype(jnp.int32), N_BUCKETS - 1)
    return jnp.where(d < max_exact, d, large)


def dilated_group_attention(q, k, v, bias_table, window, dilation):
    B, S, H, Dh = q.shape
    r = dilation
    span = window // dilation
    L = S // r
    nblk = -(-L // BLOCK)
    Lp = nblk * BLOCK

    def to_strided(t):
        t = t.reshape(B, L, r, H, Dh).transpose(0, 2, 1, 3, 4)
        t = jnp.pad(t, ((0, 0), (0, 0), (0, Lp - L), (0, 0), (0, 0)))
        return t.reshape(B, r, nblk, BLOCK, H, Dh)

    def with_prev(t):
        prev = jnp.pad(t, ((0, 0), (0, 0), (1, 0), (0, 0), (0, 0), (0, 0)))[:, :, :-1]
        return jnp.concatenate([prev, t], axis=3)

    qs = to_strided(q)
    kk = with_prev(to_strided(k))
    vv = with_prev(to_strided(v))

    logits = jnp.einsum('brnqhd,brnkhd->brnhqk', qs, kk,
                        preferred_element_type=jnp.float32) * (Dh ** -0.5)
    qi = jnp.arange(BLOCK)[:, None] + BLOCK
    ki = jnp.arange(2 * BLOCK)[None, :]
    rel = qi - ki
    bias = bias_table[t5_bucket(rel * r)].astype(jnp.float32).transpose(2, 0, 1)
    band = (rel >= 0) & (rel <= span)
    blk = jnp.arange(nblk)[:, None, None]
    valid = band[None] & ((blk > 0) | (ki[None] >= BLOCK))
    logits = jnp.where(valid[None, None, :, None], logits + bias[None, None, None], NEG_INF)

    m = jnp.max(logits, axis=-1, keepdims=True)
    p = jnp.exp(logits - m)
    s = jnp.sum(p, axis=-1, keepdims=True)
    o = jnp.einsum('brnhqk,brnkhd->brnqhd', p / s, vv.astype(jnp.float32))
    lse = (m + jnp.log(s))[..., 0].transpose(0, 1, 2, 4, 3)

    o = o.reshape(B, r, Lp, H, Dh)[:, :, :L].transpose(0, 2, 1, 3, 4).reshape(B, S, H, Dh)
    lse = lse.reshape(B, r, Lp, H)[:, :, :L].transpose(0, 2, 1, 3).reshape(B, S, H)
    return o, lse


def hybrid_mixer(x, w_in, conv_w, w_conv_proj, w_attn_proj, w_out, rel_bias):
    B, S, _ = x.shape
    b_gate, c_gate, h, q, k, v, g_conv, g_attn = jnp.split(x @ w_in, SPLITS, axis=-1)

    u = jnp.pad(c_gate * h, ((0, 0), (CONV_K - 1, 0), (0, 0)))
    conv = sum(conv_w[j] * u[:, j:j + S] for j in range(CONV_K))
    y_conv = (b_gate * conv) @ w_conv_proj

    q = q.reshape(B, S, N_ATTN_HEADS, HEAD_DIM)
    k = k.reshape(B, S, N_ATTN_HEADS, HEAD_DIM)
    v = v.reshape(B, S, N_ATTN_HEADS, HEAD_DIM)
    outs, lses = [], []
    for g, (window, dilation) in enumerate(ATTN_GROUPS):
        hs = slice(g * HEADS_PER_GROUP, (g + 1) * HEADS_PER_GROUP)
        o_g, lse_g = dilated_group_attention(q[:, :, hs], k[:, :, hs], v[:, :, hs],
                                             rel_bias[:, hs], window, dilation)
        outs.append(o_g)
        lses.append(lse_g)
    wts = jax.nn.softmax(jnp.stack(lses), axis=0)
    o = jnp.sum(wts[..., None] * jnp.stack(outs), axis=0)
    y_attn = o.reshape(B, S, ATTN_OUT).astype(x.dtype) @ w_attn_proj

    merged = jax.nn.sigmoid(g_conv) * y_conv + jax.nn.sigmoid(g_attn) * y_attn
    return merged @ w_out


def memory_cross_attention(x, mem, w_q, w_kv, w_o):
    B, S, _ = x.shape
    M = mem.shape[1]
    q = (x @ w_q).reshape(B, S, MEM_HEADS, MEM_HEAD_DIM)
    kv = (mem @ w_kv).reshape(B, M, 2, MEM_HEADS, MEM_HEAD_DIM)
    k, v = kv[:, :, 0], kv[:, :, 1]
    logits = jnp.einsum('bshd,bmhd->bhsm', q, k, preferred_element_type=jnp.float32) * (MEM_HEAD_DIM ** -0.5)
    p = jax.nn.softmax(logits, axis=-1)
    o = jnp.einsum('bhsm,bmhd->bshd', p.astype(v.dtype), v)
    return o.reshape(B, S, MEM_W) @ w_o


def grouped_moe(x, w_router, b_router, w_gate, w_up, w_down):
    B, S, D = x.shape
    t = x.reshape(B * S, D)
    logits = (t @ w_router).astype(jnp.float32) + b_router.astype(jnp.float32)
    probs = jax.nn.softmax(logits, axis=-1)
    gp = probs.reshape(-1, N_EXPERT_GROUPS, EXPERTS_PER_GROUP)
    group_score = jnp.sum(lax.top_k(gp, TOP_K)[0], axis=-1)
    _, gidx = lax.top_k(group_score, 1)
    in_group = jnp.take_along_axis(gp, gidx[:, :, None], axis=1)[:, 0]
    top_w, top_local = lax.top_k(in_group, TOP_K)
    top_e = gidx * EXPERTS_PER_GROUP + top_local
    top_w = top_w / jnp.sum(top_w, axis=-1, keepdims=True)
    combine = jnp.sum(jax.nn.one_hot(top_e, N_EXPERTS, dtype=jnp.float32) * top_w[..., None], axis=1)

    hg = jnp.einsum('td,edf->tef', t, w_gate)
    hu = jnp.einsum('td,edf->tef', t, w_up)
    hid = jax.nn.silu(hg) * hu * combine[:, :, None].astype(t.dtype)
    y = jnp.einsum('tef,efd->td', hid, w_down)
    return y.reshape(B, S, D)


def setup_inputs(seed: int = 0) -> dict:
    key = jax.random.key(seed)
    ks = jax.random.split(key, 24)
    D = D_MODEL

    def nrm(k, shape, scale):
        return jax.random.normal(k, shape, jnp.float32) * scale

    return {
        "x": nrm(ks[0], (BATCH, SEQ, D), 1.0),
        "mem": nrm(ks[1], (BATCH, N_MEM, D), 1.0),
        "w_in": nrm(ks[2], (DEPTH, D, IN_COLS), D ** -0.5),
        "conv_w": nrm(ks[3], (DEPTH, CONV_K, CONV_CH), CONV_K ** -0.5),
        "w_conv_proj": nrm(ks[4], (DEPTH, CONV_CH, D), CONV_CH ** -0.5),
        "w_attn_proj": nrm(ks[5], (DEPTH, ATTN_OUT, D), ATTN_OUT ** -0.5),
        "w_out": nrm(ks[6], (DEPTH, D, D), BETA * D ** -0.5),
        "rel_bias": nrm(ks[7], (N_BUCKETS, N_ATTN_HEADS), 0.5),
        "ln1_g": 1.0 + nrm(ks[8], (DEPTH, D), 0.02),
        "ln1_b": nrm(ks[9], (DEPTH, D), 0.02),
        "w_mem_q": nrm(ks[10], (DEPTH, D, MEM_W), D ** -0.5),
        "w_mem_kv": nrm(ks[11], (DEPTH, D, 2 * MEM_W), D ** -0.5),
        "w_mem_o": nrm(ks[12], (DEPTH, MEM_W, D), BETA * MEM_W ** -0.5),
        "ln2_g": 1.0 + nrm(ks[13], (DEPTH, D), 0.02),
        "ln2_b": nrm(ks[14], (DEPTH, D), 0.02),
        "w_router": nrm(ks[15], (D, N_EXPERTS), D ** -0.5),
        "b_router": nrm(ks[16], (N_EXPERTS,), 0.01),
        "w_gate": nrm(ks[17], (DEPTH, N_EXPERTS, D, D_FF_EXPERT), D ** -0.5),
        "w_up": nrm(ks[18], (DEPTH, N_EXPERTS, D, D_FF_EXPERT), D ** -0.5),
        "w_down": nrm(ks[19], (DEPTH, N_EXPERTS, D_FF_EXPERT, D), BETA * D_FF_EXPERT ** -0.5),
        "ln3_g": 1.0 + nrm(ks[20], (DEPTH, D), 0.02),
        "ln3_b": nrm(ks[21], (DEPTH, D), 0.02),
    }


def reference(x, mem, w_in, conv_w, w_conv_proj, w_attn_proj, w_out, rel_bias, ln1_g, ln1_b,
              w_mem_q, w_mem_kv, w_mem_o, ln2_g, ln2_b, w_router, b_router,
              w_gate, w_up, w_down, ln3_g, ln3_b):
    for l in range(DEPTH):
        mix = hybrid_mixer(x, w_in[l], conv_w[l], w_conv_proj[l], w_attn_proj[l], w_out[l], rel_bias)
        x = layer_norm(ALPHA * x + mix, ln1_g[l], ln1_b[l])
        xa = memory_cross_attention(x, mem, w_mem_q[l], w_mem_kv[l], w_mem_o[l])
        x = layer_norm(ALPHA * x + xa, ln2_g[l], ln2_b[l])
        ff = grouped_moe(x, w_router, b_router, w_gate[l], w_up[l], w_down[l])
        x = layer_norm(ALPHA * x + ff, ln3_g[l], ln3_b[l])
    return x
```

```python
import functools
import math

import jax
import jax.numpy as jnp
from jax import lax
from jax.experimental import pallas as pl
from jax.experimental.pallas import tpu as pltpu

HEAD_DIM = 128
ATTN_GROUPS = ((128, 1), (512, 4), (2048, 16))
ATTN_BLOCK = 128
N_BUCKETS = 32
MAX_DISTANCE = 2048
MEM_HEAD_DIM = 128
N_EXPERT_GROUPS = 4
LN_EPS = 1e-5
NEG_INF = -1e30

VMEM_LIMIT_BYTES = 56 * 1024 * 1024

F32 = jnp.float32
BF16 = jnp.bfloat16


def _params(*sem):
    return pltpu.CompilerParams(dimension_semantics=sem, vmem_limit_bytes=VMEM_LIMIT_BYTES)


def _pick(n, pref):
    if n <= pref:
        return n
    t = pref
    while n % t:
        t //= 2
    return t


def _sigmoid(x):
    return 1.0 / (1.0 + jnp.exp(-x))


def _layer_norm(x, g, b):
    mu = jnp.mean(x, axis=-1, keepdims=True)
    xc = x - mu
    var = jnp.mean(xc * xc, axis=-1, keepdims=True)
    return xc * lax.rsqrt(var + LN_EPS) * g + b


def _mm_kernel(*refs, sig_from, alpha):
    if alpha is None:
        a_ref, w_ref, o_ref = refs
    else:
        a_ref, w_ref, r_ref, o_ref = refs
    acc = jnp.dot(a_ref[...], w_ref[...], preferred_element_type=F32)
    if alpha is not None:
        acc = acc + alpha * r_ref[...]
    if sig_from is None:
        o_ref[...] = acc.astype(o_ref.dtype)
    else:
        j = pl.program_id(1)

        @pl.when(j < sig_from)
        def _():
            o_ref[...] = acc.astype(o_ref.dtype)

        @pl.when(j >= sig_from)
        def _():
            o_ref[...] = _sigmoid(acc).astype(o_ref.dtype)


def _matmul(a, w, layer, out_dtype, *, tm, tn, sig_from_col=None, resid=None, alpha=None, name):
    m, k = a.shape
    n = w.shape[-1]
    tm, tn = _pick(m, tm), _pick(n, tn)
    sig_from = None
    if sig_from_col is not None:
        assert sig_from_col % tn == 0
        sig_from = sig_from_col // tn
    in_specs = [pl.BlockSpec((tm, k), lambda i, j: (i, 0)),
                pl.BlockSpec((None, k, tn), lambda i, j: (layer, 0, j))]
    args = [a, w]
    if resid is not None:
        in_specs.append(pl.BlockSpec((tm, tn), lambda i, j: (i, j)))
        args.append(resid)
    return pl.pallas_call(
        functools.partial(_mm_kernel, sig_from=sig_from, alpha=alpha),
        grid=(m // tm, n // tn),
        in_specs=in_specs,
        out_specs=pl.BlockSpec((tm, tn), lambda i, j: (i, j)),
        out_shape=jax.ShapeDtypeStruct((m, n), out_dtype),
        compiler_params=_params("parallel", "arbitrary"),
        name=name,
    )(*args)


def _dattn_kernel(bucket_ref, tbl_ref, q_ref, kc_ref, kp_ref, vc_ref, vp_ref, o_ref, lse_ref,
                  bias_s, kcat, vcat, *, qb, heads, span, scale):
    blk = ATTN_BLOCK
    n = pl.program_id(1)

    @pl.when((pl.program_id(0) == 0) & (n == 0))
    def _():
        bk = bucket_ref[...]
        row = lax.broadcasted_iota(jnp.int32, (blk, 2 * blk), 0)
        col = lax.broadcasted_iota(jnp.int32, (blk, 2 * blk), 1)
        rel = row + blk - col
        band = (rel >= 0) & (rel <= span)
        for h in range(heads):
            bias = jnp.zeros((blk, 2 * blk), F32)
            for b in range(N_BUCKETS):
                bias = jnp.where(bk == b, tbl_ref[h, b], bias)
            bias_s[0, h] = jnp.where(band, bias, NEG_INF)
            bias_s[1, h] = jnp.where(band & (col >= blk), bias, NEG_INF)

    kcat[0:blk] = kp_ref[...]
    kcat[blk:] = kc_ref[...]
    vcat[0:blk] = vp_ref[...]
    vcat[blk:] = vc_ref[...]
    lane = lax.broadcasted_iota(jnp.int32, (blk, blk), 1)
    for s in range(qb):
        first = jnp.where(n == 0, 1, 0) if s == 0 else 0
        lse_blk = jnp.zeros((blk, blk), F32)
        for h in range(heads):
            cs = slice(h * HEAD_DIM, (h + 1) * HEAD_DIM)
            q = q_ref[s * blk:(s + 1) * blk, cs]
            k = kcat[s * blk:(s + 2) * blk, cs]
            v = vcat[s * blk:(s + 2) * blk, cs]
            lg = lax.dot_general(q, k, (((1,), (1,)), ((), ())), preferred_element_type=F32) * scale
            lg = lg + bias_s[first, h]
            m = jnp.max(lg, axis=-1, keepdims=True)
            p = jnp.exp(lg - m)
            l = jnp.sum(p, axis=-1, keepdims=True)
            o = jnp.dot(p.astype(BF16), v, preferred_element_type=F32) / l
            o_ref[s * blk:(s + 1) * blk, cs] = o.astype(o_ref.dtype)
            lse_blk = jnp.where(lane == h, m + jnp.log(l), lse_blk)
        lse_ref[s * blk:(s + 1) * blk, :] = lse_blk


def _t5_bucket(dist):
    max_exact = N_BUCKETS // 2
    d = jnp.maximum(dist, 0)
    log_part = jnp.log(jnp.maximum(d, 1).astype(F32) / max_exact) / math.log(MAX_DISTANCE / max_exact)
    large = jnp.minimum(max_exact + (log_part * (N_BUCKETS - max_exact)).astype(jnp.int32), N_BUCKETS - 1)
    return jnp.where(d < max_exact, d, large)


def _dilated_attention(qkv, tbl, batch, seq, window, dilation, name):
    r = dilation
    blk = ATTN_BLOCK
    _, br, l, w = qkv.shape
    heads = w // HEAD_DIM
    qb = _pick(l // blk, 4)
    span = window // dilation
    qi = jnp.arange(blk)[:, None] + blk
    ki = jnp.arange(2 * blk)[None, :]
    bucket = _t5_bucket((qi - ki) * r).astype(jnp.int32)

    def cur(which):
        return pl.BlockSpec((None, None, qb * blk, w), lambda b, n: (which, b, n, 0))

    def prev(which):
        return pl.BlockSpec((None, None, blk, w), lambda b, n: (which, b, jnp.maximum(n * qb - 1, 0), 0))

    return pl.pallas_call(
        functools.partial(_dattn_kernel, qb=qb, heads=heads, span=span, scale=HEAD_DIM ** -0.5),
        grid=(br, l // (qb * blk)),
        in_specs=[pl.BlockSpec((blk, 2 * blk), lambda b, n: (0, 0)),
                  pl.BlockSpec(memory_space=pltpu.SMEM),
                  cur(0), cur(1), prev(1), cur(2), prev(2)],
        out_specs=[pl.BlockSpec((None, qb * blk, w), lambda b, n: (b, n, 0)),
                   pl.BlockSpec((None, qb * blk, blk), lambda b, n: (b, n, 0))],
        out_shape=[jax.ShapeDtypeStruct((br, l, w), BF16),
                   jax.ShapeDtypeStruct((br, l, blk), F32)],
        scratch_shapes=[pltpu.VMEM((2, heads, blk, 2 * blk), F32),
                        pltpu.VMEM(((qb + 1) * blk, w), BF16),
                        pltpu.VMEM(((qb + 1) * blk, w), BF16)],
        compiler_params=_params("arbitrary", "arbitrary"),
        name=name,
    )(bucket, tbl, qkv, qkv, qkv, qkv, qkv)


def _merge_kernel(*refs, n_groups, heads, conv_k, tm, halo, seq_tiles):
    (b_ref, c_ref, h_ref, cp_ref, hp_ref, cw_ref) = refs[:6]
    o_refs = refs[6:6 + n_groups]
    lse_refs = refs[6 + n_groups:6 + 2 * n_groups]
    sgc_ref, sga_ref, wc_ref, wa_ref, out_ref, u_s, aconv_s, oatt_s = refs[6 + 2 * n_groups:]
    i = pl.program_id(0)

    @pl.when(pl.program_id(1) == 0)
    def _():
        seq_start = (i % seq_tiles) == 0
        up = cp_ref[...].astype(F32) * hp_ref[...].astype(F32)
        u_s[0:halo] = jnp.where(seq_start, 0.0, up)
        u_s[halo:] = c_ref[...].astype(F32) * h_ref[...].astype(F32)
        conv = jnp.zeros((tm, u_s.shape[1]), F32)
        for j in range(conv_k):
            off = halo - (conv_k - 1) + j
            conv = conv + cw_ref[j:j + 1, :] * u_s[off:off + tm]
        aconv_s[...] = (b_ref[...].astype(F32) * conv).astype(BF16)
        for h in range(heads):
            cs = slice(h * HEAD_DIM, (h + 1) * HEAD_DIM)
            ls = [r[:, h:h + 1] for r in lse_refs]
            mx = functools.reduce(jnp.maximum, ls)
            es = [jnp.exp(x - mx) for x in ls]
            den = functools.reduce(lambda a, b: a + b, es)
            acc = jnp.zeros((tm, HEAD_DIM), F32)
            for e, o_ref in zip(es, o_refs):
                acc = acc + (e / den) * o_ref[:, cs].astype(F32)
            oatt_s[:, cs] = acc.astype(BF16)

    yc = jnp.dot(aconv_s[...], wc_ref[...], preferred_element_type=F32)
    ya = jnp.dot(oatt_s[...], wa_ref[...], preferred_element_type=F32)
    out_ref[...] = (sgc_ref[...].astype(F32) * yc + sga_ref[...].astype(F32) * ya).astype(out_ref.dtype)


def _merge(big, os_, lses, conv_w, wc, wa, layer, seq, d_model, *, tm, tn, name):
    t = big.shape[0]
    conv_k, cc = conv_w.shape[-2:]
    aw = os_[0].shape[-1]
    heads = aw // HEAD_DIM
    n_groups = len(os_)
    tm, tn = _pick(seq, tm), _pick(d_model, tn)
    halo = 16
    assert cc % 128 == 0 and tm % halo == 0
    gc0 = (3 * cc + 3 * n_groups * aw) // tn
    ga0 = gc0 + d_model // tn
    assert gc0 * tn == 3 * cc + 3 * n_groups * aw

    def colblk(c):
        return pl.BlockSpec((tm, cc), lambda i, j: (i, c))

    def halo_blk(c):
        return pl.BlockSpec((halo, cc), lambda i, j: (jnp.maximum(i * (tm // halo) - 1, 0), c))

    row_aw = pl.BlockSpec((tm, aw), lambda i, j: (i, 0))
    row_lse = pl.BlockSpec((tm, ATTN_BLOCK), lambda i, j: (i, 0))
    in_specs = ([colblk(0), colblk(1), colblk(2), halo_blk(1), halo_blk(2),
                 pl.BlockSpec((None, conv_k, cc), lambda i, j: (layer, 0, 0))]
                + [row_aw] * n_groups + [row_lse] * n_groups
                + [pl.BlockSpec((tm, tn), lambda i, j: (i, gc0 + j)),
                   pl.BlockSpec((tm, tn), lambda i, j: (i, ga0 + j)),
                   pl.BlockSpec((None, cc, tn), lambda i, j: (layer, 0, j)),
                   pl.BlockSpec((None, aw, tn), lambda i, j: (layer, 0, j))])
    return pl.pallas_call(
        functools.partial(_merge_kernel, n_groups=n_groups, heads=heads, conv_k=conv_k, tm=tm, halo=halo,
                          seq_tiles=seq // tm),
        grid=(t // tm, d_model // tn),
        in_specs=in_specs,
        out_specs=pl.BlockSpec((tm, tn), lambda i, j: (i, j)),
        out_shape=jax.ShapeDtypeStruct((t, d_model), BF16),
        scratch_shapes=[pltpu.VMEM((tm + halo, cc), F32), pltpu.VMEM((tm, cc), BF16), pltpu.VMEM((tm, aw), BF16)],
        compiler_params=_params("parallel", "arbitrary"),
        name=name,
    )(big, big, big, big, big, conv_w, *os_, *lses, big, big, wc, wa)


def _xattn_kernel(s1_ref, g1_ref, b1_ref, wq_ref, kv_ref, wo_ref, g2_ref, b2_ref, x_ref, xb_ref, *,
                  heads, alpha, scale):
    x1 = _layer_norm(s1_ref[...], g1_ref[...], b1_ref[...])
    q = jnp.dot(x1.astype(BF16), wq_ref[...], preferred_element_type=F32).astype(BF16)
    mw = heads * MEM_HEAD_DIM
    outs = []
    for h in range(heads):
        cs = slice(h * MEM_HEAD_DIM, (h + 1) * MEM_HEAD_DIM)
        k = kv_ref[:, cs]
        v = kv_ref[:, mw + h * MEM_HEAD_DIM:mw + (h + 1) * MEM_HEAD_DIM]
        lg = lax.dot_general(q[:, cs], k, (((1,), (1,)), ((), ())), preferred_element_type=F32) * scale
        m = jnp.max(lg, axis=-1, keepdims=True)
        p = jnp.exp(lg - m)
        l = jnp.sum(p, axis=-1, keepdims=True)
        outs.append((jnp.dot(p.astype(BF16), v, preferred_element_type=F32) / l).astype(BF16))
    o = jnp.concatenate(outs, axis=-1)
    xa = jnp.dot(o, wo_ref[...], preferred_element_type=F32)
    x2 = _layer_norm(alpha * x1 + xa, g2_ref[...], b2_ref[...])
    x_ref[...] = x2
    xb_ref[...] = x2.astype(BF16)


def _xattn(s1, kv, g1, b1, wq, wo, g2, b2, layer, seq, alpha, *, tm, name):
    t, d = s1.shape
    tm = _pick(seq, tm)
    n_mem, kvw = kv.shape[1:]
    mw = wq.shape[-1]
    heads = mw // MEM_HEAD_DIM
    vec = pl.BlockSpec((None, 1, d), lambda i: (layer, 0, 0))
    return pl.pallas_call(
        functools.partial(_xattn_kernel, heads=heads, alpha=alpha, scale=MEM_HEAD_DIM ** -0.5),
        grid=(t // tm,),
        in_specs=[pl.BlockSpec((tm, d), lambda i: (i, 0)), vec, vec,
                  pl.BlockSpec((None, d, mw), lambda i: (layer, 0, 0)),
                  pl.BlockSpec((None, n_mem, kvw), lambda i: (i // (seq // tm), 0, 0)),
                  pl.BlockSpec((None, mw, d), lambda i: (layer, 0, 0)), vec, vec],
        out_specs=[pl.BlockSpec((tm, d), lambda i: (i, 0)), pl.BlockSpec((tm, d), lambda i: (i, 0))],
        out_shape=[jax.ShapeDtypeStruct((t, d), F32), jax.ShapeDtypeStruct((t, d), BF16)],
        compiler_params=_params("parallel"),
        name=name,
    )(s1, g1, b1, wq, kv, wo, g2, b2)


def _route(p):
    epg = len(p) // N_EXPERT_GROUPS
    assert epg == 4
    scores = []
    for g in range(N_EXPERT_GROUPS):
        a, b, c, d = p[g * epg:(g + 1) * epg]
        hi1, lo1, hi2, lo2 = jnp.maximum(a, b), jnp.minimum(a, b), jnp.maximum(c, d), jnp.minimum(c, d)
        scores.append(jnp.maximum(hi1, hi2) + jnp.maximum(jnp.minimum(hi1, hi2), jnp.maximum(lo1, lo2)))
    best, gidx = scores[0], jnp.zeros(scores[0].shape, jnp.int32)
    for g in range(1, N_EXPERT_GROUPS):
        take = scores[g] > best
        best = jnp.where(take, scores[g], best)
        gidx = jnp.where(take, g, gidx)
    sel = []
    for k in range(epg):
        s = p[k]
        for g in range(1, N_EXPERT_GROUPS):
            s = jnp.where(gidx == g, p[g * epg + k], s)
        sel.append(s)
    v1, i1 = sel[0], jnp.zeros(sel[0].shape, jnp.int32)
    for k in range(1, epg):
        take = sel[k] > v1
        v1 = jnp.where(take, sel[k], v1)
        i1 = jnp.where(take, k, i1)
    v2, i2 = jnp.full(v1.shape, -1.0, F32), jnp.zeros(v1.shape, jnp.int32)
    for k in range(epg):
        take = (i1 != k) & (sel[k] > v2)
        v2 = jnp.where(take, sel[k], v2)
        i2 = jnp.where(take, k, i2)
    tot = v1 + v2
    return gidx * epg + i1, gidx * epg + i2, v1 / tot, v2 / tot


def _router_kernel(x_ref, wr_ref, br_ref, comb_ref):
    lg = jnp.dot(x_ref[...], wr_ref[...], preferred_element_type=F32) + br_ref[...]
    m = jnp.max(lg, axis=-1, keepdims=True)
    ex = jnp.exp(lg - m)
    probs = ex / jnp.sum(ex, axis=-1, keepdims=True)
    n_e = probs.shape[-1]
    e1, e2, w1, w2 = _route([probs[:, e:e + 1] for e in range(n_e)])
    lane = lax.broadcasted_iota(jnp.int32, probs.shape, 1)
    comb_ref[...] = jnp.where(lane == e1, w1, 0.0) + jnp.where(lane == e2, w2, 0.0)


def _router(xb, wr, br, *, tm, name):
    t, d = xb.shape
    n_e = wr.shape[-1]
    tm = _pick(t, tm)
    return pl.pallas_call(
        _router_kernel,
        grid=(t // tm,),
        in_specs=[pl.BlockSpec((tm, d), lambda i: (i, 0)),
                  pl.BlockSpec((d, n_e), lambda i: (0, 0)),
                  pl.BlockSpec((1, n_e), lambda i: (0, 0))],
        out_specs=pl.BlockSpec((tm, n_e), lambda i: (i, 0)),
        out_shape=jax.ShapeDtypeStruct((t, n_e), F32),
        compiler_params=_params("parallel"),
        name=name,
    )(xb, wr, br)


def _moe_kernel(x_ref, comb_ref, wg_ref, wu_ref, wd_ref, o_ref):
    e = pl.program_id(1)
    x = x_ref[...]
    hg = jnp.dot(x, wg_ref[...], preferred_element_type=F32)
    hu = jnp.dot(x, wu_ref[...], preferred_element_type=F32)
    comb = comb_ref[...]
    lane = lax.broadcasted_iota(jnp.int32, comb.shape, 1)
    cw = jnp.sum(jnp.where(lane == e, comb, 0.0), axis=-1, keepdims=True)
    hid = (hg * _sigmoid(hg) * hu * cw).astype(BF16)
    y = jnp.dot(hid, wd_ref[...], preferred_element_type=F32)

    @pl.when(e == 0)
    def _():
        o_ref[...] = y

    @pl.when(e > 0)
    def _():
        o_ref[...] += y


def _moe_dense(xb, comb, wg, wu, wd, layer, *, tm, name):
    t, d = xb.shape
    n_e, _, f = wg.shape[1:]
    tm = _pick(t, tm)
    return pl.pallas_call(
        _moe_kernel,
        grid=(t // tm, n_e),
        in_specs=[pl.BlockSpec((tm, d), lambda i, e: (i, 0)),
                  pl.BlockSpec((tm, n_e), lambda i, e: (i, 0)),
                  pl.BlockSpec((None, None, d, f), lambda i, e: (layer, e, 0, 0)),
                  pl.BlockSpec((None, None, d, f), lambda i, e: (layer, e, 0, 0)),
                  pl.BlockSpec((None, None, f, d), lambda i, e: (layer, e, 0, 0))],
        out_specs=pl.BlockSpec((tm, d), lambda i, e: (i, 0)),
        out_shape=jax.ShapeDtypeStruct((t, d), F32),
        compiler_params=_params("parallel", "arbitrary"),
        name=name,
    )(xb, comb, wg, wu, wd)


def _resid_ln_kernel(x_ref, y_ref, g_ref, b_ref, o_ref, ob_ref, *, alpha):
    x3 = _layer_norm(alpha * x_ref[...] + y_ref[...], g_ref[...], b_ref[...])
    o_ref[...] = x3
    ob_ref[...] = x3.astype(BF16)


def _resid_ln(x, y, g, b, layer, alpha, *, tm, name):
    t, d = x.shape
    tm = _pick(t, tm)
    row = pl.BlockSpec((tm, d), lambda i: (i, 0))
    vec = pl.BlockSpec((None, 1, d), lambda i: (layer, 0, 0))
    return pl.pallas_call(
        functools.partial(_resid_ln_kernel, alpha=alpha),
        grid=(t // tm,),
        in_specs=[row, row, vec, vec],
        out_specs=[row, row],
        out_shape=[jax.ShapeDtypeStruct((t, d), F32), jax.ShapeDtypeStruct((t, d), BF16)],
        compiler_params=_params("parallel"),
        name=name,
    )(x, y, g, b)


def kernel(x, mem, w_in, conv_w, w_conv_proj, w_attn_proj, w_out, rel_bias, ln1_g, ln1_b, w_mem_q, w_mem_kv,
           w_mem_o, ln2_g, ln2_b, w_router, b_router, w_gate, w_up, w_down, ln3_g, ln3_b):
    batch, seq, d = x.shape
    depth = w_in.shape[0]
    t = batch * seq
    alpha = (2 * depth) ** 0.25
    cc = conv_w.shape[-1]
    n_groups = len(ATTN_GROUPS)
    aw = w_attn_proj.shape[1]
    hpg = aw // HEAD_DIM
    assert rel_bias.shape[1] == n_groups * hpg
    qkv0 = 3 * cc

    bf = lambda a: a.astype(BF16)
    w_in_b, wc_b, wa_b, wo_b = bf(w_in), bf(w_conv_proj), bf(w_attn_proj), bf(w_out)
    wq_b, wkv_b, wmo_b = bf(w_mem_q), bf(w_mem_kv), bf(w_mem_o)
    wr_b, wg_b, wu_b, wd_b = bf(w_router), bf(w_gate), bf(w_up), bf(w_down)
    br = b_router.reshape(1, -1).astype(F32)
    vec3 = lambda a: a.reshape(depth, 1, d)
    ln1_g, ln1_b, ln2_g, ln2_b, ln3_g, ln3_b = map(vec3, (ln1_g, ln1_b, ln2_g, ln2_b, ln3_g, ln3_b))
    tbl = rel_bias.T.astype(F32)
    mem_b = bf(mem).reshape(batch * mem.shape[1], d)

    xf = x.reshape(t, d)
    xb = bf(xf)
    for l in range(depth):
        big = _matmul(xb, w_in_b, l, BF16, tm=1024, tn=1024, sig_from_col=qkv0 + 3 * n_groups * aw,
                      name=f"inproj{l}")
        qkv = big[:, qkv0:qkv0 + 3 * n_groups * aw]
        os_, lses = [], []
        for g, (window, r) in enumerate(ATTN_GROUPS):
            lg = seq // r
            qg = qkv.reshape(batch, lg, r, 3, n_groups, aw)[:, :, :, :, g]
            qg = qg.transpose(3, 0, 2, 1, 4).reshape(3, batch * r, lg, aw)
            o_g, lse_g = _dilated_attention(qg, tbl[g * hpg:(g + 1) * hpg], batch, seq, window, r,
                                            name=f"dattn{l}_{g}")
            os_.append(o_g.reshape(batch, r, lg, aw).transpose(0, 2, 1, 3).reshape(t, aw))
            lses.append(lse_g.reshape(batch, r, lg, ATTN_BLOCK).transpose(0, 2, 1, 3).reshape(t, ATTN_BLOCK))
        merged = _merge(big, os_, lses, conv_w, wc_b, wa_b, l, seq, d, tm=512, tn=1024, name=f"merge{l}")
        s1 = _matmul(merged, wo_b, l, F32, tm=1024, tn=1024, resid=xf, alpha=alpha, name=f"outproj{l}")
        kv = _matmul(mem_b, wkv_b, l, BF16, tm=512, tn=1024, name=f"memkv{l}")
        kv = kv.reshape(batch, mem.shape[1], kv.shape[-1])
        x2, x2b = _xattn(s1, kv, ln1_g, ln1_b, wq_b, wmo_b, ln2_g, ln2_b, l, seq, alpha, tm=256, name=f"xattn{l}")
        comb = _router(x2b, wr_b, br, tm=1024, name=f"router{l}")
        ff = _moe_dense(x2b, comb, wg_b, wu_b, wd_b, l, tm=256, name=f"moe{l}")
        xf, xb = _resid_ln(x2, ff, ln3_g, ln3_b, l, alpha, tm=256, name=f"ln3_{l}")
    return xf.reshape(batch, seq, d)
```

```python
import functools
import math

import jax
import jax.numpy as jnp
from jax import lax
from jax.experimental import pallas as pl
from jax.experimental.pallas import tpu as pltpu

HEAD_DIM = 128
ATTN_GROUPS = ((128, 1), (512, 4), (2048, 16))
ATTN_BLOCK = 128
N_BUCKETS = 32
MAX_DISTANCE = 2048
MEM_HEAD_DIM = 128
N_EXPERT_GROUPS = 4
LN_EPS = 1e-5
NEG_INF = -1e30

VMEM_LIMIT_BYTES = 56 * 1024 * 1024
LANES = 128
ROUTE_ROWS = 8
EXPERT_TILE = 256

F32, BF16, U32, I32 = jnp.float32, jnp.bfloat16, jnp.uint32, jnp.int32


def _params(*sem):
    return pltpu.CompilerParams(dimension_semantics=sem, vmem_limit_bytes=VMEM_LIMIT_BYTES)


def _pick(n, pref):
    if n <= pref:
        return n
    t = pref
    while n % t:
        t //= 2
    return t


def _sigmoid(x):
    return 1.0 / (1.0 + jnp.exp(-x))


def _layer_norm(x, g, b):
    mu = jnp.mean(x, axis=-1, keepdims=True)
    xc = x - mu
    var = jnp.mean(xc * xc, axis=-1, keepdims=True)
    return xc * lax.rsqrt(var + LN_EPS) * g + b


def _pack_halves(x):
    half = x.shape[1] // 2
    hi = lax.bitcast_convert_type(x[:, :half].astype(BF16).astype(F32), U32)
    lo = lax.bitcast_convert_type(x[:, half:].astype(BF16).astype(F32), U32)
    return hi | (lo >> 16)


def _unpack_halves(p):
    hi = lax.bitcast_convert_type(p & jnp.uint32(0xFFFF0000), F32)
    lo = lax.bitcast_convert_type(p << 16, F32)
    return hi, lo


def _mm_kernel(*refs, sigmoid, alpha):
    if alpha is None:
        a_ref, w_ref, o_ref = refs
    else:
        a_ref, w_ref, r_ref, o_ref = refs
    acc = jnp.dot(a_ref[...], w_ref[...], preferred_element_type=F32)
    if alpha is not None:
        acc = acc + alpha * r_ref[...]
    if sigmoid:
        acc = _sigmoid(acc)
    o_ref[...] = acc.astype(o_ref.dtype)


def _matmul(a, w, layer, out_dtype, *, tm, tn, col0=0, ncols=None, sigmoid=False, resid=None, alpha=None, name):
    m, k = a.shape
    n = w.shape[-1] - col0 if ncols is None else ncols
    tm, tn = _pick(m, tm), _pick(n, tn)
    assert col0 % tn == 0
    cb = col0 // tn
    in_specs = [pl.BlockSpec((tm, k), lambda i, j: (i, 0)),
                pl.BlockSpec((None, k, tn), lambda i, j: (layer, 0, cb + j))]
    args = [a, w]
    if resid is not None:
        in_specs.append(pl.BlockSpec((tm, tn), lambda i, j: (i, j)))
        args.append(resid)
    return pl.pallas_call(
        functools.partial(_mm_kernel, sigmoid=sigmoid, alpha=alpha),
        grid=(m // tm, n // tn),
        in_specs=in_specs,
        out_specs=pl.BlockSpec((tm, tn), lambda i, j: (i, j)),
        out_shape=jax.ShapeDtypeStruct((m, n), out_dtype),
        compiler_params=_params("parallel", "arbitrary"),
        name=name,
    )(*args)


def _dattn_kernel(bucket_ref, tbl_ref, q_ref, k_ref, v_ref, kp_ref, vp_ref, o_ref, lse_ref,
                  bias_s, qs, kcat, vcat, *, r, pt, span, scale):
    blk = ATTN_BLOCK
    n = pl.program_id(1)
    h = pl.program_id(2)
    ch = pt // r
    nb = ch // blk

    @pl.when((pl.program_id(0) == 0) & (n == 0))
    def _():
        bk = bucket_ref[...]
        row = lax.broadcasted_iota(I32, (blk, 2 * blk), 0)
        col = lax.broadcasted_iota(I32, (blk, 2 * blk), 1)
        rel = row + blk - col
        band = (rel >= 0) & (rel <= span)
        bias = jnp.zeros((blk, 2 * blk), F32)
        for b in range(N_BUCKETS):
            bias = jnp.where(bk == b, tbl_ref[h, b], bias)
        bias_s[h, 0] = jnp.where(band, bias, NEG_INF)
        bias_s[h, 1] = jnp.where(band & (col >= blk), bias, NEG_INF)

    def rows(ref, start, size):
        if r == 1:
            return ref[pl.ds(start, size), :]
        return ref[pl.ds(start, size, stride=r), :]

    for rho in range(r):
        qs[...] = rows(q_ref, rho, ch).astype(BF16)
        kcat[0:blk] = rows(kp_ref, rho, blk).astype(BF16)
        kcat[blk:] = rows(k_ref, rho, ch).astype(BF16)
        vcat[0:blk] = rows(vp_ref, rho, blk).astype(BF16)
        vcat[blk:] = rows(v_ref, rho, ch).astype(BF16)
        for b in range(nb):
            first = jnp.where(n == 0, 1, 0) if b == 0 else 0
            q = qs[b * blk:(b + 1) * blk]
            k = kcat[b * blk:(b + 2) * blk]
            v = vcat[b * blk:(b + 2) * blk]
            lg = lax.dot_general(q, k, (((1,), (1,)), ((), ())), preferred_element_type=F32) * scale
            lg = lg + bias_s[h, first]
            m = jnp.max(lg, axis=-1, keepdims=True)
            p = jnp.exp(lg - m)
            l = jnp.sum(p, axis=-1, keepdims=True)
            o = jnp.dot(p.astype(BF16), v, preferred_element_type=F32) / l
            lse = jnp.broadcast_to(m + jnp.log(l), (blk, LANES))
            start = rho + b * blk * r
            if r == 1:
                o_ref[pl.ds(start, blk), :] = o
                lse_ref[pl.ds(start, blk), :] = lse
            else:
                o_ref[pl.ds(start, blk, stride=r), :] = o
                lse_ref[pl.ds(start, blk, stride=r), :] = lse


def _t5_bucket(dist):
    max_exact = N_BUCKETS // 2
    d = jnp.maximum(dist, 0)
    log_part = jnp.log(jnp.maximum(d, 1).astype(F32) / max_exact) / math.log(MAX_DISTANCE / max_exact)
    large = jnp.minimum(max_exact + (log_part * (N_BUCKETS - max_exact)).astype(I32), N_BUCKETS - 1)
    return jnp.where(d < max_exact, d, large)


def _dilated_attention(qkv, tbl, g, n_groups, hpg, batch, seq, window, r, name):
    blk = ATTN_BLOCK
    t = qkv.shape[0]
    pt = blk * ATTN_GROUPS[-1][1]
    assert seq % pt == 0 and pt % (blk * r) == 0 and HEAD_DIM == LANES
    tiles = seq // pt
    span = window // r
    qi = jnp.arange(blk)[:, None] + blk
    ki = jnp.arange(2 * blk)[None, :]
    bucket = _t5_bucket((qi - ki) * r).astype(I32)
    heads = n_groups * hpg
    prev_rows = blk * r
    ppt = pt // prev_rows

    def cur(which):
        return pl.BlockSpec((pt, LANES), lambda b, n, h: (b * tiles + n, which * heads + g * hpg + h))

    def prev(which):
        return pl.BlockSpec((prev_rows, LANES),
                            lambda b, n, h: (jnp.maximum((b * tiles + n) * ppt - 1, 0), which * heads + g * hpg + h))

    out = pl.BlockSpec((pt, LANES), lambda b, n, h: (b * tiles + n, h))
    return pl.pallas_call(
        functools.partial(_dattn_kernel, r=r, pt=pt, span=span, scale=HEAD_DIM ** -0.5),
        grid=(batch, tiles, hpg),
        in_specs=[pl.BlockSpec((blk, 2 * blk), lambda b, n, h: (0, 0)),
                  pl.BlockSpec(memory_space=pltpu.SMEM),
                  cur(0), cur(1), cur(2), prev(1), prev(2)],
        out_specs=[out, out],
        out_shape=[jax.ShapeDtypeStruct((t, hpg * LANES), F32)] * 2,
        scratch_shapes=[pltpu.VMEM((hpg, 2, blk, 2 * blk), F32),
                        pltpu.VMEM((pt // r, LANES), BF16),
                        pltpu.VMEM((pt // r + blk, LANES), BF16),
                        pltpu.VMEM((pt // r + blk, LANES), BF16)],
        compiler_params=_params("arbitrary", "arbitrary", "arbitrary"),
        name=name,
    )(bucket, tbl, qkv, qkv, qkv, qkv, qkv)


def _merge_kernel(*refs, n_groups, conv_k, tm, halo, seq_tiles):
    (b_ref, c_ref, h_ref, cp_ref, hp_ref, cw_ref) = refs[:6]
    o_refs = refs[6:6 + n_groups]
    lse_refs = refs[6 + n_groups:6 + 2 * n_groups]
    sgc_ref, sga_ref, wc_ref, wa_ref, out_ref, u_s, aconv_s, oatt_s = refs[6 + 2 * n_groups:]
    i = pl.program_id(0)

    @pl.when(pl.program_id(1) == 0)
    def _():
        seq_start = (i % seq_tiles) == 0
        up = cp_ref[...].astype(F32) * hp_ref[...].astype(F32)
        u_s[0:halo] = jnp.where(seq_start, 0.0, up)
        u_s[halo:] = c_ref[...].astype(F32) * h_ref[...].astype(F32)
        conv = jnp.zeros((tm, u_s.shape[1]), F32)
        for j in range(conv_k):
            off = halo - (conv_k - 1) + j
            conv = conv + cw_ref[j:j + 1, :] * u_s[off:off + tm]
        aconv_s[...] = (b_ref[...].astype(F32) * conv).astype(BF16)
        ls = [ref[...] for ref in lse_refs]
        mx = functools.reduce(jnp.maximum, ls)
        es = [jnp.exp(x - mx) for x in ls]
        den = functools.reduce(lambda a, b: a + b, es)
        acc = es[0] * o_refs[0][...]
        for e, o_ref in zip(es[1:], o_refs[1:]):
            acc = acc + e * o_ref[...]
        oatt_s[...] = (acc / den).astype(BF16)

    yc = jnp.dot(aconv_s[...], wc_ref[...], preferred_element_type=F32)
    ya = jnp.dot(oatt_s[...], wa_ref[...], preferred_element_type=F32)
    out_ref[...] = (sgc_ref[...].astype(F32) * yc + sga_ref[...].astype(F32) * ya).astype(out_ref.dtype)


def _merge(cin, gates, os_, lses, conv_w, wc, wa, layer, seq, d_model, *, tm, tn, name):
    t = cin.shape[0]
    conv_k, cc = conv_w.shape[-2:]
    aw = os_[0].shape[-1]
    n_groups = len(os_)
    tm, tn = _pick(seq, tm), _pick(d_model, tn)
    halo = 16
    assert cc % LANES == 0 and tm % halo == 0 and conv_k - 1 <= halo
    nj = d_model // tn

    def colblk(c):
        return pl.BlockSpec((tm, cc), lambda i, j: (i, c))

    def halo_blk(c):
        return pl.BlockSpec((halo, cc), lambda i, j: (jnp.maximum(i * (tm // halo) - 1, 0), c))

    row_aw = pl.BlockSpec((tm, aw), lambda i, j: (i, 0))
    in_specs = ([colblk(0), colblk(1), colblk(2), halo_blk(1), halo_blk(2),
                 pl.BlockSpec((None, conv_k, cc), lambda i, j: (layer, 0, 0))]
                + [row_aw] * (2 * n_groups)
                + [pl.BlockSpec((tm, tn), lambda i, j: (i, j)),
                   pl.BlockSpec((tm, tn), lambda i, j: (i, nj + j)),
                   pl.BlockSpec((None, cc, tn), lambda i, j: (layer, 0, j)),
                   pl.BlockSpec((None, aw, tn), lambda i, j: (layer, 0, j))])
    return pl.pallas_call(
        functools.partial(_merge_kernel, n_groups=n_groups, conv_k=conv_k, tm=tm, halo=halo, seq_tiles=seq // tm),
        grid=(t // tm, nj),
        in_specs=in_specs,
        out_specs=pl.BlockSpec((tm, tn), lambda i, j: (i, j)),
        out_shape=jax.ShapeDtypeStruct((t, d_model), BF16),
        scratch_shapes=[pltpu.VMEM((tm + halo, cc), F32), pltpu.VMEM((tm, cc), BF16), pltpu.VMEM((tm, aw), BF16)],
        compiler_params=_params("parallel", "arbitrary"),
        name=name,
    )(cin, cin, cin, cin, cin, conv_w, *os_, *lses, gates, gates, wc, wa)


def _route(p):
    epg = len(p) // N_EXPERT_GROUPS
    assert epg == 4
    scores = []
    for g in range(N_EXPERT_GROUPS):
        a, b, c, d = p[g * epg:(g + 1) * epg]
        hi1, lo1, hi2, lo2 = jnp.maximum(a, b), jnp.minimum(a, b), jnp.maximum(c, d), jnp.minimum(c, d)
        scores.append(jnp.maximum(hi1, hi2) + jnp.maximum(jnp.minimum(hi1, hi2), jnp.maximum(lo1, lo2)))
    best, gidx = scores[0], jnp.zeros(scores[0].shape, I32)
    for g in range(1, N_EXPERT_GROUPS):
        take = scores[g] > best
        best = jnp.where(take, scores[g], best)
        gidx = jnp.where(take, g, gidx)
    sel = []
    for k in range(epg):
        s = p[k]
        for g in range(1, N_EXPERT_GROUPS):
            s = jnp.where(gidx == g, p[g * epg + k], s)
        sel.append(s)
    v1, i1 = sel[0], jnp.zeros(sel[0].shape, I32)
    for k in range(1, epg):
        take = sel[k] > v1
        v1 = jnp.where(take, sel[k], v1)
        i1 = jnp.where(take, k, i1)
    v2, i2 = jnp.full(v1.shape, -1.0, F32), jnp.zeros(v1.shape, I32)
    for k in range(epg):
        take = (i1 != k) & (sel[k] > v2)
        v2 = jnp.where(take, sel[k], v2)
        i2 = jnp.where(take, k, i2)
    tot = v1 + v2
    return gidx * epg + i1, gidx * epg + i2, v1 / tot, v2 / tot


def _xattn_kernel(s1_ref, g1_ref, b1_ref, wq_ref, kv_ref, wo_ref, g2_ref, b2_ref, wrt_ref, br_ref,
                  x_ref, xp_ref, rw_ref, ridx_ref, cnt_ref, tril_s, carry_s, *, heads, alpha, scale, tm):
    i = pl.program_id(0)
    x1 = _layer_norm(s1_ref[...], g1_ref[...], b1_ref[...])
    q = jnp.dot(x1.astype(BF16), wq_ref[...], preferred_element_type=F32).astype(BF16)
    mw = heads * MEM_HEAD_DIM
    outs = []
    for h in range(heads):
        cs = slice(h * MEM_HEAD_DIM, (h + 1) * MEM_HEAD_DIM)
        k = kv_ref[:, cs]
        v = kv_ref[:, mw + h * MEM_HEAD_DIM:mw + (h + 1) * MEM_HEAD_DIM]
        lg = lax.dot_general(q[:, cs], k, (((1,), (1,)), ((), ())), preferred_element_type=F32) * scale
        m = jnp.max(lg, axis=-1, keepdims=True)
        p = jnp.exp(lg - m)
        l = jnp.sum(p, axis=-1, keepdims=True)
        outs.append((jnp.dot(p.astype(BF16), v, preferred_element_type=F32) / l).astype(BF16))
    o = jnp.concatenate(outs, axis=-1)
    xa = jnp.dot(o, wo_ref[...], preferred_element_type=F32)
    x2 = _layer_norm(alpha * x1 + xa, g2_ref[...], b2_ref[...])
    x_ref[...] = x2
    xp_ref[...] = _pack_halves(x2)

    half = x2.shape[1] // 2
    nt = (((1,), (1,)), ((), ()))
    lg = (lax.dot_general(x2[:, :half].astype(BF16), wrt_ref[:, :half], nt, preferred_element_type=F32)
          + lax.dot_general(x2[:, half:].astype(BF16), wrt_ref[:, half:], nt, preferred_element_type=F32)
          + br_ref[...])
    m = jnp.max(lg, axis=-1, keepdims=True)
    ex = jnp.exp(lg - m)
    probs = ex / jnp.sum(ex, axis=-1, keepdims=True)
    n_e = probs.shape[-1]
    e1, e2, w1, w2 = _route([probs[:, e:e + 1] for e in range(n_e)])
    lane = lax.broadcasted_iota(I32, (tm, LANES), 1)
    rw_ref[...] = jnp.where(lane == 0, w1, jnp.where(lane == 1, w2, 0.0))

    @pl.when(i == 0)
    def _():
        rr = lax.broadcasted_iota(I32, (tm, tm), 0)
        cc = lax.broadcasted_iota(I32, (tm, tm), 1)
        tril_s[...] = jnp.where(cc <= rr, 1.0, 0.0).astype(BF16)
        carry_s[...] = jnp.zeros(carry_s.shape, F32)

    lane_e = lax.broadcasted_iota(I32, (tm, n_e), 1)
    onehot = jnp.where((lane_e == e1) | (lane_e == e2), 1.0, 0.0)
    incl = jnp.dot(tril_s[...], onehot.astype(BF16), preferred_element_type=F32)
    rank = incl - onehot + carry_s[...]
    r1 = jnp.sum(jnp.where(lane_e == e1, rank, 0.0), axis=-1, keepdims=True).astype(I32)
    r2 = jnp.sum(jnp.where(lane_e == e2, rank, 0.0), axis=-1, keepdims=True).astype(I32)
    carry_s[...] = carry_s[...] + incl[tm - 1:tm, :]
    cnt_ref[...] = carry_s[...]

    lane8 = lax.broadcasted_iota(I32, (tm, ROUTE_ROWS), 1)
    cols = [e1, e2, r1 >> 8, r1 & 255, r2 >> 8, r2 & 255]
    tok = jnp.zeros((tm, ROUTE_ROWS), F32)
    for c, val in enumerate(cols):
        tok = jnp.where(lane8 == c, val.astype(F32), tok)
    ro = lax.broadcasted_iota(I32, (ROUTE_ROWS, ROUTE_ROWS), 0)
    co = lax.broadcasted_iota(I32, (ROUTE_ROWS, ROUTE_ROWS), 1)
    mix = jnp.where(((ro == 0) & (co == 0)) | ((ro == 1) & (co == 1)) | ((ro == 2) & (co == 3)) | ((ro == 3) & (co == 5)),
                    1.0, jnp.where(((ro == 2) & (co == 2)) | ((ro == 3) & (co == 4)), 256.0, 0.0))
    ridx_ref[...] = lax.dot_general(mix.astype(BF16), tok.astype(BF16), nt, preferred_element_type=F32).astype(I32)


def _xattn(s1, kv, g1, b1, wq, wo, g2, b2, wrt, br, layer, seq, alpha, *, tm, name):
    t, d = s1.shape
    tm = _pick(seq, tm)
    n_mem, kvw = kv.shape[1:]
    mw = wq.shape[-1]
    n_e = wrt.shape[0]
    heads = mw // MEM_HEAD_DIM
    assert 2 * t + n_e * EXPERT_TILE < 65536
    vec = pl.BlockSpec((None, 1, d), lambda i: (layer, 0, 0))
    once = pl.Buffered(1)
    row = lambda w: pl.BlockSpec((tm, w), lambda i: (i, 0))
    return pl.pallas_call(
        functools.partial(_xattn_kernel, heads=heads, alpha=alpha, scale=MEM_HEAD_DIM ** -0.5, tm=tm),
        grid=(t // tm,),
        in_specs=[row(d), vec, vec,
                  pl.BlockSpec((None, d, mw), lambda i: (layer, 0, 0), pipeline_mode=once),
                  pl.BlockSpec((None, n_mem, kvw), lambda i: (i // (seq // tm), 0, 0)),
                  pl.BlockSpec((None, mw, d), lambda i: (layer, 0, 0), pipeline_mode=once), vec, vec,
                  pl.BlockSpec((n_e, d), lambda i: (0, 0)),
                  pl.BlockSpec((1, n_e), lambda i: (0, 0))],
        out_specs=[row(d), row(d // 2), row(LANES),
                   pl.BlockSpec((ROUTE_ROWS, tm), lambda i: (0, i)),
                   pl.BlockSpec((1, n_e), lambda i: (0, 0))],
        out_shape=[jax.ShapeDtypeStruct((t, d), F32), jax.ShapeDtypeStruct((t, d // 2), U32),
                   jax.ShapeDtypeStruct((t, LANES), F32), jax.ShapeDtypeStruct((ROUTE_ROWS, t), I32),
                   jax.ShapeDtypeStruct((1, n_e), F32)],
        scratch_shapes=[pltpu.VMEM((tm, tm), BF16), pltpu.VMEM((1, n_e), F32)],
        compiler_params=_params("arbitrary"),
        name=name,
    )(s1, g1, b1, wq, kv, wo, g2, b2, wrt, br)


def _row_copy(src, s, dst, p, sem):
    return pltpu.make_async_copy(src.at[pl.ds(s, 1), :], dst.at[pl.ds(p, 1), :], sem)


def _dispatch_kernel(off_ref, cnt_ref, pad_ref, na_ref, x_ref, idx_ref, xs_hbm, sem, *, tm, n_e, n_tiles):
    def body(s, c):
        _row_copy(x_ref, s, xs_hbm, off_ref[idx_ref[0, s]] + idx_ref[2, s], sem).start()
        _row_copy(x_ref, s, xs_hbm, off_ref[idx_ref[1, s]] + idx_ref[3, s], sem).start()
        return c

    lax.fori_loop(0, tm, body, 0)

    @pl.when(pl.program_id(0) == 0)
    def _():
        for e in range(n_e):
            base = off_ref[e] + cnt_ref[e]
            n_pad = pad_ref[e] - cnt_ref[e]

            def fill(s, c):
                _row_copy(x_ref, 0, xs_hbm, base + s, sem).start()
                return c

            def drain(s, c):
                _row_copy(x_ref, 0, xs_hbm, base, sem).wait()
                return c

            lax.fori_loop(0, n_pad, fill, 0)
            lax.fori_loop(0, n_pad, drain, 0)

        def tail_copy(i):
            return pltpu.make_async_copy(x_ref.at[pl.ds(0, EXPERT_TILE), :],
                                         xs_hbm.at[pl.ds(i * EXPERT_TILE, EXPERT_TILE), :], sem)

        def tail_fill(i, c):
            tail_copy(i).start()
            return c

        def tail_drain(i, c):
            tail_copy(i).wait()
            return c

        lax.fori_loop(na_ref[0], n_tiles, tail_fill, 0)
        lax.fori_loop(na_ref[0], n_tiles, tail_drain, 0)

    pltpu.make_async_copy(xs_hbm.at[pl.ds(0, 2 * tm), :], xs_hbm.at[pl.ds(0, 2 * tm), :], sem).wait()


def _dispatch(xp, ridx, off, cnt, padded, na, n_slots, *, tm, name):
    t, w = xp.shape
    tm = _pick(t, tm)
    n_e = off.shape[0]
    assert tm >= EXPERT_TILE
    return pl.pallas_call(
        functools.partial(_dispatch_kernel, tm=tm, n_e=n_e, n_tiles=n_slots // EXPERT_TILE),
        grid_spec=pltpu.PrefetchScalarGridSpec(
            num_scalar_prefetch=4, grid=(t // tm,),
            in_specs=[pl.BlockSpec((tm, w), lambda i, *_: (i, 0)),
                      pl.BlockSpec((ROUTE_ROWS, tm), lambda i, *_: (0, i), memory_space=pltpu.SMEM)],
            out_specs=pl.BlockSpec(memory_space=pl.ANY),
            scratch_shapes=[pltpu.SemaphoreType.DMA(())]),
        out_shape=jax.ShapeDtypeStruct((n_slots, w), U32),
        compiler_params=_params("arbitrary"),
        name=name,
    )(off, cnt, padded, na, xp, ridx)


def _experts_kernel(te_ref, na_ref, xs_ref, wg_ref, wu_ref, wd_ref, ys_ref):
    @pl.when(pl.program_id(0) < na_ref[0])
    def _():
        hi, lo = _unpack_halves(xs_ref[...])
        hi, lo = hi.astype(BF16), lo.astype(BF16)
        half = hi.shape[1]
        hg = (jnp.dot(hi, wg_ref[:half], preferred_element_type=F32)
              + jnp.dot(lo, wg_ref[half:], preferred_element_type=F32))
        hu = (jnp.dot(hi, wu_ref[:half], preferred_element_type=F32)
              + jnp.dot(lo, wu_ref[half:], preferred_element_type=F32))
        hid = (hg * _sigmoid(hg) * hu).astype(BF16)
        ys_ref[...] = _pack_halves(jnp.dot(hid, wd_ref[...], preferred_element_type=F32))

    @pl.when(pl.program_id(0) >= na_ref[0])
    def _():
        ys_ref[...] = jnp.zeros(ys_ref.shape, ys_ref.dtype)


def _experts(xs, te, na, wg, wu, wd, layer, *, name):
    n_slots, w = xs.shape
    d, f = wg.shape[2:]
    tm = EXPERT_TILE
    nt = n_slots // tm

    def tile(i, te_ref, na_ref):
        return (jnp.minimum(i, na_ref[0] - 1), 0)

    def wmap(i, te_ref, na_ref):
        return (layer, te_ref[jnp.minimum(i, na_ref[0] - 1)], 0, 0)

    return pl.pallas_call(
        _experts_kernel,
        grid_spec=pltpu.PrefetchScalarGridSpec(
            num_scalar_prefetch=2, grid=(nt,),
            in_specs=[pl.BlockSpec((tm, w), tile),
                      pl.BlockSpec((None, None, d, f), wmap),
                      pl.BlockSpec((None, None, d, f), wmap),
                      pl.BlockSpec((None, None, f, d), wmap)],
            out_specs=pl.BlockSpec((tm, w), lambda i, te_ref, na_ref: (i, 0))),
        out_shape=jax.ShapeDtypeStruct((n_slots, w), U32),
        compiler_params=_params("arbitrary"),
        name=name,
    )(te, na, xs, wg, wu, wd)


def _combine_kernel(off_ref, idx_ref, nidx_ref, x_ref, rw_ref, g_ref, b_ref, ys_hbm, o_ref, ob_ref, buf, sem, *,
                    tm, alpha):
    i = pl.program_id(0)
    n = pl.num_programs(0)

    def issue(ref, slot):
        def body(s, c):
            _row_copy(ys_hbm, off_ref[ref[0, s]] + ref[2, s], buf.at[slot], s, sem.at[slot]).start()
            _row_copy(ys_hbm, off_ref[ref[1, s]] + ref[3, s], buf.at[slot], tm + s, sem.at[slot]).start()
            return c

        lax.fori_loop(0, tm, body, 0)

    @pl.when(i == 0)
    def _():
        issue(idx_ref, 0)

    @pl.when(i + 1 < n)
    def _():
        issue(nidx_ref, (i + 1) % 2)

    slot = i % 2
    pltpu.make_async_copy(ys_hbm.at[pl.ds(0, 2 * tm), :], buf.at[slot], sem.at[slot]).wait()
    hi, lo = _unpack_halves(buf[slot])
    w1 = rw_ref[:, 0:1]
    w2 = rw_ref[:, 1:2]
    half = hi.shape[1]
    s_hi = alpha * x_ref[:, :half] + (w1 * hi[:tm] + w2 * hi[tm:])
    s_lo = alpha * x_ref[:, half:] + (w1 * lo[:tm] + w2 * lo[tm:])
    d = 2 * half
    mu = (jnp.sum(s_hi, axis=-1, keepdims=True) + jnp.sum(s_lo, axis=-1, keepdims=True)) / d
    c_hi, c_lo = s_hi - mu, s_lo - mu
    var = (jnp.sum(c_hi * c_hi, axis=-1, keepdims=True) + jnp.sum(c_lo * c_lo, axis=-1, keepdims=True)) / d
    inv = lax.rsqrt(var + LN_EPS)
    y_hi = c_hi * inv * g_ref[:, :half] + b_ref[:, :half]
    y_lo = c_lo * inv * g_ref[:, half:] + b_ref[:, half:]
    o_ref[:, :half] = y_hi
    o_ref[:, half:] = y_lo
    ob_ref[:, :half] = y_hi.astype(BF16)
    ob_ref[:, half:] = y_lo.astype(BF16)


def _combine(x2, rw, ridx, off, ys, g, b, layer, alpha, *, tm, name):
    t, d = x2.shape
    tm = _pick(t, tm)
    nsteps = t // tm
    w = ys.shape[1]
    row = lambda width: pl.BlockSpec((tm, width), lambda i, *_: (i, 0))
    vec = pl.BlockSpec((None, 1, d), lambda i, *_: (layer, 0, 0))
    return pl.pallas_call(
        functools.partial(_combine_kernel, tm=tm, alpha=alpha),
        grid_spec=pltpu.PrefetchScalarGridSpec(
            num_scalar_prefetch=1, grid=(nsteps,),
            in_specs=[pl.BlockSpec((ROUTE_ROWS, tm), lambda i, *_: (0, i), memory_space=pltpu.SMEM),
                      pl.BlockSpec((ROUTE_ROWS, tm), lambda i, *_: (0, jnp.minimum(i + 1, nsteps - 1)),
                                   memory_space=pltpu.SMEM),
                      row(d), row(LANES), vec, vec,
                      pl.BlockSpec(memory_space=pl.ANY)],
            out_specs=[row(d), row(d)],
            scratch_shapes=[pltpu.VMEM((2, 2 * tm, w), U32), pltpu.SemaphoreType.DMA((2,))]),
        out_shape=[jax.ShapeDtypeStruct((t, d), F32), jax.ShapeDtypeStruct((t, d), BF16)],
        compiler_params=_params("arbitrary"),
        name=name,
    )(off, ridx, ridx, x2, rw, g, b, ys)


def _expert_layout(counts, n_tiles):
    cnt = counts.reshape(-1).astype(I32)
    padded = (cnt + EXPERT_TILE - 1) // EXPERT_TILE * EXPERT_TILE
    ends = jnp.cumsum(padded)
    off = ends - padded
    tile_ends = ends // EXPERT_TILE
    na = tile_ends[-1:]
    tiles = jnp.minimum(jnp.arange(n_tiles, dtype=I32), na[0] - 1)
    te = jnp.sum((tiles[:, None] >= tile_ends[None, :]).astype(I32), axis=1)
    return off, cnt, padded, te, na


def kernel(x, mem, w_in, conv_w, w_conv_proj, w_attn_proj, w_out, rel_bias, ln1_g, ln1_b, w_mem_q, w_mem_kv,
           w_mem_o, ln2_g, ln2_b, w_router, b_router, w_gate, w_up, w_down, ln3_g, ln3_b):
    batch, seq, d = x.shape
    depth = w_in.shape[0]
    t = batch * seq
    alpha = (2 * depth) ** 0.25
    cc = conv_w.shape[-1]
    n_groups = len(ATTN_GROUPS)
    aw = w_attn_proj.shape[1]
    hpg = aw // HEAD_DIM
    n_e = w_router.shape[1]
    assert rel_bias.shape[1] == n_groups * hpg
    qkv_w = 3 * n_groups * aw
    n_tiles = -(-(2 * t) // EXPERT_TILE) + n_e
    n_slots = n_tiles * EXPERT_TILE

    bf = lambda a: a.astype(BF16)
    w_in_b, wc_b, wa_b, wo_b = bf(w_in), bf(w_conv_proj), bf(w_attn_proj), bf(w_out)
    wq_b, wkv_b, wmo_b = bf(w_mem_q), bf(w_mem_kv), bf(w_mem_o)
    wrt_b, wg_b, wu_b, wd_b = bf(w_router.T), bf(w_gate), bf(w_up), bf(w_down)
    br = b_router.reshape(1, -1).astype(F32)
    vec3 = lambda a: a.reshape(depth, 1, d)
    ln1_g, ln1_b, ln2_g, ln2_b, ln3_g, ln3_b = map(vec3, (ln1_g, ln1_b, ln2_g, ln2_b, ln3_g, ln3_b))
    tbl = rel_bias.T.astype(F32)
    mem_b = bf(mem).reshape(batch * mem.shape[1], d)

    xf = x.reshape(t, d)
    xb = bf(xf)
    for l in range(depth):
        cin = _matmul(xb, w_in_b, l, BF16, tm=1024, tn=768, col0=0, ncols=3 * cc, name=f"inproj_conv{l}")
        qkv = _matmul(xb, w_in_b, l, F32, tm=1024, tn=768, col0=3 * cc, ncols=qkv_w, name=f"inproj_qkv{l}")
        gates = _matmul(xb, w_in_b, l, BF16, tm=1024, tn=1024, col0=3 * cc + qkv_w, ncols=2 * d, sigmoid=True,
                        name=f"inproj_gate{l}")
        os_, lses = [], []
        for g, (window, r) in enumerate(ATTN_GROUPS):
            o_g, lse_g = _dilated_attention(qkv, tbl[g * hpg:(g + 1) * hpg], g, n_groups, hpg, batch, seq, window, r,
                                            name=f"dattn{l}_{g}")
            os_.append(o_g)
            lses.append(lse_g)
        merged = _merge(cin, gates, os_, lses, conv_w, wc_b, wa_b, l, seq, d, tm=512, tn=1024, name=f"merge{l}")
        s1 = _matmul(merged, wo_b, l, F32, tm=1024, tn=1024, resid=xf, alpha=alpha, name=f"outproj{l}")
        kv = _matmul(mem_b, wkv_b, l, BF16, tm=512, tn=1024, name=f"memkv{l}")
        kv = kv.reshape(batch, mem.shape[1], kv.shape[-1])
        x2, x2p, rw, ridx, counts = _xattn(s1, kv, ln1_g, ln1_b, wq_b, wmo_b, ln2_g, ln2_b, wrt_b, br, l, seq, alpha,
                                           tm=256, name=f"xattn{l}")
        off, cnt, padded, te, na = _expert_layout(counts, n_tiles)
        xs = _dispatch(x2p, ridx, off, cnt, padded, na, n_slots, tm=512, name=f"dispatch{l}")
        ys = _experts(xs, te, na, wg_b, wu_b, wd_b, l, name=f"experts{l}")
        xf, xb = _combine(x2, rw, ridx, off, ys, ln3_g, ln3_b, l, alpha, tm=256, name=f"combine{l}")
    return xf.reshape(batch, seq, d)
```

```python
import functools
import math

import jax
import jax.numpy as jnp
from jax import lax
from jax.experimental import pallas as pl
from jax.experimental.pallas import tpu as pltpu

HEAD_DIM = 128
ATTN_GROUPS = ((128, 1), (512, 4), (2048, 16))
ATTN_BLOCK = 128
N_BUCKETS = 32
MAX_DISTANCE = 2048
MEM_HEAD_DIM = 128
N_EXPERT_GROUPS = 4
LN_EPS = 1e-5
NEG_INF = -1e30

VMEM_LIMIT_BYTES = 56 * 1024 * 1024
LANES = 128
ROUTE_ROWS = 8
EXPERT_TILE = 256
LN_ROWS = 64
MERGE_ROWS = 32
CONV_ROWS = 64
DMA_UNROLL = 8

F32, BF16, U32, I32 = jnp.float32, jnp.bfloat16, jnp.uint32, jnp.int32


def _params(*sem):
    return pltpu.CompilerParams(dimension_semantics=sem, vmem_limit_bytes=VMEM_LIMIT_BYTES)


def _pick(n, pref):
    if n <= pref:
        return n
    t = pref
    while n % t:
        t //= 2
    return t


def _sigmoid(x):
    return 1.0 / (1.0 + jnp.exp(-x))


def _layer_norm(x, g, b):
    mu = jnp.mean(x, axis=-1, keepdims=True)
    xc = x - mu
    var = jnp.mean(xc * xc, axis=-1, keepdims=True)
    return xc * lax.rsqrt(var + LN_EPS) * g + b


def _pack_halves(x):
    half = x.shape[1] // 2
    hi = lax.bitcast_convert_type(x[:, :half].astype(BF16).astype(F32), U32)
    lo = lax.bitcast_convert_type(x[:, half:].astype(BF16).astype(F32), U32)
    return hi | (lo >> 16)


def _unpack_halves(p):
    hi = lax.bitcast_convert_type(p & jnp.uint32(0xFFFF0000), F32)
    lo = lax.bitcast_convert_type(p << 16, F32)
    return hi, lo


def _mm_kernel(*refs, sigmoid, alpha):
    if alpha is None:
        a_ref, w_ref, o_ref, wb_s = refs
    else:
        a_ref, w_ref, r_ref, o_ref, wb_s = refs

    @pl.when(pl.program_id(1) == 0)
    def _():
        wb_s[...] = w_ref[...].astype(BF16)

    acc = jnp.dot(a_ref[...], wb_s[...], preferred_element_type=F32)
    if alpha is not None:
        acc = acc + alpha * r_ref[...]
    if sigmoid:
        acc = _sigmoid(acc)
    o_ref[...] = acc.astype(o_ref.dtype)


def _matmul(a, w, layer, out_dtype, *, tm, tn, col0=0, ncols=None, sigmoid=False, resid=None, alpha=None, name):
    m, k = a.shape
    n = w.shape[-1] - col0 if ncols is None else ncols
    tm, tn = _pick(m, tm), _pick(n, tn)
    assert col0 % tn == 0
    cb = col0 // tn
    in_specs = [pl.BlockSpec((tm, k), lambda j, i: (i, 0)),
                pl.BlockSpec((None, k, tn), lambda j, i: (layer, 0, cb + j))]
    args = [a, w]
    if resid is not None:
        in_specs.append(pl.BlockSpec((tm, tn), lambda j, i: (i, j)))
        args.append(resid)
    return pl.pallas_call(
        functools.partial(_mm_kernel, sigmoid=sigmoid, alpha=alpha),
        grid=(n // tn, m // tm),
        in_specs=in_specs,
        out_specs=pl.BlockSpec((tm, tn), lambda j, i: (i, j)),
        out_shape=jax.ShapeDtypeStruct((m, n), out_dtype),
        scratch_shapes=[pltpu.VMEM((k, tn), BF16)],
        compiler_params=_params("arbitrary", "arbitrary"),
        name=name,
    )(*args)


def _dattn_kernel(bucket_ref, tbl_ref, q_ref, k_ref, v_ref, kp_ref, vp_ref, *rest, r, pt, span, scale, n_other):
    others = rest[:2 * n_other]
    if n_other:
        oatt_ref, bias_s, qs, kcat, vcat, o_ref, lse_ref = rest[2 * n_other:]
    else:
        o_ref, lse_ref, bias_s, qs, kcat, vcat = rest
    blk = ATTN_BLOCK
    n = pl.program_id(1)
    h = pl.program_id(2)
    ch = pt // r
    nb = ch // blk

    @pl.when((pl.program_id(0) == 0) & (n == 0))
    def _():
        bk = bucket_ref[...]
        row = lax.broadcasted_iota(I32, (blk, 2 * blk), 0)
        col = lax.broadcasted_iota(I32, (blk, 2 * blk), 1)
        rel = row + blk - col
        band = (rel >= 0) & (rel <= span)
        bias = jnp.zeros((blk, 2 * blk), F32)
        for b in range(N_BUCKETS):
            bias = jnp.where(bk == b, tbl_ref[h, b], bias)
        bias_s[h, 0] = jnp.where(band, bias, NEG_INF)
        bias_s[h, 1] = jnp.where(band & (col >= blk), bias, NEG_INF)

    def rows(ref, start, size):
        if r == 1:
            return ref[pl.ds(start, size), :]
        return ref[pl.ds(start, size, stride=r), :]

    for rho in range(r):
        qs[...] = rows(q_ref, rho, ch).astype(BF16)
        kcat[0:blk] = rows(kp_ref, rho, blk).astype(BF16)
        kcat[blk:] = rows(k_ref, rho, ch).astype(BF16)
        vcat[0:blk] = rows(vp_ref, rho, blk).astype(BF16)
        vcat[blk:] = rows(v_ref, rho, ch).astype(BF16)
        for b in range(nb):
            first = jnp.where(n == 0, 1, 0) if b == 0 else 0
            q = qs[b * blk:(b + 1) * blk]
            k = kcat[b * blk:(b + 2) * blk]
            v = vcat[b * blk:(b + 2) * blk]
            lg = lax.dot_general(q, k, (((1,), (1,)), ((), ())), preferred_element_type=F32) * scale
            lg = lg + bias_s[h, first]
            m = jnp.max(lg, axis=-1, keepdims=True)
            p = jnp.exp(lg - m)
            l = jnp.sum(p, axis=-1, keepdims=True)
            o = jnp.dot(p.astype(BF16), v, preferred_element_type=F32) / l
            lse = jnp.broadcast_to(m + jnp.log(l), (blk, LANES))
            start = rho + b * blk * r
            if r == 1:
                o_ref[pl.ds(start, blk), :] = o
                lse_ref[pl.ds(start, blk), :] = lse
            else:
                o_ref[pl.ds(start, blk, stride=r), :] = o
                lse_ref[pl.ds(start, blk, stride=r), :] = lse

    if n_other:
        rc = MERGE_ROWS

        def merge_rows(c, carry):
            sl = pl.ds(pl.multiple_of(c * rc, rc), rc)
            ls = [others[2 * g + 1][sl, :] for g in range(n_other)] + [lse_ref[sl, :]]
            os_ = [others[2 * g][sl, :] for g in range(n_other)] + [o_ref[sl, :]]
            mx = functools.reduce(jnp.maximum, ls)
            es = [jnp.exp(x - mx) for x in ls]
            den = functools.reduce(lambda a, b: a + b, es)
            acc = es[0] * os_[0]
            for e, o in zip(es[1:], os_[1:]):
                acc = acc + e * o
            oatt_ref[sl, :] = (acc / den).astype(oatt_ref.dtype)
            return carry

        lax.fori_loop(0, pt // rc, merge_rows, 0)


def _t5_bucket(dist):
    max_exact = N_BUCKETS // 2
    d = jnp.maximum(dist, 0)
    log_part = jnp.log(jnp.maximum(d, 1).astype(F32) / max_exact) / math.log(MAX_DISTANCE / max_exact)
    large = jnp.minimum(max_exact + (log_part * (N_BUCKETS - max_exact)).astype(I32), N_BUCKETS - 1)
    return jnp.where(d < max_exact, d, large)


def _dilated_attention(qkv, tbl, g, n_groups, hpg, batch, seq, window, r, others, name):
    blk = ATTN_BLOCK
    t = qkv.shape[0]
    pt = blk * ATTN_GROUPS[-1][1]
    assert seq % pt == 0 and pt % (blk * r) == 0 and HEAD_DIM == LANES
    tiles = seq // pt
    span = window // r
    qi = jnp.arange(blk)[:, None] + blk
    ki = jnp.arange(2 * blk)[None, :]
    bucket = _t5_bucket((qi - ki) * r).astype(I32)
    heads = n_groups * hpg
    prev_rows = blk * r
    ppt = pt // prev_rows

    def cur(which):
        return pl.BlockSpec((pt, LANES), lambda b, n, h: (b * tiles + n, which * heads + g * hpg + h))

    def prev(which):
        return pl.BlockSpec((prev_rows, LANES),
                            lambda b, n, h: (jnp.maximum((b * tiles + n) * ppt - 1, 0), which * heads + g * hpg + h))

    out = pl.BlockSpec((pt, LANES), lambda b, n, h: (b * tiles + n, h))
    scratch = [pltpu.VMEM((hpg, 2, blk, 2 * blk), F32),
               pltpu.VMEM((pt // r, LANES), BF16),
               pltpu.VMEM((pt // r + blk, LANES), BF16),
               pltpu.VMEM((pt // r + blk, LANES), BF16)]
    if others:
        assert pt % MERGE_ROWS == 0
        out_specs, out_shape = out, jax.ShapeDtypeStruct((t, hpg * LANES), BF16)
        scratch += [pltpu.VMEM((pt, LANES), F32)] * 2
    else:
        out_specs, out_shape = [out, out], [jax.ShapeDtypeStruct((t, hpg * LANES), F32)] * 2
    return pl.pallas_call(
        functools.partial(_dattn_kernel, r=r, pt=pt, span=span, scale=HEAD_DIM ** -0.5, n_other=len(others) // 2),
        grid=(batch, tiles, hpg),
        in_specs=[pl.BlockSpec((blk, 2 * blk), lambda b, n, h: (0, 0)),
                  pl.BlockSpec(memory_space=pltpu.SMEM),
                  cur(0), cur(1), cur(2), prev(1), prev(2)] + [out] * len(others),
        out_specs=out_specs,
        out_shape=out_shape,
        scratch_shapes=scratch,
        compiler_params=_params("arbitrary", "arbitrary", "arbitrary"),
        name=name,
    )(bucket, tbl, qkv, qkv, qkv, qkv, qkv, *others)


def _merge_kernel(b_ref, c_ref, h_ref, cp_ref, hp_ref, cw_ref, oatt_ref, sgc_ref, sga_ref, wc_ref, wa_ref, out_ref,
                  u_s, aconv_s, *, conv_k, tm, halo, seq_tiles):
    i = pl.program_id(0)

    @pl.when(pl.program_id(1) == 0)
    def _():
        seq_start = (i % seq_tiles) == 0
        up = cp_ref[...].astype(F32) * hp_ref[...].astype(F32)
        u_s[0:halo] = jnp.where(seq_start, 0.0, up)
        rc = CONV_ROWS
        for r0 in range(0, tm, rc):
            u_s[halo + r0:halo + r0 + rc] = c_ref[r0:r0 + rc].astype(F32) * h_ref[r0:r0 + rc].astype(F32)
        for r0 in range(0, tm, rc):
            lo = halo - (conv_k - 1) + r0
            conv = cw_ref[0:1, :] * u_s[lo:lo + rc]
            for j in range(1, conv_k):
                conv = conv + cw_ref[j:j + 1, :] * u_s[lo + j:lo + j + rc]
            aconv_s[r0:r0 + rc] = (b_ref[r0:r0 + rc].astype(F32) * conv).astype(BF16)

    yc = jnp.dot(aconv_s[...], wc_ref[...], preferred_element_type=F32)
    ya = jnp.dot(oatt_ref[...], wa_ref[...], preferred_element_type=F32)
    out_ref[...] = (sgc_ref[...].astype(F32) * yc + sga_ref[...].astype(F32) * ya).astype(out_ref.dtype)


def _merge(cin, gates, oatt, conv_w, wc, wa, layer, seq, d_model, *, tm, tn, name):
    t = cin.shape[0]
    conv_k, cc = conv_w.shape[-2:]
    aw = oatt.shape[-1]
    tm, tn = _pick(seq, tm), _pick(d_model, tn)
    halo = 16
    assert cc % LANES == 0 and tm % halo == 0 and conv_k - 1 <= halo and tm % CONV_ROWS == 0
    nj = d_model // tn

    def colblk(c):
        return pl.BlockSpec((tm, cc), lambda i, j: (i, c))

    def halo_blk(c):
        return pl.BlockSpec((halo, cc), lambda i, j: (jnp.maximum(i * (tm // halo) - 1, 0), c))

    in_specs = [colblk(0), colblk(1), colblk(2), halo_blk(1), halo_blk(2),
                pl.BlockSpec((None, conv_k, cc), lambda i, j: (layer, 0, 0)),
                pl.BlockSpec((tm, aw), lambda i, j: (i, 0)),
                pl.BlockSpec((tm, tn), lambda i, j: (i, j)),
                pl.BlockSpec((tm, tn), lambda i, j: (i, nj + j)),
                pl.BlockSpec((None, cc, tn), lambda i, j: (layer, 0, j)),
                pl.BlockSpec((None, aw, tn), lambda i, j: (layer, 0, j))]
    return pl.pallas_call(
        functools.partial(_merge_kernel, conv_k=conv_k, tm=tm, halo=halo, seq_tiles=seq // tm),
        grid=(t // tm, nj),
        in_specs=in_specs,
        out_specs=pl.BlockSpec((tm, tn), lambda i, j: (i, j)),
        out_shape=jax.ShapeDtypeStruct((t, d_model), BF16),
        scratch_shapes=[pltpu.VMEM((tm + halo, cc), F32), pltpu.VMEM((tm, cc), BF16)],
        compiler_params=_params("parallel", "arbitrary"),
        name=name,
    )(cin, cin, cin, cin, cin, conv_w, oatt, gates, gates, wc, wa)


def _route(p):
    epg = len(p) // N_EXPERT_GROUPS
    assert epg == 4
    scores = []
    for g in range(N_EXPERT_GROUPS):
        a, b, c, d = p[g * epg:(g + 1) * epg]
        hi1, lo1, hi2, lo2 = jnp.maximum(a, b), jnp.minimum(a, b), jnp.maximum(c, d), jnp.minimum(c, d)
        scores.append(jnp.maximum(hi1, hi2) + jnp.maximum(jnp.minimum(hi1, hi2), jnp.maximum(lo1, lo2)))
    best, gidx = scores[0], jnp.zeros(scores[0].shape, I32)
    for g in range(1, N_EXPERT_GROUPS):
        take = scores[g] > best
        best = jnp.where(take, scores[g], best)
        gidx = jnp.where(take, g, gidx)
    sel = []
    for k in range(epg):
        s = p[k]
        for g in range(1, N_EXPERT_GROUPS):
            s = jnp.where(gidx == g, p[g * epg + k], s)
        sel.append(s)
    v1, i1 = sel[0], jnp.zeros(sel[0].shape, I32)
    for k in range(1, epg):
        take = sel[k] > v1
        v1 = jnp.where(take, sel[k], v1)
        i1 = jnp.where(take, k, i1)
    v2, i2 = jnp.full(v1.shape, -1.0, F32), jnp.zeros(v1.shape, I32)
    for k in range(epg):
        take = (i1 != k) & (sel[k] > v2)
        v2 = jnp.where(take, sel[k], v2)
        i2 = jnp.where(take, k, i2)
    tot = v1 + v2
    return gidx * epg + i1, gidx * epg + i2, v1 / tot, v2 / tot


def _xattn_kernel(s1_ref, g1_ref, b1_ref, wq_ref, kv_ref, wo_ref, g2_ref, b2_ref, wrt_ref, br_ref,
                  x_ref, xp_ref, rw_ref, ridx_ref, cnt_ref, tril_s, carry_s, *, heads, alpha, scale, tm):
    i = pl.program_id(0)
    x1 = _layer_norm(s1_ref[...], g1_ref[...], b1_ref[...])
    q = jnp.dot(x1.astype(BF16), wq_ref[...], preferred_element_type=F32).astype(BF16)
    mw = heads * MEM_HEAD_DIM
    outs = []
    for h in range(heads):
        cs = slice(h * MEM_HEAD_DIM, (h + 1) * MEM_HEAD_DIM)
        k = kv_ref[:, cs]
        v = kv_ref[:, mw + h * MEM_HEAD_DIM:mw + (h + 1) * MEM_HEAD_DIM]
        lg = lax.dot_general(q[:, cs], k, (((1,), (1,)), ((), ())), preferred_element_type=F32) * scale
        m = jnp.max(lg, axis=-1, keepdims=True)
        p = jnp.exp(lg - m)
        l = jnp.sum(p, axis=-1, keepdims=True)
        outs.append((jnp.dot(p.astype(BF16), v, preferred_element_type=F32) / l).astype(BF16))
    o = jnp.concatenate(outs, axis=-1)
    xa = jnp.dot(o, wo_ref[...], preferred_element_type=F32)
    x2 = _layer_norm(alpha * x1 + xa, g2_ref[...], b2_ref[...])
    x_ref[...] = x2
    xp_ref[...] = _pack_halves(x2)

    nt = (((1,), (1,)), ((), ()))
    lg = lax.dot_general(x2.astype(BF16), wrt_ref[...], nt, preferred_element_type=F32) + br_ref[...]
    m = jnp.max(lg, axis=-1, keepdims=True)
    ex = jnp.exp(lg - m)
    probs = ex / jnp.sum(ex, axis=-1, keepdims=True)
    n_e = probs.shape[-1]
    e1, e2, w1, w2 = _route([probs[:, e:e + 1] for e in range(n_e)])
    lane = lax.broadcasted_iota(I32, (tm, LANES), 1)
    rw_ref[...] = jnp.where(lane == 0, w1, jnp.where(lane == 1, w2, 0.0))

    @pl.when(i == 0)
    def _():
        rr = lax.broadcasted_iota(I32, (tm, tm), 0)
        cc = lax.broadcasted_iota(I32, (tm, tm), 1)
        tril_s[...] = jnp.where(cc <= rr, 1.0, 0.0).astype(BF16)
        carry_s[...] = jnp.zeros(carry_s.shape, F32)

    lane_e = lax.broadcasted_iota(I32, (tm, n_e), 1)
    onehot = jnp.where((lane_e == e1) | (lane_e == e2), 1.0, 0.0)
    incl = jnp.dot(tril_s[...], onehot.astype(BF16), preferred_element_type=F32)
    rank = incl - onehot + carry_s[...]
    r1 = jnp.sum(jnp.where(lane_e == e1, rank, 0.0), axis=-1, keepdims=True).astype(I32)
    r2 = jnp.sum(jnp.where(lane_e == e2, rank, 0.0), axis=-1, keepdims=True).astype(I32)
    carry_s[...] = carry_s[...] + incl[tm - 1:tm, :]
    cnt_ref[...] = carry_s[...]

    lane8 = lax.broadcasted_iota(I32, (tm, ROUTE_ROWS), 1)
    cols = [e1, e2, r1 >> 8, r1 & 255, r2 >> 8, r2 & 255]
    tok = jnp.zeros((tm, ROUTE_ROWS), F32)
    for c, val in enumerate(cols):
        tok = jnp.where(lane8 == c, val.astype(F32), tok)
    ro = lax.broadcasted_iota(I32, (ROUTE_ROWS, ROUTE_ROWS), 0)
    co = lax.broadcasted_iota(I32, (ROUTE_ROWS, ROUTE_ROWS), 1)
    mix = jnp.where(((ro == 0) & (co == 0)) | ((ro == 1) & (co == 1)) | ((ro == 2) & (co == 3)) | ((ro == 3) & (co == 5)),
                    1.0, jnp.where(((ro == 2) & (co == 2)) | ((ro == 3) & (co == 4)), 256.0, 0.0))
    ridx_ref[...] = lax.dot_general(mix.astype(BF16), tok.astype(BF16), nt, preferred_element_type=F32).astype(I32)


def _xattn(s1, kv, g1, b1, wq, wo, g2, b2, wrt, br, layer, seq, alpha, *, tm, name):
    t, d = s1.shape
    tm = _pick(seq, tm)
    n_mem, kvw = kv.shape[1:]
    mw = wq.shape[-1]
    n_e = wrt.shape[0]
    heads = mw // MEM_HEAD_DIM
    assert 2 * t + n_e * EXPERT_TILE < 65536
    assert tm % LN_ROWS == 0
    vec = pl.BlockSpec((None, 1, d), lambda i: (layer, 0, 0))
    once = pl.Buffered(1)
    row = lambda w: pl.BlockSpec((tm, w), lambda i: (i, 0))
    return pl.pallas_call(
        functools.partial(_xattn_kernel, heads=heads, alpha=alpha, scale=MEM_HEAD_DIM ** -0.5, tm=tm),
        grid=(t // tm,),
        in_specs=[row(d), vec, vec,
                  pl.BlockSpec((None, d, mw), lambda i: (layer, 0, 0), pipeline_mode=once),
                  pl.BlockSpec((None, n_mem, kvw), lambda i: (i // (seq // tm), 0, 0)),
                  pl.BlockSpec((None, mw, d), lambda i: (layer, 0, 0), pipeline_mode=once), vec, vec,
                  pl.BlockSpec((n_e, d), lambda i: (0, 0)),
                  pl.BlockSpec((1, n_e), lambda i: (0, 0))],
        out_specs=[row(d), row(d // 2), row(LANES),
                   pl.BlockSpec((ROUTE_ROWS, tm), lambda i: (0, i)),
                   pl.BlockSpec((1, n_e), lambda i: (0, 0))],
        out_shape=[jax.ShapeDtypeStruct((t, d), F32), jax.ShapeDtypeStruct((t, d // 2), U32),
                   jax.ShapeDtypeStruct((t, LANES), F32), jax.ShapeDtypeStruct((ROUTE_ROWS, t), I32),
                   jax.ShapeDtypeStruct((1, n_e), F32)],
        scratch_shapes=[pltpu.VMEM((tm, tm), BF16), pltpu.VMEM((1, n_e), F32)],
        compiler_params=_params("arbitrary"),
        name=name,
    )(s1, g1, b1, wq, kv, wo, g2, b2, wrt, br)


def _row_copy(src, s, dst, p, sem):
    return pltpu.make_async_copy(src.at[pl.ds(s, 1), :], dst.at[pl.ds(p, 1), :], sem)


def _dispatch_kernel(off_ref, cnt_ref, pad_ref, na_ref, x_ref, idx_ref, xs_hbm, sem, *, tm, n_e, n_tiles):
    def body(s, c):
        _row_copy(x_ref, s, xs_hbm, idx_ref[0, s], sem).start()
        _row_copy(x_ref, s, xs_hbm, idx_ref[1, s], sem).start()
        return c

    lax.fori_loop(0, tm, body, 0, unroll=DMA_UNROLL)

    @pl.when(pl.program_id(0) == 0)
    def _():
        for e in range(n_e):
            base = off_ref[e] + cnt_ref[e]
            n_pad = pad_ref[e] - cnt_ref[e]

            def fill(s, c):
                _row_copy(x_ref, 0, xs_hbm, base + s, sem).start()
                return c

            def drain(s, c):
                _row_copy(x_ref, 0, xs_hbm, base, sem).wait()
                return c

            lax.fori_loop(0, n_pad, fill, 0)
            lax.fori_loop(0, n_pad, drain, 0)

        def tail_copy(i):
            return pltpu.make_async_copy(x_ref.at[pl.ds(0, EXPERT_TILE), :],
                                         xs_hbm.at[pl.ds(i * EXPERT_TILE, EXPERT_TILE), :], sem)

        def tail_fill(i, c):
            tail_copy(i).start()
            return c

        def tail_drain(i, c):
            tail_copy(i).wait()
            return c

        lax.fori_loop(na_ref[0], n_tiles, tail_fill, 0)
        lax.fori_loop(na_ref[0], n_tiles, tail_drain, 0)

    pltpu.make_async_copy(xs_hbm.at[pl.ds(0, 2 * tm), :], xs_hbm.at[pl.ds(0, 2 * tm), :], sem).wait()


def _dispatch(xp, ridx, off, cnt, padded, na, n_slots, *, tm, name):
    t, w = xp.shape
    tm = _pick(t, tm)
    n_e = off.shape[0]
    assert tm >= EXPERT_TILE
    return pl.pallas_call(
        functools.partial(_dispatch_kernel, tm=tm, n_e=n_e, n_tiles=n_slots // EXPERT_TILE),
        grid_spec=pltpu.PrefetchScalarGridSpec(
            num_scalar_prefetch=4, grid=(t // tm,),
            in_specs=[pl.BlockSpec((tm, w), lambda i, *_: (i, 0)),
                      pl.BlockSpec((ROUTE_ROWS, tm), lambda i, *_: (0, i), memory_space=pltpu.SMEM)],
            out_specs=pl.BlockSpec(memory_space=pl.ANY),
            scratch_shapes=[pltpu.SemaphoreType.DMA(())]),
        out_shape=jax.ShapeDtypeStruct((n_slots, w), U32),
        compiler_params=_params("arbitrary"),
        name=name,
    )(off, cnt, padded, na, xp, ridx)


def _experts_kernel(te_ref, na_ref, xs_ref, wg_ref, wu_ref, wd_ref, ys_ref):
    @pl.when(pl.program_id(0) < na_ref[0])
    def _():
        hi, lo = _unpack_halves(xs_ref[...])
        hi, lo = hi.astype(BF16), lo.astype(BF16)
        half = hi.shape[1]
        hg = (jnp.dot(hi, wg_ref[:half], preferred_element_type=F32)
              + jnp.dot(lo, wg_ref[half:], preferred_element_type=F32))
        hu = (jnp.dot(hi, wu_ref[:half], preferred_element_type=F32)
              + jnp.dot(lo, wu_ref[half:], preferred_element_type=F32))
        hid = (hg * _sigmoid(hg) * hu).astype(BF16)
        ys_ref[...] = _pack_halves(jnp.dot(hid, wd_ref[...], preferred_element_type=F32))

    @pl.when(pl.program_id(0) >= na_ref[0])
    def _():
        ys_ref[...] = jnp.zeros(ys_ref.shape, ys_ref.dtype)


def _experts(xs, te, na, wg, wu, wd, layer, *, name):
    n_slots, w = xs.shape
    d, f = wg.shape[2:]
    tm = EXPERT_TILE
    nt = n_slots // tm

    def tile(i, te_ref, na_ref):
        return (jnp.minimum(i, na_ref[0] - 1), 0)

    def wmap(i, te_ref, na_ref):
        return (layer, te_ref[jnp.minimum(i, na_ref[0] - 1)], 0, 0)

    return pl.pallas_call(
        _experts_kernel,
        grid_spec=pltpu.PrefetchScalarGridSpec(
            num_scalar_prefetch=2, grid=(nt,),
            in_specs=[pl.BlockSpec((tm, w), tile),
                      pl.BlockSpec((None, None, d, f), wmap),
                      pl.BlockSpec((None, None, d, f), wmap),
                      pl.BlockSpec((None, None, f, d), wmap)],
            out_specs=pl.BlockSpec((tm, w), lambda i, te_ref, na_ref: (i, 0))),
        out_shape=jax.ShapeDtypeStruct((n_slots, w), U32),
        compiler_params=_params("arbitrary"),
        name=name,
    )(te, na, xs, wg, wu, wd)


def _combine_kernel(idx_ref, nidx_ref, x_ref, rw_ref, g_ref, b_ref, ys_hbm, o_ref, ob_ref, buf, sem, *, tm, alpha):
    i = pl.program_id(0)
    n = pl.num_programs(0)

    def issue(ref, slot):
        def body(s, c):
            _row_copy(ys_hbm, ref[0, s], buf.at[slot], s, sem.at[slot]).start()
            _row_copy(ys_hbm, ref[1, s], buf.at[slot], tm + s, sem.at[slot]).start()
            return c

        lax.fori_loop(0, tm, body, 0, unroll=DMA_UNROLL)

    @pl.when(i == 0)
    def _():
        issue(idx_ref, 0)

    @pl.when(i + 1 < n)
    def _():
        issue(nidx_ref, (i + 1) % 2)

    slot = i % 2
    pltpu.make_async_copy(ys_hbm.at[pl.ds(0, 2 * tm), :], buf.at[slot], sem.at[slot]).wait()
    rc = LN_ROWS
    half = buf.shape[-1]
    d = 2 * half

    for r0 in range(0, tm, rc):
        sl = pl.ds(r0, rc)
        hi1, lo1 = _unpack_halves(buf[slot, sl, :])
        hi2, lo2 = _unpack_halves(buf[slot, pl.ds(tm + r0, rc), :])
        w1 = rw_ref[sl, 0:1]
        w2 = rw_ref[sl, 1:2]
        s_hi = alpha * x_ref[sl, :half] + (w1 * hi1 + w2 * hi2)
        s_lo = alpha * x_ref[sl, half:] + (w1 * lo1 + w2 * lo2)
        mu = (jnp.sum(s_hi, axis=-1, keepdims=True) + jnp.sum(s_lo, axis=-1, keepdims=True)) / d
        c_hi, c_lo = s_hi - mu, s_lo - mu
        var = (jnp.sum(c_hi * c_hi, axis=-1, keepdims=True) + jnp.sum(c_lo * c_lo, axis=-1, keepdims=True)) / d
        inv = lax.rsqrt(var + LN_EPS)
        y_hi = c_hi * inv * g_ref[:, :half] + b_ref[:, :half]
        y_lo = c_lo * inv * g_ref[:, half:] + b_ref[:, half:]
        o_ref[sl, :half] = y_hi
        o_ref[sl, half:] = y_lo
        ob_ref[sl, :half] = y_hi.astype(BF16)
        ob_ref[sl, half:] = y_lo.astype(BF16)


def _combine(x2, rw, pos, ys, g, b, layer, alpha, *, tm, name):
    t, d = x2.shape
    tm = _pick(t, tm)
    nsteps = t // tm
    w = ys.shape[1]
    assert tm % LN_ROWS == 0
    row = lambda width: pl.BlockSpec((tm, width), lambda i: (i, 0))
    vec = pl.BlockSpec((None, 1, d), lambda i: (layer, 0, 0))
    return pl.pallas_call(
        functools.partial(_combine_kernel, tm=tm, alpha=alpha),
        grid=(nsteps,),
        in_specs=[pl.BlockSpec((ROUTE_ROWS, tm), lambda i: (0, i), memory_space=pltpu.SMEM),
                  pl.BlockSpec((ROUTE_ROWS, tm), lambda i: (0, jnp.minimum(i + 1, nsteps - 1)),
                               memory_space=pltpu.SMEM),
                  row(d), row(LANES), vec, vec,
                  pl.BlockSpec(memory_space=pl.ANY)],
        out_specs=[row(d), row(d)],
        out_shape=[jax.ShapeDtypeStruct((t, d), F32), jax.ShapeDtypeStruct((t, d), BF16)],
        scratch_shapes=[pltpu.VMEM((2, 2 * tm, w), U32), pltpu.SemaphoreType.DMA((2,))],
        compiler_params=_params("arbitrary"),
        name=name,
    )(pos, pos, x2, rw, g, b, ys)


def _expert_layout(counts, ridx, n_tiles):
    cnt = counts.reshape(-1).astype(I32)
    padded = (cnt + EXPERT_TILE - 1) // EXPERT_TILE * EXPERT_TILE
    ends = jnp.cumsum(padded)
    off = ends - padded
    tile_ends = ends // EXPERT_TILE
    na = tile_ends[-1:]
    tiles = jnp.minimum(jnp.arange(n_tiles, dtype=I32), na[0] - 1)
    te = jnp.sum((tiles[:, None] >= tile_ends[None, :]).astype(I32), axis=1)
    e, rank = ridx[0:2], ridx[2:4]
    base = jnp.sum(jnp.where(e[:, None, :] == jnp.arange(cnt.shape[0], dtype=I32)[None, :, None], off[None, :, None], 0),
                   axis=1)
    pos = jnp.concatenate([base + rank, jnp.zeros((ROUTE_ROWS - 2, ridx.shape[1]), I32)], axis=0)
    return off, cnt, padded, te, na, pos


def kernel(x, mem, w_in, conv_w, w_conv_proj, w_attn_proj, w_out, rel_bias, ln1_g, ln1_b, w_mem_q, w_mem_kv,
           w_mem_o, ln2_g, ln2_b, w_router, b_router, w_gate, w_up, w_down, ln3_g, ln3_b):
    batch, seq, d = x.shape
    depth = w_in.shape[0]
    t = batch * seq
    alpha = (2 * depth) ** 0.25
    cc = conv_w.shape[-1]
    n_groups = len(ATTN_GROUPS)
    aw = w_attn_proj.shape[1]
    hpg = aw // HEAD_DIM
    n_e = w_router.shape[1]
    assert rel_bias.shape[1] == n_groups * hpg
    qkv_w = 3 * n_groups * aw
    n_tiles = -(-(2 * t) // EXPERT_TILE) + n_e
    n_slots = n_tiles * EXPERT_TILE

    bf = lambda a: a.astype(BF16)
    wc_b, wa_b = bf(w_conv_proj), bf(w_attn_proj)
    wq_b, wmo_b = bf(w_mem_q), bf(w_mem_o)
    wrt_b, wg_b, wu_b, wd_b = bf(w_router.T), bf(w_gate), bf(w_up), bf(w_down)
    br = b_router.reshape(1, -1).astype(F32)
    vec3 = lambda a: a.reshape(depth, 1, d)
    ln1_g, ln1_b, ln2_g, ln2_b, ln3_g, ln3_b = map(vec3, (ln1_g, ln1_b, ln2_g, ln2_b, ln3_g, ln3_b))
    tbl = rel_bias.T.astype(F32)
    mem_b = bf(mem).reshape(batch * mem.shape[1], d)

    xf = x.reshape(t, d)
    xb = bf(xf)
    for l in range(depth):
        cin = _matmul(xb, w_in, l, BF16, tm=1024, tn=512, col0=0, ncols=3 * cc, name=f"inproj_conv{l}")
        qkv = _matmul(xb, w_in, l, F32, tm=1024, tn=512, col0=3 * cc, ncols=qkv_w, name=f"inproj_qkv{l}")
        gates = _matmul(xb, w_in, l, BF16, tm=1024, tn=512, col0=3 * cc + qkv_w, ncols=2 * d, sigmoid=True,
                        name=f"inproj_gate{l}")
        others = []
        for g, (window, r) in enumerate(ATTN_GROUPS):
            last = g == n_groups - 1
            res = _dilated_attention(qkv, tbl[g * hpg:(g + 1) * hpg], g, n_groups, hpg, batch, seq, window, r,
                                     others if last else [], name=f"dattn{l}_{g}")
            if not last:
                others += list(res)
        merged = _merge(cin, gates, res, conv_w, wc_b, wa_b, l, seq, d, tm=1024, tn=512, name=f"merge{l}")
        s1 = _matmul(merged, w_out, l, F32, tm=1024, tn=512, resid=xf, alpha=alpha, name=f"outproj{l}")
        kv = _matmul(mem_b, w_mem_kv, l, BF16, tm=512, tn=512, name=f"memkv{l}")
        kv = kv.reshape(batch, mem.shape[1], kv.shape[-1])
        x2, x2p, rw, ridx, counts = _xattn(s1, kv, ln1_g, ln1_b, wq_b, wmo_b, ln2_g, ln2_b, wrt_b, br, l, seq, alpha,
                                           tm=256, name=f"xattn{l}")
        off, cnt, padded, te, na, pos = _expert_layout(counts, ridx, n_tiles)
        xs = _dispatch(x2p, pos, off, cnt, padded, na, n_slots, tm=512, name=f"dispatch{l}")
        ys = _experts(xs, te, na, wg_b, wu_b, wd_b, l, name=f"experts{l}")
        xf, xb = _combine(x2, rw, pos, ys, ln3_g, ln3_b, l, alpha, tm=256, name=f"combine{l}")
    return xf.reshape(batch, seq, d)
```

```python
import functools
import math

import jax
import jax.numpy as jnp
from jax import lax
from jax.experimental import pallas as pl
from jax.experimental.pallas import tpu as pltpu

HEAD_DIM = 128
ATTN_GROUPS = ((128, 1), (512, 4), (2048, 16))
ATTN_BLOCK = 128
N_BUCKETS = 32
MAX_DISTANCE = 2048
MEM_HEAD_DIM = 128
N_EXPERT_GROUPS = 4
LN_EPS = 1e-5
NEG_INF = -1e30

VMEM_LIMIT_BYTES = 56 * 1024 * 1024
LANES = 128
ROUTE_ROWS = 8
EXPERT_TILE = 256
LN_ROWS = 64
MERGE_ROWS = 32
CONV_ROWS = 64
DMA_UNROLL = 8

F32, BF16, U32, I32 = jnp.float32, jnp.bfloat16, jnp.uint32, jnp.int32


def _params(*sem):
    return pltpu.CompilerParams(dimension_semantics=sem, vmem_limit_bytes=VMEM_LIMIT_BYTES)


def _pick(n, pref):
    if n <= pref:
        return n
    t = pref
    while n % t:
        t //= 2
    return t


def _sigmoid(x):
    return 1.0 / (1.0 + jnp.exp(-x))


def _layer_norm(x, g, b):
    mu = jnp.mean(x, axis=-1, keepdims=True)
    xc = x - mu
    var = jnp.mean(xc * xc, axis=-1, keepdims=True)
    return xc * lax.rsqrt(var + LN_EPS) * g + b


def _pack_halves(x):
    half = x.shape[1] // 2
    hi = lax.bitcast_convert_type(x[:, :half].astype(BF16).astype(F32), U32)
    lo = lax.bitcast_convert_type(x[:, half:].astype(BF16).astype(F32), U32)
    return hi | (lo >> 16)


def _unpack_halves(p):
    hi = lax.bitcast_convert_type(p & jnp.uint32(0xFFFF0000), F32)
    lo = lax.bitcast_convert_type(p << 16, F32)
    return hi, lo


def _mm_kernel(*refs, sigmoid, alpha, layer, col0, tn):
    if alpha is None:
        a_ref, w_hbm, o_ref, wst_s, wb_s, sem = refs
    else:
        a_ref, w_hbm, r_ref, o_ref, wst_s, wb_s, sem = refs
    j = pl.program_id(0)

    def w_copy(jj):
        cols = pl.ds(pl.multiple_of(col0 + jj * tn, LANES), tn)
        return pltpu.make_async_copy(w_hbm.at[layer, :, cols], wst_s, sem)

    @pl.when(pl.program_id(1) == 0)
    def _():
        @pl.when(j == 0)
        def _():
            w_copy(0).start()

        w_copy(j).wait()
        wb_s[...] = wst_s[...].astype(BF16)

        @pl.when(j + 1 < pl.num_programs(0))
        def _():
            w_copy(j + 1).start()

    acc = jnp.dot(a_ref[...], wb_s[...], preferred_element_type=F32)
    if alpha is not None:
        acc = acc + alpha * r_ref[...]
    if sigmoid:
        acc = _sigmoid(acc)
    o_ref[...] = acc.astype(o_ref.dtype)


def _matmul(a, w, layer, out_dtype, *, tm, tn, col0=0, ncols=None, sigmoid=False, resid=None, alpha=None, name):
    m, k = a.shape
    n = w.shape[-1] - col0 if ncols is None else ncols
    tm, tn = _pick(m, tm), _pick(n, tn)
    assert col0 % LANES == 0 and tn % LANES == 0
    in_specs = [pl.BlockSpec((tm, k), lambda j, i: (i, 0)),
                pl.BlockSpec(memory_space=pl.ANY)]
    args = [a, w]
    if resid is not None:
        in_specs.append(pl.BlockSpec((tm, tn), lambda j, i: (i, j)))
        args.append(resid)
    return pl.pallas_call(
        functools.partial(_mm_kernel, sigmoid=sigmoid, alpha=alpha, layer=layer, col0=col0, tn=tn),
        grid=(n // tn, m // tm),
        in_specs=in_specs,
        out_specs=pl.BlockSpec((tm, tn), lambda j, i: (i, j)),
        out_shape=jax.ShapeDtypeStruct((m, n), out_dtype),
        scratch_shapes=[pltpu.VMEM((k, tn), F32), pltpu.VMEM((k, tn), BF16), pltpu.SemaphoreType.DMA(())],
        compiler_params=_params("arbitrary", "arbitrary"),
        name=name,
    )(*args)


def _dattn_kernel(bucket_ref, tbl_ref, q_ref, k_ref, v_ref, kp_ref, vp_ref, *rest, r, pt, span, scale, n_other):
    others = rest[:2 * n_other]
    if n_other:
        oatt_ref, bias_s, qs, kcat, vcat, o_ref, lse_ref = rest[2 * n_other:]
    else:
        o_ref, lse_ref, bias_s, qs, kcat, vcat = rest
    blk = ATTN_BLOCK
    n = pl.program_id(1)
    h = pl.program_id(2)
    ch = pt // r
    nb = ch // blk

    @pl.when((pl.program_id(0) == 0) & (n == 0))
    def _():
        bk = bucket_ref[...]
        row = lax.broadcasted_iota(I32, (blk, 2 * blk), 0)
        col = lax.broadcasted_iota(I32, (blk, 2 * blk), 1)
        rel = row + blk - col
        band = (rel >= 0) & (rel <= span)
        bias = jnp.zeros((blk, 2 * blk), F32)
        for b in range(N_BUCKETS):
            bias = jnp.where(bk == b, tbl_ref[h, b], bias)
        bias_s[h, 0] = jnp.where(band, bias, NEG_INF)
        bias_s[h, 1] = jnp.where(band & (col >= blk), bias, NEG_INF)

    def rows(ref, start, size):
        if r == 1:
            return ref[pl.ds(start, size), :]
        return ref[pl.ds(start, size, stride=r), :]

    for rho in range(r):
        qs[...] = rows(q_ref, rho, ch).astype(BF16)
        kcat[0:blk] = rows(kp_ref, rho, blk).astype(BF16)
        kcat[blk:] = rows(k_ref, rho, ch).astype(BF16)
        vcat[0:blk] = rows(vp_ref, rho, blk).astype(BF16)
        vcat[blk:] = rows(v_ref, rho, ch).astype(BF16)
        for b in range(nb):
            first = jnp.where(n == 0, 1, 0) if b == 0 else 0
            q = qs[b * blk:(b + 1) * blk]
            k = kcat[b * blk:(b + 2) * blk]
            v = vcat[b * blk:(b + 2) * blk]
            lg = lax.dot_general(q, k, (((1,), (1,)), ((), ())), preferred_element_type=F32) * scale
            lg = lg + bias_s[h, first]
            m = jnp.max(lg, axis=-1, keepdims=True)
            p = jnp.exp(lg - m)
            l = jnp.sum(p, axis=-1, keepdims=True)
            o = jnp.dot(p.astype(BF16), v, preferred_element_type=F32) / l
            lse = jnp.broadcast_to(m + jnp.log(l), (blk, LANES))
            start = rho + b * blk * r
            if r == 1:
                o_ref[pl.ds(start, blk), :] = o
                lse_ref[pl.ds(start, blk), :] = lse
            else:
                o_ref[pl.ds(start, blk, stride=r), :] = o
                lse_ref[pl.ds(start, blk, stride=r), :] = lse

    if n_other:
        rc = MERGE_ROWS

        def merge_rows(c, carry):
            sl = pl.ds(pl.multiple_of(c * rc, rc), rc)
            ls = [others[2 * g + 1][sl, :] for g in range(n_other)] + [lse_ref[sl, :]]
            os_ = [others[2 * g][sl, :] for g in range(n_other)] + [o_ref[sl, :]]
            mx = functools.reduce(jnp.maximum, ls)
            es = [jnp.exp(x - mx) for x in ls]
            den = functools.reduce(lambda a, b: a + b, es)
            acc = es[0] * os_[0]
            for e, o in zip(es[1:], os_[1:]):
                acc = acc + e * o
            oatt_ref[sl, :] = (acc / den).astype(oatt_ref.dtype)
            return carry

        lax.fori_loop(0, pt // rc, merge_rows, 0, unroll=4)


def _t5_bucket(dist):
    max_exact = N_BUCKETS // 2
    d = jnp.maximum(dist, 0)
    log_part = jnp.log(jnp.maximum(d, 1).astype(F32) / max_exact) / math.log(MAX_DISTANCE / max_exact)
    large = jnp.minimum(max_exact + (log_part * (N_BUCKETS - max_exact)).astype(I32), N_BUCKETS - 1)
    return jnp.where(d < max_exact, d, large)


def _dilated_attention(qkv, tbl, g, n_groups, hpg, batch, seq, window, r, others, name):
    blk = ATTN_BLOCK
    t = qkv.shape[0]
    pt = blk * ATTN_GROUPS[-1][1]
    assert seq % pt == 0 and pt % (blk * r) == 0 and HEAD_DIM == LANES
    tiles = seq // pt
    span = window // r
    qi = jnp.arange(blk)[:, None] + blk
    ki = jnp.arange(2 * blk)[None, :]
    bucket = _t5_bucket((qi - ki) * r).astype(I32)
    heads = n_groups * hpg
    prev_rows = blk * r
    ppt = pt // prev_rows

    def cur(which):
        return pl.BlockSpec((pt, LANES), lambda b, n, h: (b * tiles + n, which * heads + g * hpg + h))

    def prev(which):
        return pl.BlockSpec((prev_rows, LANES),
                            lambda b, n, h: (jnp.maximum((b * tiles + n) * ppt - 1, 0), which * heads + g * hpg + h))

    out = pl.BlockSpec((pt, LANES), lambda b, n, h: (b * tiles + n, h))
    scratch = [pltpu.VMEM((hpg, 2, blk, 2 * blk), F32),
               pltpu.VMEM((pt // r, LANES), BF16),
               pltpu.VMEM((pt // r + blk, LANES), BF16),
               pltpu.VMEM((pt // r + blk, LANES), BF16)]
    if others:
        assert pt % MERGE_ROWS == 0
        out_specs, out_shape = out, jax.ShapeDtypeStruct((t, hpg * LANES), BF16)
        scratch += [pltpu.VMEM((pt, LANES), F32)] * 2
    else:
        out_specs, out_shape = [out, out], [jax.ShapeDtypeStruct((t, hpg * LANES), F32)] * 2
    return pl.pallas_call(
        functools.partial(_dattn_kernel, r=r, pt=pt, span=span, scale=HEAD_DIM ** -0.5, n_other=len(others) // 2),
        grid=(batch, tiles, hpg),
        in_specs=[pl.BlockSpec((blk, 2 * blk), lambda b, n, h: (0, 0)),
                  pl.BlockSpec(memory_space=pltpu.SMEM),
                  cur(0), cur(1), cur(2), prev(1), prev(2)] + [out] * len(others),
        out_specs=out_specs,
        out_shape=out_shape,
        scratch_shapes=scratch,
        compiler_params=_params("arbitrary", "arbitrary", "arbitrary"),
        name=name,
    )(bucket, tbl, qkv, qkv, qkv, qkv, qkv, *others)


def _merge_kernel(b_ref, c_ref, h_ref, cp_ref, hp_ref, cw_ref, oatt_ref, sgc_ref, sga_ref, wc_ref, wa_ref, out_ref,
                  u_s, aconv_s, *, conv_k, tm, halo, seq_tiles):
    i = pl.program_id(0)

    @pl.when(pl.program_id(1) == 0)
    def _():
        seq_start = (i % seq_tiles) == 0
        up = cp_ref[...].astype(F32) * hp_ref[...].astype(F32)
        u_s[0:halo] = jnp.where(seq_start, 0.0, up)
        rc = CONV_ROWS
        for r0 in range(0, tm, rc):
            u_s[halo + r0:halo + r0 + rc] = c_ref[r0:r0 + rc].astype(F32) * h_ref[r0:r0 + rc].astype(F32)
        for r0 in range(0, tm, rc):
            lo = halo - (conv_k - 1) + r0
            conv = cw_ref[0:1, :] * u_s[lo:lo + rc]
            for j in range(1, conv_k):
                conv = conv + cw_ref[j:j + 1, :] * u_s[lo + j:lo + j + rc]
            aconv_s[r0:r0 + rc] = (b_ref[r0:r0 + rc].astype(F32) * conv).astype(BF16)

    yc = jnp.dot(aconv_s[...], wc_ref[...], preferred_element_type=F32)
    ya = jnp.dot(oatt_ref[...], wa_ref[...], preferred_element_type=F32)
    out_ref[...] = (sgc_ref[...].astype(F32) * yc + sga_ref[...].astype(F32) * ya).astype(out_ref.dtype)


def _merge(cin, gates, oatt, conv_w, wc, wa, layer, seq, d_model, *, tm, tn, name):
    t = cin.shape[0]
    conv_k, cc = conv_w.shape[-2:]
    aw = oatt.shape[-1]
    tm, tn = _pick(seq, tm), _pick(d_model, tn)
    halo = 16
    assert cc % LANES == 0 and tm % halo == 0 and conv_k - 1 <= halo and tm % CONV_ROWS == 0
    nj = d_model // tn

    def colblk(c):
        return pl.BlockSpec((tm, cc), lambda i, j: (i, c))

    def halo_blk(c):
        return pl.BlockSpec((halo, cc), lambda i, j: (jnp.maximum(i * (tm // halo) - 1, 0), c))

    in_specs = [colblk(0), colblk(1), colblk(2), halo_blk(1), halo_blk(2),
                pl.BlockSpec((None, conv_k, cc), lambda i, j: (layer, 0, 0)),
                pl.BlockSpec((tm, aw), lambda i, j: (i, 0)),
                pl.BlockSpec((tm, tn), lambda i, j: (i, j)),
                pl.BlockSpec((tm, tn), lambda i, j: (i, nj + j)),
                pl.BlockSpec((None, cc, tn), lambda i, j: (layer, 0, j)),
                pl.BlockSpec((None, aw, tn), lambda i, j: (layer, 0, j))]
    return pl.pallas_call(
        functools.partial(_merge_kernel, conv_k=conv_k, tm=tm, halo=halo, seq_tiles=seq // tm),
        grid=(t // tm, nj),
        in_specs=in_specs,
        out_specs=pl.BlockSpec((tm, tn), lambda i, j: (i, j)),
        out_shape=jax.ShapeDtypeStruct((t, d_model), BF16),
        scratch_shapes=[pltpu.VMEM((tm + halo, cc), F32), pltpu.VMEM((tm, cc), BF16)],
        compiler_params=_params("parallel", "arbitrary"),
        name=name,
    )(cin, cin, cin, cin, cin, conv_w, oatt, gates, gates, wc, wa)


def _route(p):
    epg = len(p) // N_EXPERT_GROUPS
    assert epg == 4
    scores = []
    for g in range(N_EXPERT_GROUPS):
        a, b, c, d = p[g * epg:(g + 1) * epg]
        hi1, lo1, hi2, lo2 = jnp.maximum(a, b), jnp.minimum(a, b), jnp.maximum(c, d), jnp.minimum(c, d)
        scores.append(jnp.maximum(hi1, hi2) + jnp.maximum(jnp.minimum(hi1, hi2), jnp.maximum(lo1, lo2)))
    best, gidx = scores[0], jnp.zeros(scores[0].shape, I32)
    for g in range(1, N_EXPERT_GROUPS):
        take = scores[g] > best
        best = jnp.where(take, scores[g], best)
        gidx = jnp.where(take, g, gidx)
    sel = []
    for k in range(epg):
        s = p[k]
        for g in range(1, N_EXPERT_GROUPS):
            s = jnp.where(gidx == g, p[g * epg + k], s)
        sel.append(s)
    v1, i1 = sel[0], jnp.zeros(sel[0].shape, I32)
    for k in range(1, epg):
        take = sel[k] > v1
        v1 = jnp.where(take, sel[k], v1)
        i1 = jnp.where(take, k, i1)
    v2, i2 = jnp.full(v1.shape, -1.0, F32), jnp.zeros(v1.shape, I32)
    for k in range(epg):
        take = (i1 != k) & (sel[k] > v2)
        v2 = jnp.where(take, sel[k], v2)
        i2 = jnp.where(take, k, i2)
    tot = v1 + v2
    return gidx * epg + i1, gidx * epg + i2, v1 / tot, v2 / tot


def _xattn_kernel(s1_ref, g1_ref, b1_ref, wq_ref, kv_ref, wo_ref, g2_ref, b2_ref, wrt_ref, br_ref,
                  x_ref, xp_ref, rw_ref, ridx_ref, cnt_ref, tril_s, carry_s, *, heads, alpha, scale, tm):
    i = pl.program_id(0)
    x1 = _layer_norm(s1_ref[...], g1_ref[...], b1_ref[...])
    q = jnp.dot(x1.astype(BF16), wq_ref[...], preferred_element_type=F32).astype(BF16)
    mw = heads * MEM_HEAD_DIM
    outs = []
    for h in range(heads):
        cs = slice(h * MEM_HEAD_DIM, (h + 1) * MEM_HEAD_DIM)
        k = kv_ref[:, cs]
        v = kv_ref[:, mw + h * MEM_HEAD_DIM:mw + (h + 1) * MEM_HEAD_DIM]
        lg = lax.dot_general(q[:, cs], k, (((1,), (1,)), ((), ())), preferred_element_type=F32) * scale
        m = jnp.max(lg, axis=-1, keepdims=True)
        p = jnp.exp(lg - m)
        l = jnp.sum(p, axis=-1, keepdims=True)
        outs.append((jnp.dot(p.astype(BF16), v, preferred_element_type=F32) / l).astype(BF16))
    o = jnp.concatenate(outs, axis=-1)
    xa = jnp.dot(o, wo_ref[...], preferred_element_type=F32)
    x2 = _layer_norm(alpha * x1 + xa, g2_ref[...], b2_ref[...])
    x_ref[...] = x2
    xp_ref[...] = _pack_halves(x2)

    half = x2.shape[1] // 2
    nt = (((1,), (1,)), ((), ()))
    lg = (lax.dot_general(x2[:, :half].astype(BF16), wrt_ref[:, :half], nt, preferred_element_type=F32)
          + lax.dot_general(x2[:, half:].astype(BF16), wrt_ref[:, half:], nt, preferred_element_type=F32)
          + br_ref[...])
    m = jnp.max(lg, axis=-1, keepdims=True)
    ex = jnp.exp(lg - m)
    probs = ex / jnp.sum(ex, axis=-1, keepdims=True)
    n_e = probs.shape[-1]
    e1, e2, w1, w2 = _route([probs[:, e:e + 1] for e in range(n_e)])
    lane = lax.broadcasted_iota(I32, (tm, LANES), 1)
    rw_ref[...] = jnp.where(lane == 0, w1, jnp.where(lane == 1, w2, 0.0))

    @pl.when(i == 0)
    def _():
        rr = lax.broadcasted_iota(I32, (tm, tm), 0)
        cc = lax.broadcasted_iota(I32, (tm, tm), 1)
        tril_s[...] = jnp.where(cc <= rr, 1.0, 0.0).astype(BF16)
        carry_s[...] = jnp.zeros(carry_s.shape, F32)

    lane_e = lax.broadcasted_iota(I32, (tm, n_e), 1)
    onehot = jnp.where((lane_e == e1) | (lane_e == e2), 1.0, 0.0)
    incl = jnp.dot(tril_s[...], onehot.astype(BF16), preferred_element_type=F32)
    rank = incl - onehot + carry_s[...]
    r1 = jnp.sum(jnp.where(lane_e == e1, rank, 0.0), axis=-1, keepdims=True).astype(I32)
    r2 = jnp.sum(jnp.where(lane_e == e2, rank, 0.0), axis=-1, keepdims=True).astype(I32)
    carry_s[...] = carry_s[...] + incl[tm - 1:tm, :]
    cnt_ref[...] = carry_s[...]

    lane8 = lax.broadcasted_iota(I32, (tm, ROUTE_ROWS), 1)
    cols = [e1, e2, r1 >> 8, r1 & 255, r2 >> 8, r2 & 255]
    tok = jnp.zeros((tm, ROUTE_ROWS), F32)
    for c, val in enumerate(cols):
        tok = jnp.where(lane8 == c, val.astype(F32), tok)
    ro = lax.broadcasted_iota(I32, (ROUTE_ROWS, ROUTE_ROWS), 0)
    co = lax.broadcasted_iota(I32, (ROUTE_ROWS, ROUTE_ROWS), 1)
    mix = jnp.where(((ro == 0) & (co == 0)) | ((ro == 1) & (co == 1)) | ((ro == 2) & (co == 3)) | ((ro == 3) & (co == 5)),
                    1.0, jnp.where(((ro == 2) & (co == 2)) | ((ro == 3) & (co == 4)), 256.0, 0.0))
    ridx_ref[...] = lax.dot_general(mix.astype(BF16), tok.astype(BF16), nt, preferred_element_type=F32).astype(I32)


def _xattn(s1, kv, g1, b1, wq, wo, g2, b2, wrt, br, layer, seq, alpha, *, tm, name):
    t, d = s1.shape
    tm = _pick(seq, tm)
    n_mem, kvw = kv.shape[1:]
    mw = wq.shape[-1]
    n_e = wrt.shape[0]
    heads = mw // MEM_HEAD_DIM
    assert 2 * t + n_e * EXPERT_TILE < 65536
    assert tm % LN_ROWS == 0
    vec = pl.BlockSpec((None, 1, d), lambda i: (layer, 0, 0))
    once = pl.Buffered(1)
    row = lambda w: pl.BlockSpec((tm, w), lambda i: (i, 0))
    return pl.pallas_call(
        functools.partial(_xattn_kernel, heads=heads, alpha=alpha, scale=MEM_HEAD_DIM ** -0.5, tm=tm),
        grid=(t // tm,),
        in_specs=[row(d), vec, vec,
                  pl.BlockSpec((None, d, mw), lambda i: (layer, 0, 0), pipeline_mode=once),
                  pl.BlockSpec((None, n_mem, kvw), lambda i: (i // (seq // tm), 0, 0)),
                  pl.BlockSpec((None, mw, d), lambda i: (layer, 0, 0), pipeline_mode=once), vec, vec,
                  pl.BlockSpec((n_e, d), lambda i: (0, 0)),
                  pl.BlockSpec((1, n_e), lambda i: (0, 0))],
        out_specs=[row(d), row(d // 2), row(LANES),
                   pl.BlockSpec((ROUTE_ROWS, tm), lambda i: (0, i)),
                   pl.BlockSpec((1, n_e), lambda i: (0, 0))],
        out_shape=[jax.ShapeDtypeStruct((t, d), F32), jax.ShapeDtypeStruct((t, d // 2), U32),
                   jax.ShapeDtypeStruct((t, LANES), F32), jax.ShapeDtypeStruct((ROUTE_ROWS, t), I32),
                   jax.ShapeDtypeStruct((1, n_e), F32)],
        scratch_shapes=[pltpu.VMEM((tm, tm), BF16), pltpu.VMEM((1, n_e), F32)],
        compiler_params=_params("arbitrary"),
        name=name,
    )(s1, g1, b1, wq, kv, wo, g2, b2, wrt, br)


def _row_copy(src, s, dst, p, sem):
    return pltpu.make_async_copy(src.at[pl.ds(s, 1), :], dst.at[pl.ds(p, 1), :], sem)


def _dispatch_kernel(off_ref, cnt_ref, pad_ref, na_ref, x_ref, idx_ref, xs_hbm, sem, *, tm, n_e, n_tiles):
    def body(s, c):
        _row_copy(x_ref, s, xs_hbm, idx_ref[0, s], sem).start()
        _row_copy(x_ref, s, xs_hbm, idx_ref[1, s], sem).start()
        return c

    lax.fori_loop(0, tm, body, 0, unroll=DMA_UNROLL)

    @pl.when(pl.program_id(0) == 0)
    def _():
        for e in range(n_e):
            base = off_ref[e] + cnt_ref[e]
            n_pad = pad_ref[e] - cnt_ref[e]

            def fill(s, c):
                _row_copy(x_ref, 0, xs_hbm, base + s, sem).start()
                return c

            def drain(s, c):
                _row_copy(x_ref, 0, xs_hbm, base, sem).wait()
                return c

            lax.fori_loop(0, n_pad, fill, 0)
            lax.fori_loop(0, n_pad, drain, 0)

        def tail_copy(i):
            return pltpu.make_async_copy(x_ref.at[pl.ds(0, EXPERT_TILE), :],
                                         xs_hbm.at[pl.ds(i * EXPERT_TILE, EXPERT_TILE), :], sem)

        def tail_fill(i, c):
            tail_copy(i).start()
            return c

        def tail_drain(i, c):
            tail_copy(i).wait()
            return c

        lax.fori_loop(na_ref[0], n_tiles, tail_fill, 0)
        lax.fori_loop(na_ref[0], n_tiles, tail_drain, 0)

    pltpu.make_async_copy(xs_hbm.at[pl.ds(0, 2 * tm), :], xs_hbm.at[pl.ds(0, 2 * tm), :], sem).wait()


def _dispatch(xp, ridx, off, cnt, padded, na, n_slots, *, tm, name):
    t, w = xp.shape
    tm = _pick(t, tm)
    n_e = off.shape[0]
    assert tm >= EXPERT_TILE
    return pl.pallas_call(
        functools.partial(_dispatch_kernel, tm=tm, n_e=n_e, n_tiles=n_slots // EXPERT_TILE),
        grid_spec=pltpu.PrefetchScalarGridSpec(
            num_scalar_prefetch=4, grid=(t // tm,),
            in_specs=[pl.BlockSpec((tm, w), lambda i, *_: (i, 0)),
                      pl.BlockSpec((ROUTE_ROWS, tm), lambda i, *_: (0, i), memory_space=pltpu.SMEM)],
            out_specs=pl.BlockSpec(memory_space=pl.ANY),
            scratch_shapes=[pltpu.SemaphoreType.DMA(())]),
        out_shape=jax.ShapeDtypeStruct((n_slots, w), U32),
        compiler_params=_params("arbitrary"),
        name=name,
    )(off, cnt, padded, na, xp, ridx)


def _experts_kernel(te_ref, na_ref, xs_ref, wg_ref, wu_ref, wd_ref, ys_ref):
    @pl.when(pl.program_id(0) < na_ref[0])
    def _():
        hi, lo = _unpack_halves(xs_ref[...])
        hi, lo = hi.astype(BF16), lo.astype(BF16)
        half = hi.shape[1]
        hg = (jnp.dot(hi, wg_ref[:half], preferred_element_type=F32)
              + jnp.dot(lo, wg_ref[half:], preferred_element_type=F32))
        hu = (jnp.dot(hi, wu_ref[:half], preferred_element_type=F32)
              + jnp.dot(lo, wu_ref[half:], preferred_element_type=F32))
        hid = (hg * _sigmoid(hg) * hu).astype(BF16)
        ys_ref[...] = _pack_halves(jnp.dot(hid, wd_ref[...], preferred_element_type=F32))

    @pl.when(pl.program_id(0) >= na_ref[0])
    def _():
        ys_ref[...] = jnp.zeros(ys_ref.shape, ys_ref.dtype)


def _experts(xs, te, na, wg, wu, wd, layer, *, name):
    n_slots, w = xs.shape
    d, f = wg.shape[2:]
    tm = EXPERT_TILE
    nt = n_slots // tm

    def tile(i, te_ref, na_ref):
        return (jnp.minimum(i, na_ref[0] - 1), 0)

    def wmap(i, te_ref, na_ref):
        return (layer, te_ref[jnp.minimum(i, na_ref[0] - 1)], 0, 0)

    return pl.pallas_call(
        _experts_kernel,
        grid_spec=pltpu.PrefetchScalarGridSpec(
            num_scalar_prefetch=2, grid=(nt,),
            in_specs=[pl.BlockSpec((tm, w), tile),
                      pl.BlockSpec((None, None, d, f), wmap),
                      pl.BlockSpec((None, None, d, f), wmap),
                      pl.BlockSpec((None, None, f, d), wmap)],
            out_specs=pl.BlockSpec((tm, w), lambda i, te_ref, na_ref: (i, 0))),
        out_shape=jax.ShapeDtypeStruct((n_slots, w), U32),
        compiler_params=_params("arbitrary"),
        name=name,
    )(te, na, xs, wg, wu, wd)


def _combine_kernel(idx_ref, nidx_ref, x_ref, rw_ref, g_ref, b_ref, ys_hbm, o_ref, ob_ref, buf, sem, *, tm, alpha):
    i = pl.program_id(0)
    n = pl.num_programs(0)

    def issue(ref, slot):
        def body(s, c):
            _row_copy(ys_hbm, ref[0, s], buf.at[slot], s, sem.at[slot]).start()
            _row_copy(ys_hbm, ref[1, s], buf.at[slot], tm + s, sem.at[slot]).start()
            return c

        lax.fori_loop(0, tm, body, 0, unroll=DMA_UNROLL)

    @pl.when(i == 0)
    def _():
        issue(idx_ref, 0)

    @pl.when(i + 1 < n)
    def _():
        issue(nidx_ref, (i + 1) % 2)

    slot = i % 2
    pltpu.make_async_copy(ys_hbm.at[pl.ds(0, 2 * tm), :], buf.at[slot], sem.at[slot]).wait()
    rc = LN_ROWS
    half = buf.shape[-1]
    d = 2 * half

    for r0 in range(0, tm, rc):
        sl = pl.ds(r0, rc)
        hi1, lo1 = _unpack_halves(buf[slot, sl, :])
        hi2, lo2 = _unpack_halves(buf[slot, pl.ds(tm + r0, rc), :])
        w1 = rw_ref[sl, 0:1]
        w2 = rw_ref[sl, 1:2]
        s_hi = alpha * x_ref[sl, :half] + (w1 * hi1 + w2 * hi2)
        s_lo = alpha * x_ref[sl, half:] + (w1 * lo1 + w2 * lo2)
        mu = (jnp.sum(s_hi, axis=-1, keepdims=True) + jnp.sum(s_lo, axis=-1, keepdims=True)) / d
        c_hi, c_lo = s_hi - mu, s_lo - mu
        var = (jnp.sum(c_hi * c_hi, axis=-1, keepdims=True) + jnp.sum(c_lo * c_lo, axis=-1, keepdims=True)) / d
        inv = lax.rsqrt(var + LN_EPS)
        y_hi = c_hi * inv * g_ref[:, :half] + b_ref[:, :half]
        y_lo = c_lo * inv * g_ref[:, half:] + b_ref[:, half:]
        o_ref[sl, :half] = y_hi
        o_ref[sl, half:] = y_lo
        ob_ref[sl, :half] = y_hi.astype(BF16)
        ob_ref[sl, half:] = y_lo.astype(BF16)


def _combine(x2, rw, pos, ys, g, b, layer, alpha, *, tm, name):
    t, d = x2.shape
    tm = _pick(t, tm)
    nsteps = t // tm
    w = ys.shape[1]
    assert tm % LN_ROWS == 0
    row = lambda width: pl.BlockSpec((tm, width), lambda i: (i, 0))
    vec = pl.BlockSpec((None, 1, d), lambda i: (layer, 0, 0))
    return pl.pallas_call(
        functools.partial(_combine_kernel, tm=tm, alpha=alpha),
        grid=(nsteps,),
        in_specs=[pl.BlockSpec((ROUTE_ROWS, tm), lambda i: (0, i), memory_space=pltpu.SMEM),
                  pl.BlockSpec((ROUTE_ROWS, tm), lambda i: (0, jnp.minimum(i + 1, nsteps - 1)),
                               memory_space=pltpu.SMEM),
                  row(d), row(LANES), vec, vec,
                  pl.BlockSpec(memory_space=pl.ANY)],
        out_specs=[row(d), row(d)],
        out_shape=[jax.ShapeDtypeStruct((t, d), F32), jax.ShapeDtypeStruct((t, d), BF16)],
        scratch_shapes=[pltpu.VMEM((2, 2 * tm, w), U32), pltpu.SemaphoreType.DMA((2,))],
        compiler_params=_params("arbitrary"),
        name=name,
    )(pos, pos, x2, rw, g, b, ys)


def _expert_layout(counts, ridx, n_tiles):
    cnt = counts.reshape(-1).astype(I32)
    padded = (cnt + EXPERT_TILE - 1) // EXPERT_TILE * EXPERT_TILE
    ends = jnp.cumsum(padded)
    off = ends - padded
    tile_ends = ends // EXPERT_TILE
    na = tile_ends[-1:]
    tiles = jnp.minimum(jnp.arange(n_tiles, dtype=I32), na[0] - 1)
    te = jnp.sum((tiles[:, None] >= tile_ends[None, :]).astype(I32), axis=1)
    e, rank = ridx[0:2], ridx[2:4]
    base = jnp.sum(jnp.where(e[:, None, :] == jnp.arange(cnt.shape[0], dtype=I32)[None, :, None], off[None, :, None], 0),
                   axis=1)
    pos = jnp.concatenate([base + rank, jnp.zeros((ROUTE_ROWS - 2, ridx.shape[1]), I32)], axis=0)
    return off, cnt, padded, te, na, pos


def kernel(x, mem, w_in, conv_w, w_conv_proj, w_attn_proj, w_out, rel_bias, ln1_g, ln1_b, w_mem_q, w_mem_kv,
           w_mem_o, ln2_g, ln2_b, w_router, b_router, w_gate, w_up, w_down, ln3_g, ln3_b):
    batch, seq, d = x.shape
    depth = w_in.shape[0]
    t = batch * seq
    alpha = (2 * depth) ** 0.25
    cc = conv_w.shape[-1]
    n_groups = len(ATTN_GROUPS)
    aw = w_attn_proj.shape[1]
    hpg = aw // HEAD_DIM
    n_e = w_router.shape[1]
    assert rel_bias.shape[1] == n_groups * hpg
    qkv_w = 3 * n_groups * aw
    n_tiles = -(-(2 * t) // EXPERT_TILE) + n_e
    n_slots = n_tiles * EXPERT_TILE

    bf = lambda a: a.astype(BF16)
    wc_b, wa_b = bf(w_conv_proj), bf(w_attn_proj)
    wq_b, wmo_b = bf(w_mem_q), bf(w_mem_o)
    wrt_b, wg_b, wu_b, wd_b = bf(w_router.T), bf(w_gate), bf(w_up), bf(w_down)
    br = b_router.reshape(1, -1).astype(F32)
    vec3 = lambda a: a.reshape(depth, 1, d)
    ln1_g, ln1_b, ln2_g, ln2_b, ln3_g, ln3_b = map(vec3, (ln1_g, ln1_b, ln2_g, ln2_b, ln3_g, ln3_b))
    tbl = rel_bias.T.astype(F32)
    mem_b = bf(mem).reshape(batch * mem.shape[1], d)

    xf = x.reshape(t, d)
    xb = bf(xf)
    for l in range(depth):
        cin = _matmul(xb, w_in, l, BF16, tm=1024, tn=768, col0=0, ncols=3 * cc, name=f"inproj_conv{l}")
        qkv = _matmul(xb, w_in, l, F32, tm=1024, tn=768, col0=3 * cc, ncols=qkv_w, name=f"inproj_qkv{l}")
        gates = _matmul(xb, w_in, l, BF16, tm=1024, tn=1024, col0=3 * cc + qkv_w, ncols=2 * d, sigmoid=True,
                        name=f"inproj_gate{l}")
        others = []
        for g, (window, r) in enumerate(ATTN_GROUPS):
            last = g == n_groups - 1
            res = _dilated_attention(qkv, tbl[g * hpg:(g + 1) * hpg], g, n_groups, hpg, batch, seq, window, r,
                                     others if last else [], name=f"dattn{l}_{g}")
            if not last:
                others += list(res)
        merged = _merge(cin, gates, res, conv_w, wc_b, wa_b, l, seq, d, tm=1024, tn=512, name=f"merge{l}")
        s1 = _matmul(merged, w_out, l, F32, tm=512, tn=1024, resid=xf, alpha=alpha, name=f"outproj{l}")
        kv = _matmul(mem_b, w_mem_kv, l, BF16, tm=512, tn=1024, name=f"memkv{l}")
        kv = kv.reshape(batch, mem.shape[1], kv.shape[-1])
        x2, x2p, rw, ridx, counts = _xattn(s1, kv, ln1_g, ln1_b, wq_b, wmo_b, ln2_g, ln2_b, wrt_b, br, l, seq, alpha,
                                           tm=256, name=f"xattn{l}")
        off, cnt, padded, te, na, pos = _expert_layout(counts, ridx, n_tiles)
        xs = _dispatch(x2p, pos, off, cnt, padded, na, n_slots, tm=512, name=f"dispatch{l}")
        ys = _experts(xs, te, na, wg_b, wu_b, wd_b, l, name=f"experts{l}")
        xf, xb = _combine(x2, rw, pos, ys, ln3_g, ln3_b, l, alpha, tm=256, name=f"combine{l}")
    return xf.reshape(batch, seq, d)
```

```python
import functools
import math

import jax
import jax.numpy as jnp
from jax import lax
from jax.experimental import pallas as pl
from jax.experimental.pallas import tpu as pltpu

HEAD_DIM = 128
ATTN_GROUPS = ((128, 1), (512, 4), (2048, 16))
ATTN_BLOCK = 128
N_BUCKETS = 32
MAX_DISTANCE = 2048
MEM_HEAD_DIM = 128
N_EXPERT_GROUPS = 4
LN_EPS = 1e-5
NEG_INF = -1e30

VMEM_LIMIT_BYTES = 56 * 1024 * 1024
LANES = 128
ROUTE_ROWS = 8
EXPERT_TILE = 256
LN_ROWS = 64
MERGE_ROWS = 32
CONV_ROWS = 64
DMA_UNROLL = 8

F32, BF16, U32, I32 = jnp.float32, jnp.bfloat16, jnp.uint32, jnp.int32


def _params(*sem):
    return pltpu.CompilerParams(dimension_semantics=sem, vmem_limit_bytes=VMEM_LIMIT_BYTES)


def _pick(n, pref):
    if n <= pref:
        return n
    t = pref
    while n % t:
        t //= 2
    return t


def _sigmoid(x):
    return 1.0 / (1.0 + jnp.exp(-x))


def _layer_norm(x, g, b):
    mu = jnp.mean(x, axis=-1, keepdims=True)
    xc = x - mu
    var = jnp.mean(xc * xc, axis=-1, keepdims=True)
    return xc * lax.rsqrt(var + LN_EPS) * g + b


def _pack_halves(x):
    half = x.shape[1] // 2
    hi = lax.bitcast_convert_type(x[:, :half].astype(BF16).astype(F32), U32)
    lo = lax.bitcast_convert_type(x[:, half:].astype(BF16).astype(F32), U32)
    return hi | (lo >> 16)


def _unpack_halves(p):
    hi = lax.bitcast_convert_type(p & jnp.uint32(0xFFFF0000), F32)
    lo = lax.bitcast_convert_type(p << 16, F32)
    return hi, lo


def _mm_kernel(*refs, sigmoid, alpha, layer, col0, tn):
    if alpha is None:
        a_ref, w_hbm, o_ref, wst_s, wb_s, sem = refs
    else:
        a_ref, w_hbm, r_ref, o_ref, wst_s, wb_s, sem = refs
    j = pl.program_id(0)

    def w_copy(jj):
        cols = pl.ds(pl.multiple_of(col0 + jj * tn, LANES), tn)
        return pltpu.make_async_copy(w_hbm.at[layer, :, cols], wst_s, sem)

    @pl.when(pl.program_id(1) == 0)
    def _():
        @pl.when(j == 0)
        def _():
            w_copy(0).start()

        w_copy(j).wait()
        wb_s[...] = wst_s[...].astype(BF16)

        @pl.when(j + 1 < pl.num_programs(0))
        def _():
            w_copy(j + 1).start()

    acc = jnp.dot(a_ref[...], wb_s[...], preferred_element_type=F32)
    if alpha is not None:
        acc = acc + alpha * r_ref[...]
    if sigmoid:
        acc = _sigmoid(acc)
    o_ref[...] = acc.astype(o_ref.dtype)


def _matmul(a, w, layer, out_dtype, *, tm, tn, col0=0, ncols=None, sigmoid=False, resid=None, alpha=None, name):
    m, k = a.shape
    n = w.shape[-1] - col0 if ncols is None else ncols
    tm, tn = _pick(m, tm), _pick(n, tn)
    assert col0 % LANES == 0 and tn % LANES == 0
    in_specs = [pl.BlockSpec((tm, k), lambda j, i: (i, 0)),
                pl.BlockSpec(memory_space=pl.ANY)]
    args = [a, w]
    if resid is not None:
        in_specs.append(pl.BlockSpec((tm, tn), lambda j, i: (i, j)))
        args.append(resid)
    return pl.pallas_call(
        functools.partial(_mm_kernel, sigmoid=sigmoid, alpha=alpha, layer=layer, col0=col0, tn=tn),
        grid=(n // tn, m // tm),
        in_specs=in_specs,
        out_specs=pl.BlockSpec((tm, tn), lambda j, i: (i, j)),
        out_shape=jax.ShapeDtypeStruct((m, n), out_dtype),
        scratch_shapes=[pltpu.VMEM((k, tn), F32), pltpu.VMEM((k, tn), BF16), pltpu.SemaphoreType.DMA(())],
        compiler_params=_params("arbitrary", "arbitrary"),
        name=name,
    )(*args)


def _dattn_kernel(bucket_ref, tbl_ref, q_ref, k_ref, v_ref, kp_ref, vp_ref, *rest, r, pt, span, scale, n_other):
    others = rest[:2 * n_other]
    if n_other:
        oatt_ref, bias_s, qs, kcat, vcat, o_ref, lse_ref = rest[2 * n_other:]
    else:
        o_ref, lse_ref, bias_s, qs, kcat, vcat = rest
    blk = ATTN_BLOCK
    n = pl.program_id(1)
    h = pl.program_id(2)
    ch = pt // r
    nb = ch // blk

    @pl.when((pl.program_id(0) == 0) & (n == 0))
    def _():
        bk = bucket_ref[...]
        row = lax.broadcasted_iota(I32, (blk, 2 * blk), 0)
        col = lax.broadcasted_iota(I32, (blk, 2 * blk), 1)
        rel = row + blk - col
        band = (rel >= 0) & (rel <= span)
        bias = jnp.zeros((blk, 2 * blk), F32)
        for b in range(N_BUCKETS):
            bias = jnp.where(bk == b, tbl_ref[h, b], bias)
        bias_s[h, 0] = jnp.where(band, bias, NEG_INF)
        bias_s[h, 1] = jnp.where(band & (col >= blk), bias, NEG_INF)

    def rows(ref, start, size):
        if r == 1:
            return ref[pl.ds(start, size), :]
        return ref[pl.ds(start, size, stride=r), :]

    for rho in range(r):
        qs[...] = rows(q_ref, rho, ch).astype(BF16)
        kcat[0:blk] = rows(kp_ref, rho, blk).astype(BF16)
        kcat[blk:] = rows(k_ref, rho, ch).astype(BF16)
        vcat[0:blk] = rows(vp_ref, rho, blk).astype(BF16)
        vcat[blk:] = rows(v_ref, rho, ch).astype(BF16)
        for b in range(nb):
            first = jnp.where(n == 0, 1, 0) if b == 0 else 0
            q = qs[b * blk:(b + 1) * blk]
            k = kcat[b * blk:(b + 2) * blk]
            v = vcat[b * blk:(b + 2) * blk]
            lg = lax.dot_general(q, k, (((1,), (1,)), ((), ())), preferred_element_type=F32) * scale
            lg = lg + bias_s[h, first]
            m = jnp.max(lg, axis=-1, keepdims=True)
            p = jnp.exp(lg - m)
            l = jnp.sum(p, axis=-1, keepdims=True)
            o = jnp.dot(p.astype(BF16), v, preferred_element_type=F32) / l
            lse = jnp.broadcast_to(m + jnp.log(l), (blk, LANES))
            start = rho + b * blk * r
            if r == 1:
                o_ref[pl.ds(start, blk), :] = o
                lse_ref[pl.ds(start, blk), :] = lse
            else:
                o_ref[pl.ds(start, blk, stride=r), :] = o
                lse_ref[pl.ds(start, blk, stride=r), :] = lse

    if n_other:
        rc = MERGE_ROWS

        def merge_rows(c, carry):
            sl = pl.ds(pl.multiple_of(c * rc, rc), rc)
            ls = [others[2 * g + 1][sl, :] for g in range(n_other)] + [lse_ref[sl, :]]
            os_ = [others[2 * g][sl, :] for g in range(n_other)] + [o_ref[sl, :]]
            mx = functools.reduce(jnp.maximum, ls)
            es = [jnp.exp(x - mx) for x in ls]
            den = functools.reduce(lambda a, b: a + b, es)
            acc = es[0] * os_[0]
            for e, o in zip(es[1:], os_[1:]):
                acc = acc + e * o
            oatt_ref[sl, :] = (acc / den).astype(oatt_ref.dtype)
            return carry

        lax.fori_loop(0, pt // rc, merge_rows, 0, unroll=4)


def _t5_bucket(dist):
    max_exact = N_BUCKETS // 2
    d = jnp.maximum(dist, 0)
    log_part = jnp.log(jnp.maximum(d, 1).astype(F32) / max_exact) / math.log(MAX_DISTANCE / max_exact)
    large = jnp.minimum(max_exact + (log_part * (N_BUCKETS - max_exact)).astype(I32), N_BUCKETS - 1)
    return jnp.where(d < max_exact, d, large)


def _dilated_attention(qkv, tbl, g, n_groups, hpg, batch, seq, window, r, others, name):
    blk = ATTN_BLOCK
    t = qkv.shape[0]
    pt = blk * ATTN_GROUPS[-1][1]
    assert seq % pt == 0 and pt % (blk * r) == 0 and HEAD_DIM == LANES
    tiles = seq // pt
    span = window // r
    qi = jnp.arange(blk)[:, None] + blk
    ki = jnp.arange(2 * blk)[None, :]
    bucket = _t5_bucket((qi - ki) * r).astype(I32)
    heads = n_groups * hpg
    prev_rows = blk * r
    ppt = pt // prev_rows

    def cur(which):
        return pl.BlockSpec((pt, LANES), lambda b, n, h: (b * tiles + n, which * heads + g * hpg + h))

    def prev(which):
        return pl.BlockSpec((prev_rows, LANES),
                            lambda b, n, h: (jnp.maximum((b * tiles + n) * ppt - 1, 0), which * heads + g * hpg + h))

    out = pl.BlockSpec((pt, LANES), lambda b, n, h: (b * tiles + n, h))
    scratch = [pltpu.VMEM((hpg, 2, blk, 2 * blk), F32),
               pltpu.VMEM((pt // r, LANES), BF16),
               pltpu.VMEM((pt // r + blk, LANES), BF16),
               pltpu.VMEM((pt // r + blk, LANES), BF16)]
    if others:
        assert pt % MERGE_ROWS == 0
        out_specs, out_shape = out, jax.ShapeDtypeStruct((t, hpg * LANES), BF16)
        scratch += [pltpu.VMEM((pt, LANES), F32)] * 2
    else:
        out_specs, out_shape = [out, out], [jax.ShapeDtypeStruct((t, hpg * LANES), F32)] * 2
    return pl.pallas_call(
        functools.partial(_dattn_kernel, r=r, pt=pt, span=span, scale=HEAD_DIM ** -0.5, n_other=len(others) // 2),
        grid=(batch, tiles, hpg),
        in_specs=[pl.BlockSpec((blk, 2 * blk), lambda b, n, h: (0, 0)),
                  pl.BlockSpec(memory_space=pltpu.SMEM),
                  cur(0), cur(1), cur(2), prev(1), prev(2)] + [out] * len(others),
        out_specs=out_specs,
        out_shape=out_shape,
        scratch_shapes=scratch,
        compiler_params=_params("arbitrary", "arbitrary", "arbitrary"),
        name=name,
    )(bucket, tbl, qkv, qkv, qkv, qkv, qkv, *others)


def _merge_kernel(b_ref, c_ref, h_ref, cp_ref, hp_ref, cw_ref, oatt_ref, sgc_ref, sga_ref, wc_ref, wa_ref, out_ref,
                  u_s, aconv_s, *, conv_k, tm, halo, seq_tiles):
    i = pl.program_id(0)

    @pl.when(pl.program_id(1) == 0)
    def _():
        seq_start = (i % seq_tiles) == 0
        up = cp_ref[...].astype(F32) * hp_ref[...].astype(F32)
        u_s[0:halo] = jnp.where(seq_start, 0.0, up)
        rc = CONV_ROWS
        for r0 in range(0, tm, rc):
            u_s[halo + r0:halo + r0 + rc] = c_ref[r0:r0 + rc].astype(F32) * h_ref[r0:r0 + rc].astype(F32)
        for r0 in range(0, tm, rc):
            lo = halo - (conv_k - 1) + r0
            conv = cw_ref[0:1, :] * u_s[lo:lo + rc]
            for j in range(1, conv_k):
                conv = conv + cw_ref[j:j + 1, :] * u_s[lo + j:lo + j + rc]
            aconv_s[r0:r0 + rc] = (b_ref[r0:r0 + rc].astype(F32) * conv).astype(BF16)

    yc = jnp.dot(aconv_s[...], wc_ref[...], preferred_element_type=F32)
    ya = jnp.dot(oatt_ref[...], wa_ref[...], preferred_element_type=F32)
    out_ref[...] = (sgc_ref[...].astype(F32) * yc + sga_ref[...].astype(F32) * ya).astype(out_ref.dtype)


def _merge(cin, gates, oatt, conv_w, wc, wa, layer, seq, d_model, *, tm, tn, name):
    t = cin.shape[0]
    conv_k, cc = conv_w.shape[-2:]
    aw = oatt.shape[-1]
    tm, tn = _pick(seq, tm), _pick(d_model, tn)
    halo = 16
    assert cc % LANES == 0 and tm % halo == 0 and conv_k - 1 <= halo and tm % CONV_ROWS == 0
    nj = d_model // tn

    def colblk(c):
        return pl.BlockSpec((tm, cc), lambda i, j: (i, c))

    def halo_blk(c):
        return pl.BlockSpec((halo, cc), lambda i, j: (jnp.maximum(i * (tm // halo) - 1, 0), c))

    in_specs = [colblk(0), colblk(1), colblk(2), halo_blk(1), halo_blk(2),
                pl.BlockSpec((None, conv_k, cc), lambda i, j: (layer, 0, 0)),
                pl.BlockSpec((tm, aw), lambda i, j: (i, 0)),
                pl.BlockSpec((tm, tn), lambda i, j: (i, j)),
                pl.BlockSpec((tm, tn), lambda i, j: (i, nj + j)),
                pl.BlockSpec((None, cc, tn), lambda i, j: (layer, 0, j)),
                pl.BlockSpec((None, aw, tn), lambda i, j: (layer, 0, j))]
    return pl.pallas_call(
        functools.partial(_merge_kernel, conv_k=conv_k, tm=tm, halo=halo, seq_tiles=seq // tm),
        grid=(t // tm, nj),
        in_specs=in_specs,
        out_specs=pl.BlockSpec((tm, tn), lambda i, j: (i, j)),
        out_shape=jax.ShapeDtypeStruct((t, d_model), BF16),
        scratch_shapes=[pltpu.VMEM((tm + halo, cc), F32), pltpu.VMEM((tm, cc), BF16)],
        compiler_params=_params("parallel", "arbitrary"),
        name=name,
    )(cin, cin, cin, cin, cin, conv_w, oatt, gates, gates, wc, wa)


def _route(p):
    epg = len(p) // N_EXPERT_GROUPS
    assert epg == 4
    scores = []
    for g in range(N_EXPERT_GROUPS):
        a, b, c, d = p[g * epg:(g + 1) * epg]
        hi1, lo1, hi2, lo2 = jnp.maximum(a, b), jnp.minimum(a, b), jnp.maximum(c, d), jnp.minimum(c, d)
        scores.append(jnp.maximum(hi1, hi2) + jnp.maximum(jnp.minimum(hi1, hi2), jnp.maximum(lo1, lo2)))
    best, gidx = scores[0], jnp.zeros(scores[0].shape, I32)
    for g in range(1, N_EXPERT_GROUPS):
        take = scores[g] > best
        best = jnp.where(take, scores[g], best)
        gidx = jnp.where(take, g, gidx)
    sel = []
    for k in range(epg):
        s = p[k]
        for g in range(1, N_EXPERT_GROUPS):
            s = jnp.where(gidx == g, p[g * epg + k], s)
        sel.append(s)
    v1, i1 = sel[0], jnp.zeros(sel[0].shape, I32)
    for k in range(1, epg):
        take = sel[k] > v1
        v1 = jnp.where(take, sel[k], v1)
        i1 = jnp.where(take, k, i1)
    v2, i2 = jnp.full(v1.shape, -1.0, F32), jnp.zeros(v1.shape, I32)
    for k in range(epg):
        take = (i1 != k) & (sel[k] > v2)
        v2 = jnp.where(take, sel[k], v2)
        i2 = jnp.where(take, k, i2)
    tot = v1 + v2
    return gidx * epg + i1, gidx * epg + i2, v1 / tot, v2 / tot


def _xattn_kernel(s1_ref, g1_ref, b1_ref, wq_ref, kv_ref, wo_ref, g2_ref, b2_ref, wrt_ref, br_ref,
                  x_ref, xp_ref, rw_ref, ridx_ref, cnt_ref, tril_s, carry_s, *, heads, alpha, scale, tm):
    i = pl.program_id(0)
    x1 = _layer_norm(s1_ref[...], g1_ref[...], b1_ref[...])
    q = jnp.dot(x1.astype(BF16), wq_ref[...], preferred_element_type=F32).astype(BF16)
    mw = heads * MEM_HEAD_DIM
    outs = []
    for h in range(heads):
        cs = slice(h * MEM_HEAD_DIM, (h + 1) * MEM_HEAD_DIM)
        k = kv_ref[:, cs]
        v = kv_ref[:, mw + h * MEM_HEAD_DIM:mw + (h + 1) * MEM_HEAD_DIM]
        lg = lax.dot_general(q[:, cs], k, (((1,), (1,)), ((), ())), preferred_element_type=F32) * scale
        m = jnp.max(lg, axis=-1, keepdims=True)
        p = jnp.exp(lg - m)
        l = jnp.sum(p, axis=-1, keepdims=True)
        outs.append((jnp.dot(p.astype(BF16), v, preferred_element_type=F32) / l).astype(BF16))
    o = jnp.concatenate(outs, axis=-1)
    xa = jnp.dot(o, wo_ref[...], preferred_element_type=F32)
    x2 = _layer_norm(alpha * x1 + xa, g2_ref[...], b2_ref[...])
    x_ref[...] = x2
    xp_ref[...] = _pack_halves(x2)

    half = x2.shape[1] // 2
    nt = (((1,), (1,)), ((), ()))
    lg = (lax.dot_general(x2[:, :half].astype(BF16), wrt_ref[:, :half], nt, preferred_element_type=F32)
          + lax.dot_general(x2[:, half:].astype(BF16), wrt_ref[:, half:], nt, preferred_element_type=F32)
          + br_ref[...])
    m = jnp.max(lg, axis=-1, keepdims=True)
    ex = jnp.exp(lg - m)
    probs = ex / jnp.sum(ex, axis=-1, keepdims=True)
    n_e = probs.shape[-1]
    e1, e2, w1, w2 = _route([probs[:, e:e + 1] for e in range(n_e)])
    lane = lax.broadcasted_iota(I32, (tm, LANES), 1)
    rw_ref[...] = jnp.where(lane == 0, w1, jnp.where(lane == 1, w2, 0.0))

    @pl.when(i == 0)
    def _():
        rr = lax.broadcasted_iota(I32, (tm, tm), 0)
        cc = lax.broadcasted_iota(I32, (tm, tm), 1)
        tril_s[...] = jnp.where(cc <= rr, 1.0, 0.0).astype(BF16)
        carry_s[...] = jnp.zeros(carry_s.shape, F32)

    lane_e = lax.broadcasted_iota(I32, (tm, n_e), 1)
    onehot = jnp.where((lane_e == e1) | (lane_e == e2), 1.0, 0.0)
    incl = jnp.dot(tril_s[...], onehot.astype(BF16), preferred_element_type=F32)
    rank = incl - onehot + carry_s[...]
    r1 = jnp.sum(jnp.where(lane_e == e1, rank, 0.0), axis=-1, keepdims=True).astype(I32)
    r2 = jnp.sum(jnp.where(lane_e == e2, rank, 0.0), axis=-1, keepdims=True).astype(I32)
    carry_s[...] = carry_s[...] + incl[tm - 1:tm, :]
    cnt_ref[...] = carry_s[...]

    lane8 = lax.broadcasted_iota(I32, (tm, ROUTE_ROWS), 1)
    cols = [e1, e2, r1 >> 8, r1 & 255, r2 >> 8, r2 & 255]
    tok = jnp.zeros((tm, ROUTE_ROWS), F32)
    for c, val in enumerate(cols):
        tok = jnp.where(lane8 == c, val.astype(F32), tok)
    ro = lax.broadcasted_iota(I32, (ROUTE_ROWS, ROUTE_ROWS), 0)
    co = lax.broadcasted_iota(I32, (ROUTE_ROWS, ROUTE_ROWS), 1)
    mix = jnp.where(((ro == 0) & (co == 0)) | ((ro == 1) & (co == 1)) | ((ro == 2) & (co == 3)) | ((ro == 3) & (co == 5)),
                    1.0, jnp.where(((ro == 2) & (co == 2)) | ((ro == 3) & (co == 4)), 256.0, 0.0))
    ridx_ref[...] = lax.dot_general(mix.astype(BF16), tok.astype(BF16), nt, preferred_element_type=F32).astype(I32)


def _xattn(s1, kv, g1, b1, wq, wo, g2, b2, wrt, br, layer, seq, alpha, *, tm, name):
    t, d = s1.shape
    tm = _pick(seq, tm)
    n_mem, kvw = kv.shape[1:]
    mw = wq.shape[-1]
    n_e = wrt.shape[0]
    heads = mw // MEM_HEAD_DIM
    assert 2 * t + n_e * EXPERT_TILE < 65536
    assert tm % LN_ROWS == 0
    vec = pl.BlockSpec((None, 1, d), lambda i: (layer, 0, 0))
    once = pl.Buffered(1)
    row = lambda w: pl.BlockSpec((tm, w), lambda i: (i, 0))
    return pl.pallas_call(
        functools.partial(_xattn_kernel, heads=heads, alpha=alpha, scale=MEM_HEAD_DIM ** -0.5, tm=tm),
        grid=(t // tm,),
        in_specs=[row(d), vec, vec,
                  pl.BlockSpec((None, d, mw), lambda i: (layer, 0, 0), pipeline_mode=once),
                  pl.BlockSpec((None, n_mem, kvw), lambda i: (i // (seq // tm), 0, 0)),
                  pl.BlockSpec((None, mw, d), lambda i: (layer, 0, 0), pipeline_mode=once), vec, vec,
                  pl.BlockSpec((n_e, d), lambda i: (0, 0)),
                  pl.BlockSpec((1, n_e), lambda i: (0, 0))],
        out_specs=[row(d), row(d // 2), row(LANES),
                   pl.BlockSpec((ROUTE_ROWS, tm), lambda i: (0, i)),
                   pl.BlockSpec((1, n_e), lambda i: (0, 0))],
        out_shape=[jax.ShapeDtypeStruct((t, d), F32), jax.ShapeDtypeStruct((t, d // 2), U32),
                   jax.ShapeDtypeStruct((t, LANES), F32), jax.ShapeDtypeStruct((ROUTE_ROWS, t), I32),
                   jax.ShapeDtypeStruct((1, n_e), F32)],
        scratch_shapes=[pltpu.VMEM((tm, tm), BF16), pltpu.VMEM((1, n_e), F32)],
        compiler_params=_params("arbitrary"),
        name=name,
    )(s1, g1, b1, wq, kv, wo, g2, b2, wrt, br)


def _row_copy(src, s, dst, p, sem):
    return pltpu.make_async_copy(src.at[pl.ds(s, 1), :], dst.at[pl.ds(p, 1), :], sem)


def _dispatch_kernel(off_ref, cnt_ref, pad_ref, na_ref, x_ref, idx_ref, xs_hbm, sem, *, tm, n_e, n_tiles):
    def body(s, c):
        _row_copy(x_ref, s, xs_hbm, idx_ref[0, s], sem).start()
        _row_copy(x_ref, s, xs_hbm, idx_ref[1, s], sem).start()
        return c

    lax.fori_loop(0, tm, body, 0, unroll=DMA_UNROLL)

    @pl.when(pl.program_id(0) == 0)
    def _():
        for e in range(n_e):
            base = off_ref[e] + cnt_ref[e]
            n_pad = pad_ref[e] - cnt_ref[e]

            def fill(s, c):
                _row_copy(x_ref, 0, xs_hbm, base + s, sem).start()
                return c

            def drain(s, c):
                _row_copy(x_ref, 0, xs_hbm, base, sem).wait()
                return c

            lax.fori_loop(0, n_pad, fill, 0)
            lax.fori_loop(0, n_pad, drain, 0)

        def tail_copy(i):
            return pltpu.make_async_copy(x_ref.at[pl.ds(0, EXPERT_TILE), :],
                                         xs_hbm.at[pl.ds(i * EXPERT_TILE, EXPERT_TILE), :], sem)

        def tail_fill(i, c):
            tail_copy(i).start()
            return c

        def tail_drain(i, c):
            tail_copy(i).wait()
            return c

        lax.fori_loop(na_ref[0], n_tiles, tail_fill, 0)
        lax.fori_loop(na_ref[0], n_tiles, tail_drain, 0)

    pltpu.make_async_copy(xs_hbm.at[pl.ds(0, 2 * tm), :], xs_hbm.at[pl.ds(0, 2 * tm), :], sem).wait()


def _dispatch(xp, ridx, off, cnt, padded, na, n_slots, *, tm, name):
    t, w = xp.shape
    tm = _pick(t, tm)
    n_e = off.shape[0]
    assert tm >= EXPERT_TILE
    return pl.pallas_call(
        functools.partial(_dispatch_kernel, tm=tm, n_e=n_e, n_tiles=n_slots // EXPERT_TILE),
        grid_spec=pltpu.PrefetchScalarGridSpec(
            num_scalar_prefetch=4, grid=(t // tm,),
            in_specs=[pl.BlockSpec((tm, w), lambda i, *_: (i, 0)),
                      pl.BlockSpec((ROUTE_ROWS, tm), lambda i, *_: (0, i), memory_space=pltpu.SMEM)],
            out_specs=pl.BlockSpec(memory_space=pl.ANY),
            scratch_shapes=[pltpu.SemaphoreType.DMA(())]),
        out_shape=jax.ShapeDtypeStruct((n_slots, w), U32),
        compiler_params=_params("arbitrary"),
        name=name,
    )(off, cnt, padded, na, xp, ridx)


N_WEIGHT_PIECES = 6


def _experts_kernel(te_ref, nxt_ref, na_ref, xs_ref, wg_hbm, wu_hbm, wd_hbm, ys_ref,
                    wgb, wub, wdb, st_in, st_out, state, sem, *, layer):
    i = pl.program_id(0)
    hd, hf = st_in.shape[0], st_out.shape[0]

    def piece_copy(e, k):
        if k < 4:
            src = (wg_hbm if k < 2 else wu_hbm).at[layer, e, pl.ds((k % 2) * hd, hd), :]
            return pltpu.make_async_copy(src, st_in, sem)
        return pltpu.make_async_copy(wd_hbm.at[layer, e, pl.ds((k - 4) * hf, hf), :], st_out, sem)

    def for_piece(k, fn):
        for kk in range(N_WEIGHT_PIECES):
            pl.when(k == kk)(functools.partial(fn, kk))

    def start_piece(e, k):
        for_piece(k, lambda kk: piece_copy(e, kk).start())

    def finish_piece(e, k, slot):
        def fn(kk):
            piece_copy(e, kk).wait()
            if kk < 4:
                dst = wgb if kk < 2 else wub
                dst[slot, pl.ds((kk % 2) * hd, hd), :] = st_in[...].astype(BF16)
            else:
                wdb[slot, pl.ds((kk - 4) * hf, hf), :] = st_out[...].astype(BF16)

        for_piece(k, fn)

    def advance(e, slot):
        k = state[1]
        finish_piece(e, k, slot)
        state[1] = k + 1

        @pl.when(k + 1 < N_WEIGHT_PIECES)
        def _():
            start_piece(e, k + 1)

    def finish_all(e, slot):
        lax.fori_loop(state[1], N_WEIGHT_PIECES, lambda _, c: (advance(e, slot), c)[1], 0)

    def begin(e):
        state[1] = 0
        state[2] = e

        @pl.when(e >= 0)
        def _():
            start_piece(e, 0)

    active = i < na_ref[0]

    @pl.when(i == 0)
    def _():
        state[0] = 1
        begin(te_ref[0])

    switch = active & ((i == 0) | (te_ref[i] != te_ref[jnp.maximum(i - 1, 0)]))

    @pl.when(switch)
    def _():
        slot = 1 - state[0]
        finish_all(te_ref[i], slot)
        state[0] = slot
        begin(nxt_ref[i])

    @pl.when(active & jnp.logical_not(switch) & (state[2] >= 0) & (state[1] < N_WEIGHT_PIECES))
    def _():
        advance(state[2], 1 - state[0])

    @pl.when(active)
    def _():
        slot = state[0]
        hi, lo = _unpack_halves(xs_ref[...])
        hi, lo = hi.astype(BF16), lo.astype(BF16)
        half = hi.shape[1]
        hg = (jnp.dot(hi, wgb[slot, :half], preferred_element_type=F32)
              + jnp.dot(lo, wgb[slot, half:], preferred_element_type=F32))
        hu = (jnp.dot(hi, wub[slot, :half], preferred_element_type=F32)
              + jnp.dot(lo, wub[slot, half:], preferred_element_type=F32))
        hid = (hg * _sigmoid(hg) * hu).astype(BF16)
        ys_ref[...] = _pack_halves(jnp.dot(hid, wdb[slot], preferred_element_type=F32))

    @pl.when(jnp.logical_not(active))
    def _():
        ys_ref[...] = jnp.zeros(ys_ref.shape, ys_ref.dtype)


def _experts(xs, te, nxt, na, wg, wu, wd, layer, *, name):
    n_slots, w = xs.shape
    d, f = wg.shape[2:]
    tm = EXPERT_TILE
    nt = n_slots // tm
    assert d % 2 == 0 and f % 2 == 0

    def tile(i, te_ref, nxt_ref, na_ref):
        return (jnp.minimum(i, na_ref[0] - 1), 0)

    hbm = pl.BlockSpec(memory_space=pl.ANY)
    return pl.pallas_call(
        functools.partial(_experts_kernel, layer=layer),
        grid_spec=pltpu.PrefetchScalarGridSpec(
            num_scalar_prefetch=3, grid=(nt,),
            in_specs=[pl.BlockSpec((tm, w), tile), hbm, hbm, hbm],
            out_specs=pl.BlockSpec((tm, w), lambda i, *_: (i, 0)),
            scratch_shapes=[pltpu.VMEM((2, d, f), BF16), pltpu.VMEM((2, d, f), BF16), pltpu.VMEM((2, f, d), BF16),
                            pltpu.VMEM((d // 2, f), F32), pltpu.VMEM((f // 2, d), F32),
                            pltpu.SMEM((3,), I32), pltpu.SemaphoreType.DMA(())]),
        out_shape=jax.ShapeDtypeStruct((n_slots, w), U32),
        compiler_params=_params("arbitrary"),
        name=name,
    )(te, nxt, na, xs, wg, wu, wd)


def _combine_kernel(idx_ref, nidx_ref, x_ref, rw_ref, g_ref, b_ref, ys_hbm, o_ref, ob_ref, buf, sem, *, tm, alpha):
    i = pl.program_id(0)
    n = pl.num_programs(0)

    def issue(ref, slot):
        def body(s, c):
            _row_copy(ys_hbm, ref[0, s], buf.at[slot], s, sem.at[slot]).start()
            _row_copy(ys_hbm, ref[1, s], buf.at[slot], tm + s, sem.at[slot]).start()
            return c

        lax.fori_loop(0, tm, body, 0, unroll=DMA_UNROLL)

    @pl.when(i == 0)
    def _():
        issue(idx_ref, 0)

    @pl.when(i + 1 < n)
    def _():
        issue(nidx_ref, (i + 1) % 2)

    slot = i % 2
    pltpu.make_async_copy(ys_hbm.at[pl.ds(0, 2 * tm), :], buf.at[slot], sem.at[slot]).wait()
    rc = LN_ROWS
    half = buf.shape[-1]
    d = 2 * half

    for r0 in range(0, tm, rc):
        sl = pl.ds(r0, rc)
        hi1, lo1 = _unpack_halves(buf[slot, sl, :])
        hi2, lo2 = _unpack_halves(buf[slot, pl.ds(tm + r0, rc), :])
        w1 = rw_ref[sl, 0:1]
        w2 = rw_ref[sl, 1:2]
        s_hi = alpha * x_ref[sl, :half] + (w1 * hi1 + w2 * hi2)
        s_lo = alpha * x_ref[sl, half:] + (w1 * lo1 + w2 * lo2)
        mu = (jnp.sum(s_hi, axis=-1, keepdims=True) + jnp.sum(s_lo, axis=-1, keepdims=True)) / d
        c_hi, c_lo = s_hi - mu, s_lo - mu
        var = (jnp.sum(c_hi * c_hi, axis=-1, keepdims=True) + jnp.sum(c_lo * c_lo, axis=-1, keepdims=True)) / d
        inv = lax.rsqrt(var + LN_EPS)
        y_hi = c_hi * inv * g_ref[:, :half] + b_ref[:, :half]
        y_lo = c_lo * inv * g_ref[:, half:] + b_ref[:, half:]
        o_ref[sl, :half] = y_hi
        o_ref[sl, half:] = y_lo
        ob_ref[sl, :half] = y_hi.astype(BF16)
        ob_ref[sl, half:] = y_lo.astype(BF16)


def _combine(x2, rw, pos, ys, g, b, layer, alpha, *, tm, name):
    t, d = x2.shape
    tm = _pick(t, tm)
    nsteps = t // tm
    w = ys.shape[1]
    assert tm % LN_ROWS == 0
    row = lambda width: pl.BlockSpec((tm, width), lambda i: (i, 0))
    vec = pl.BlockSpec((None, 1, d), lambda i: (layer, 0, 0))
    return pl.pallas_call(
        functools.partial(_combine_kernel, tm=tm, alpha=alpha),
        grid=(nsteps,),
        in_specs=[pl.BlockSpec((ROUTE_ROWS, tm), lambda i: (0, i), memory_space=pltpu.SMEM),
                  pl.BlockSpec((ROUTE_ROWS, tm), lambda i: (0, jnp.minimum(i + 1, nsteps - 1)),
                               memory_space=pltpu.SMEM),
                  row(d), row(LANES), vec, vec,
                  pl.BlockSpec(memory_space=pl.ANY)],
        out_specs=[row(d), row(d)],
        out_shape=[jax.ShapeDtypeStruct((t, d), F32), jax.ShapeDtypeStruct((t, d), BF16)],
        scratch_shapes=[pltpu.VMEM((2, 2 * tm, w), U32), pltpu.SemaphoreType.DMA((2,))],
        compiler_params=_params("arbitrary"),
        name=name,
    )(pos, pos, x2, rw, g, b, ys)


def _expert_layout(counts, ridx, n_tiles):
    cnt = counts.reshape(-1).astype(I32)
    padded = (cnt + EXPERT_TILE - 1) // EXPERT_TILE * EXPERT_TILE
    ends = jnp.cumsum(padded)
    off = ends - padded
    tile_ends = ends // EXPERT_TILE
    na = tile_ends[-1:]
    tiles = jnp.minimum(jnp.arange(n_tiles, dtype=I32), na[0] - 1)
    expert_of = lambda tile: jnp.sum((tile[:, None] >= tile_ends[None, :]).astype(I32), axis=1)
    te = expert_of(tiles)
    seg_end = jnp.sum(jnp.where(te[:, None] == jnp.arange(cnt.shape[0], dtype=I32)[None, :], tile_ends[None, :], 0), axis=1)
    nxt = jnp.where(seg_end < na[0], expert_of(seg_end), -1)
    e, rank = ridx[0:2], ridx[2:4]
    base = jnp.sum(jnp.where(e[:, None, :] == jnp.arange(cnt.shape[0], dtype=I32)[None, :, None], off[None, :, None], 0),
                   axis=1)
    pos = jnp.concatenate([base + rank, jnp.zeros((ROUTE_ROWS - 2, ridx.shape[1]), I32)], axis=0)
    return off, cnt, padded, te, nxt, na, pos


def kernel(x, mem, w_in, conv_w, w_conv_proj, w_attn_proj, w_out, rel_bias, ln1_g, ln1_b, w_mem_q, w_mem_kv,
           w_mem_o, ln2_g, ln2_b, w_router, b_router, w_gate, w_up, w_down, ln3_g, ln3_b):
    batch, seq, d = x.shape
    depth = w_in.shape[0]
    t = batch * seq
    alpha = (2 * depth) ** 0.25
    cc = conv_w.shape[-1]
    n_groups = len(ATTN_GROUPS)
    aw = w_attn_proj.shape[1]
    hpg = aw // HEAD_DIM
    n_e = w_router.shape[1]
    assert rel_bias.shape[1] == n_groups * hpg
    qkv_w = 3 * n_groups * aw
    n_tiles = -(-(2 * t) // EXPERT_TILE) + n_e
    n_slots = n_tiles * EXPERT_TILE

    bf = lambda a: a.astype(BF16)
    wc_b, wa_b = bf(w_conv_proj), bf(w_attn_proj)
    wq_b, wmo_b = bf(w_mem_q), bf(w_mem_o)
    wrt_b = bf(w_router.T)
    br = b_router.reshape(1, -1).astype(F32)
    vec3 = lambda a: a.reshape(depth, 1, d)
    ln1_g, ln1_b, ln2_g, ln2_b, ln3_g, ln3_b = map(vec3, (ln1_g, ln1_b, ln2_g, ln2_b, ln3_g, ln3_b))
    tbl = rel_bias.T.astype(F32)
    mem_b = bf(mem).reshape(batch * mem.shape[1], d)

    xf = x.reshape(t, d)
    xb = bf(xf)
    for l in range(depth):
        cin = _matmul(xb, w_in, l, BF16, tm=1024, tn=768, col0=0, ncols=3 * cc, name=f"inproj_conv{l}")
        qkv = _matmul(xb, w_in, l, F32, tm=1024, tn=768, col0=3 * cc, ncols=qkv_w, name=f"inproj_qkv{l}")
        gates = _matmul(xb, w_in, l, BF16, tm=1024, tn=1024, col0=3 * cc + qkv_w, ncols=2 * d, sigmoid=True,
                        name=f"inproj_gate{l}")
        others = []
        for g, (window, r) in enumerate(ATTN_GROUPS):
            last = g == n_groups - 1
            res = _dilated_attention(qkv, tbl[g * hpg:(g + 1) * hpg], g, n_groups, hpg, batch, seq, window, r,
                                     others if last else [], name=f"dattn{l}_{g}")
            if not last:
                others += list(res)
        merged = _merge(cin, gates, res, conv_w, wc_b, wa_b, l, seq, d, tm=1024, tn=512, name=f"merge{l}")
        s1 = _matmul(merged, w_out, l, F32, tm=512, tn=1024, resid=xf, alpha=alpha, name=f"outproj{l}")
        kv = _matmul(mem_b, w_mem_kv, l, BF16, tm=512, tn=1024, name=f"memkv{l}")
        kv = kv.reshape(batch, mem.shape[1], kv.shape[-1])
        x2, x2p, rw, ridx, counts = _xattn(s1, kv, ln1_g, ln1_b, wq_b, wmo_b, ln2_g, ln2_b, wrt_b, br, l, seq, alpha,
                                           tm=256, name=f"xattn{l}")
        off, cnt, padded, te, nxt, na, pos = _expert_layout(counts, ridx, n_tiles)
        xs = _dispatch(x2p, pos, off, cnt, padded, na, n_slots, tm=512, name=f"dispatch{l}")
        ys = _experts(xs, te, nxt, na, w_gate, w_up, w_down, l, name=f"experts{l}")
        xf, xb = _combine(x2, rw, pos, ys, ln3_g, ln3_b, l, alpha, tm=256, name=f"combine{l}")
    return xf.reshape(batch, seq, d)
```

```python
import functools
import math

import jax
import jax.numpy as jnp
from jax import lax
from jax.experimental import pallas as pl
from jax.experimental.pallas import tpu as pltpu

HEAD_DIM = 128
ATTN_GROUPS = ((128, 1), (512, 4), (2048, 16))
ATTN_BLOCK = 128
N_BUCKETS = 32
MAX_DISTANCE = 2048
MEM_HEAD_DIM = 128
N_EXPERT_GROUPS = 4
LN_EPS = 1e-5
NEG_INF = -1e30

VMEM_LIMIT_BYTES = 56 * 1024 * 1024
LANES = 128
ROUTE_ROWS = 8
EXPERT_TILE = 256
LN_ROWS = 64
MERGE_ROWS = 32
CONV_ROWS = 64
DMA_UNROLL = 8

F32, BF16, U32, I32 = jnp.float32, jnp.bfloat16, jnp.uint32, jnp.int32


def _params(*sem):
    return pltpu.CompilerParams(dimension_semantics=sem, vmem_limit_bytes=VMEM_LIMIT_BYTES)


def _pick(n, pref):
    if n <= pref:
        return n
    t = pref
    while n % t:
        t //= 2
    return t


def _sigmoid(x):
    return 1.0 / (1.0 + jnp.exp(-x))


def _layer_norm(x, g, b):
    mu = jnp.mean(x, axis=-1, keepdims=True)
    xc = x - mu
    var = jnp.mean(xc * xc, axis=-1, keepdims=True)
    return xc * lax.rsqrt(var + LN_EPS) * g + b


def _pack_halves(x):
    half = x.shape[1] // 2
    hi = lax.bitcast_convert_type(x[:, :half].astype(BF16).astype(F32), U32)
    lo = lax.bitcast_convert_type(x[:, half:].astype(BF16).astype(F32), U32)
    return hi | (lo >> 16)


def _unpack_halves(p):
    hi = lax.bitcast_convert_type(p & jnp.uint32(0xFFFF0000), F32)
    lo = lax.bitcast_convert_type(p << 16, F32)
    return hi, lo


def _mm_kernel(*refs, sigmoid, alpha, layer, col0, tn):
    if alpha is None:
        a_ref, w_hbm, o_ref, wst_s, wb_s, sem = refs
    else:
        a_ref, w_hbm, r_ref, o_ref, wst_s, wb_s, sem = refs
    j = pl.program_id(0)

    def w_copy(jj):
        cols = pl.ds(pl.multiple_of(col0 + jj * tn, LANES), tn)
        return pltpu.make_async_copy(w_hbm.at[layer, :, cols], wst_s, sem)

    @pl.when(pl.program_id(1) == 0)
    def _():
        @pl.when(j == 0)
        def _():
            w_copy(0).start()

        w_copy(j).wait()
        wb_s[...] = wst_s[...].astype(BF16)

        @pl.when(j + 1 < pl.num_programs(0))
        def _():
            w_copy(j + 1).start()

    acc = jnp.dot(a_ref[...], wb_s[...], preferred_element_type=F32)
    if alpha is not None:
        acc = acc + alpha * r_ref[...]
    if sigmoid:
        acc = _sigmoid(acc)
    o_ref[...] = acc.astype(o_ref.dtype)


def _matmul(a, w, layer, out_dtype, *, tm, tn, col0=0, ncols=None, sigmoid=False, resid=None, alpha=None, name):
    m, k = a.shape
    n = w.shape[-1] - col0 if ncols is None else ncols
    tm, tn = _pick(m, tm), _pick(n, tn)
    assert col0 % LANES == 0 and tn % LANES == 0
    in_specs = [pl.BlockSpec((tm, k), lambda j, i: (i, 0)),
                pl.BlockSpec(memory_space=pl.ANY)]
    args = [a, w]
    if resid is not None:
        in_specs.append(pl.BlockSpec((tm, tn), lambda j, i: (i, j)))
        args.append(resid)
    return pl.pallas_call(
        functools.partial(_mm_kernel, sigmoid=sigmoid, alpha=alpha, layer=layer, col0=col0, tn=tn),
        grid=(n // tn, m // tm),
        in_specs=in_specs,
        out_specs=pl.BlockSpec((tm, tn), lambda j, i: (i, j)),
        out_shape=jax.ShapeDtypeStruct((m, n), out_dtype),
        scratch_shapes=[pltpu.VMEM((k, tn), F32), pltpu.VMEM((k, tn), BF16), pltpu.SemaphoreType.DMA(())],
        compiler_params=_params("arbitrary", "arbitrary"),
        name=name,
    )(*args)


def _dattn_kernel(bucket_ref, tbl_ref, q_ref, k_ref, v_ref, kp_ref, vp_ref, *rest, r, pt, span, scale, n_other):
    others = rest[:2 * n_other]
    if n_other:
        oatt_ref, bias_s, qs, kcat, vcat, o_ref, lse_ref = rest[2 * n_other:]
    else:
        o_ref, lse_ref, bias_s, qs, kcat, vcat = rest
    blk = ATTN_BLOCK
    n = pl.program_id(1)
    h = pl.program_id(2)
    ch = pt // r
    nb = ch // blk

    @pl.when((pl.program_id(0) == 0) & (n == 0))
    def _():
        bk = bucket_ref[...]
        row = lax.broadcasted_iota(I32, (blk, 2 * blk), 0)
        col = lax.broadcasted_iota(I32, (blk, 2 * blk), 1)
        rel = row + blk - col
        band = (rel >= 0) & (rel <= span)
        bias = jnp.zeros((blk, 2 * blk), F32)
        for b in range(N_BUCKETS):
            bias = jnp.where(bk == b, tbl_ref[h, b], bias)
        bias_s[h, 0] = jnp.where(band, bias, NEG_INF)
        bias_s[h, 1] = jnp.where(band & (col >= blk), bias, NEG_INF)

    def rows(ref, start, size):
        if r == 1:
            return ref[pl.ds(start, size), :]
        return ref[pl.ds(start, size, stride=r), :]

    for rho in range(r):
        qs[...] = rows(q_ref, rho, ch).astype(BF16)
        kcat[0:blk] = rows(kp_ref, rho, blk).astype(BF16)
        kcat[blk:] = rows(k_ref, rho, ch).astype(BF16)
        vcat[0:blk] = rows(vp_ref, rho, blk).astype(BF16)
        vcat[blk:] = rows(v_ref, rho, ch).astype(BF16)
        for b in range(nb):
            first = jnp.where(n == 0, 1, 0) if b == 0 else 0
            q = qs[b * blk:(b + 1) * blk]
            k = kcat[b * blk:(b + 2) * blk]
            v = vcat[b * blk:(b + 2) * blk]
            lg = lax.dot_general(q, k, (((1,), (1,)), ((), ())), preferred_element_type=F32) * scale
            lg = lg + bias_s[h, first]
            m = jnp.max(lg, axis=-1, keepdims=True)
            p = jnp.exp(lg - m)
            l = jnp.sum(p, axis=-1, keepdims=True)
            o = jnp.dot(p.astype(BF16), v, preferred_element_type=F32) / l
            lse = jnp.broadcast_to(m + jnp.log(l), (blk, LANES))
            start = rho + b * blk * r
            if r == 1:
                o_ref[pl.ds(start, blk), :] = o
                lse_ref[pl.ds(start, blk), :] = lse
            else:
                o_ref[pl.ds(start, blk, stride=r), :] = o
                lse_ref[pl.ds(start, blk, stride=r), :] = lse

    if n_other:
        rc = MERGE_ROWS

        def merge_rows(c, carry):
            sl = pl.ds(pl.multiple_of(c * rc, rc), rc)
            ls = [others[2 * g + 1][sl, :] for g in range(n_other)] + [lse_ref[sl, :]]
            os_ = [others[2 * g][sl, :] for g in range(n_other)] + [o_ref[sl, :]]
            mx = functools.reduce(jnp.maximum, ls)
            es = [jnp.exp(x - mx) for x in ls]
            den = functools.reduce(lambda a, b: a + b, es)
            acc = es[0] * os_[0]
            for e, o in zip(es[1:], os_[1:]):
                acc = acc + e * o
            oatt_ref[sl, :] = (acc / den).astype(oatt_ref.dtype)
            return carry

        lax.fori_loop(0, pt // rc, merge_rows, 0, unroll=4)


def _t5_bucket(dist):
    max_exact = N_BUCKETS // 2
    d = jnp.maximum(dist, 0)
    log_part = jnp.log(jnp.maximum(d, 1).astype(F32) / max_exact) / math.log(MAX_DISTANCE / max_exact)
    large = jnp.minimum(max_exact + (log_part * (N_BUCKETS - max_exact)).astype(I32), N_BUCKETS - 1)
    return jnp.where(d < max_exact, d, large)


def _dilated_attention(qkv, tbl, g, n_groups, hpg, batch, seq, window, r, others, name):
    blk = ATTN_BLOCK
    t = qkv.shape[0]
    pt = blk * ATTN_GROUPS[-1][1]
    assert seq % pt == 0 and pt % (blk * r) == 0 and HEAD_DIM == LANES
    tiles = seq // pt
    span = window // r
    qi = jnp.arange(blk)[:, None] + blk
    ki = jnp.arange(2 * blk)[None, :]
    bucket = _t5_bucket((qi - ki) * r).astype(I32)
    heads = n_groups * hpg
    prev_rows = blk * r
    ppt = pt // prev_rows

    def cur(which):
        return pl.BlockSpec((pt, LANES), lambda b, n, h: (b * tiles + n, which * heads + g * hpg + h))

    def prev(which):
        return pl.BlockSpec((prev_rows, LANES),
                            lambda b, n, h: (jnp.maximum((b * tiles + n) * ppt - 1, 0), which * heads + g * hpg + h))

    out = pl.BlockSpec((pt, LANES), lambda b, n, h: (b * tiles + n, h))
    scratch = [pltpu.VMEM((hpg, 2, blk, 2 * blk), F32),
               pltpu.VMEM((pt // r, LANES), BF16),
               pltpu.VMEM((pt // r + blk, LANES), BF16),
               pltpu.VMEM((pt // r + blk, LANES), BF16)]
    if others:
        assert pt % MERGE_ROWS == 0
        out_specs, out_shape = out, jax.ShapeDtypeStruct((t, hpg * LANES), BF16)
        scratch += [pltpu.VMEM((pt, LANES), F32)] * 2
    else:
        out_specs, out_shape = [out, out], [jax.ShapeDtypeStruct((t, hpg * LANES), F32)] * 2
    return pl.pallas_call(
        functools.partial(_dattn_kernel, r=r, pt=pt, span=span, scale=HEAD_DIM ** -0.5, n_other=len(others) // 2),
        grid=(batch, tiles, hpg),
        in_specs=[pl.BlockSpec((blk, 2 * blk), lambda b, n, h: (0, 0)),
                  pl.BlockSpec(memory_space=pltpu.SMEM),
                  cur(0), cur(1), cur(2), prev(1), prev(2)] + [out] * len(others),
        out_specs=out_specs,
        out_shape=out_shape,
        scratch_shapes=scratch,
        compiler_params=_params("arbitrary", "arbitrary", "arbitrary"),
        name=name,
    )(bucket, tbl, qkv, qkv, qkv, qkv, qkv, *others)


def _merge_kernel(b_ref, c_ref, h_ref, cp_ref, hp_ref, cw_ref, oatt_ref, sgc_ref, sga_ref, wc_ref, wa_ref, out_ref,
                  u_s, aconv_s, *, conv_k, tm, halo, seq_tiles):
    i = pl.program_id(0)

    @pl.when(pl.program_id(1) == 0)
    def _():
        seq_start = (i % seq_tiles) == 0
        up = cp_ref[...].astype(F32) * hp_ref[...].astype(F32)
        u_s[0:halo] = jnp.where(seq_start, 0.0, up)
        rc = CONV_ROWS
        for r0 in range(0, tm, rc):
            u_s[halo + r0:halo + r0 + rc] = c_ref[r0:r0 + rc].astype(F32) * h_ref[r0:r0 + rc].astype(F32)
        for r0 in range(0, tm, rc):
            lo = halo - (conv_k - 1) + r0
            conv = cw_ref[0:1, :] * u_s[lo:lo + rc]
            for j in range(1, conv_k):
                conv = conv + cw_ref[j:j + 1, :] * u_s[lo + j:lo + j + rc]
            aconv_s[r0:r0 + rc] = (b_ref[r0:r0 + rc].astype(F32) * conv).astype(BF16)

    yc = jnp.dot(aconv_s[...], wc_ref[...], preferred_element_type=F32)
    ya = jnp.dot(oatt_ref[...], wa_ref[...], preferred_element_type=F32)
    out_ref[...] = (sgc_ref[...].astype(F32) * yc + sga_ref[...].astype(F32) * ya).astype(out_ref.dtype)


def _merge(cin, gates, oatt, conv_w, wc, wa, layer, seq, d_model, *, tm, tn, name):
    t = cin.shape[0]
    conv_k, cc = conv_w.shape[-2:]
    aw = oatt.shape[-1]
    tm, tn = _pick(seq, tm), _pick(d_model, tn)
    halo = 16
    assert cc % LANES == 0 and tm % halo == 0 and conv_k - 1 <= halo and tm % CONV_ROWS == 0
    nj = d_model // tn

    def colblk(c):
        return pl.BlockSpec((tm, cc), lambda i, j: (i, c))

    def halo_blk(c):
        return pl.BlockSpec((halo, cc), lambda i, j: (jnp.maximum(i * (tm // halo) - 1, 0), c))

    in_specs = [colblk(0), colblk(1), colblk(2), halo_blk(1), halo_blk(2),
                pl.BlockSpec((None, conv_k, cc), lambda i, j: (layer, 0, 0)),
                pl.BlockSpec((tm, aw), lambda i, j: (i, 0)),
                pl.BlockSpec((tm, tn), lambda i, j: (i, j)),
                pl.BlockSpec((tm, tn), lambda i, j: (i, nj + j)),
                pl.BlockSpec((None, cc, tn), lambda i, j: (layer, 0, j)),
                pl.BlockSpec((None, aw, tn), lambda i, j: (layer, 0, j))]
    return pl.pallas_call(
        functools.partial(_merge_kernel, conv_k=conv_k, tm=tm, halo=halo, seq_tiles=seq // tm),
        grid=(t // tm, nj),
        in_specs=in_specs,
        out_specs=pl.BlockSpec((tm, tn), lambda i, j: (i, j)),
        out_shape=jax.ShapeDtypeStruct((t, d_model), BF16),
        scratch_shapes=[pltpu.VMEM((tm + halo, cc), F32), pltpu.VMEM((tm, cc), BF16)],
        compiler_params=_params("parallel", "arbitrary"),
        name=name,
    )(cin, cin, cin, cin, cin, conv_w, oatt, gates, gates, wc, wa)


def _route(p):
    epg = len(p) // N_EXPERT_GROUPS
    assert epg == 4
    scores = []
    for g in range(N_EXPERT_GROUPS):
        a, b, c, d = p[g * epg:(g + 1) * epg]
        hi1, lo1, hi2, lo2 = jnp.maximum(a, b), jnp.minimum(a, b), jnp.maximum(c, d), jnp.minimum(c, d)
        scores.append(jnp.maximum(hi1, hi2) + jnp.maximum(jnp.minimum(hi1, hi2), jnp.maximum(lo1, lo2)))
    best, gidx = scores[0], jnp.zeros(scores[0].shape, I32)
    for g in range(1, N_EXPERT_GROUPS):
        take = scores[g] > best
        best = jnp.where(take, scores[g], best)
        gidx = jnp.where(take, g, gidx)
    sel = []
    for k in range(epg):
        s = p[k]
        for g in range(1, N_EXPERT_GROUPS):
            s = jnp.where(gidx == g, p[g * epg + k], s)
        sel.append(s)
    v1, i1 = sel[0], jnp.zeros(sel[0].shape, I32)
    for k in range(1, epg):
        take = sel[k] > v1
        v1 = jnp.where(take, sel[k], v1)
        i1 = jnp.where(take, k, i1)
    v2, i2 = jnp.full(v1.shape, -1.0, F32), jnp.zeros(v1.shape, I32)
    for k in range(epg):
        take = (i1 != k) & (sel[k] > v2)
        v2 = jnp.where(take, sel[k], v2)
        i2 = jnp.where(take, k, i2)
    tot = v1 + v2
    return gidx * epg + i1, gidx * epg + i2, v1 / tot, v2 / tot


def _xattn_kernel(s1_ref, g1_ref, b1_ref, wq_hbm, kv_ref, wo_hbm, g2_ref, b2_ref, wrt_ref, br_ref,
                  x_ref, xp_ref, rw_ref, ridx_ref, cnt_ref, tril_s, carry_s, wq_ref, wo_ref, stq_s, sto_s, wsem, *,
                  heads, alpha, scale, tm, layer):
    i = pl.program_id(0)

    @pl.when(i == 0)
    def _():
        for src, dst, stage in ((wq_hbm, wq_ref, stq_s), (wo_hbm, wo_ref, sto_s)):
            rows = stage.shape[0]
            for r0 in range(0, dst.shape[0], rows):
                cp = pltpu.make_async_copy(src.at[layer, pl.ds(r0, rows), :], stage, wsem)
                cp.start()
                cp.wait()
                dst[pl.ds(r0, rows), :] = stage[...].astype(BF16)

    x1 = _layer_norm(s1_ref[...], g1_ref[...], b1_ref[...])
    q = jnp.dot(x1.astype(BF16), wq_ref[...], preferred_element_type=F32).astype(BF16)
    mw = heads * MEM_HEAD_DIM
    outs = []
    for h in range(heads):
        cs = slice(h * MEM_HEAD_DIM, (h + 1) * MEM_HEAD_DIM)
        k = kv_ref[:, cs]
        v = kv_ref[:, mw + h * MEM_HEAD_DIM:mw + (h + 1) * MEM_HEAD_DIM]
        lg = lax.dot_general(q[:, cs], k, (((1,), (1,)), ((), ())), preferred_element_type=F32) * scale
        m = jnp.max(lg, axis=-1, keepdims=True)
        p = jnp.exp(lg - m)
        l = jnp.sum(p, axis=-1, keepdims=True)
        outs.append((jnp.dot(p.astype(BF16), v, preferred_element_type=F32) / l).astype(BF16))
    o = jnp.concatenate(outs, axis=-1)
    xa = jnp.dot(o, wo_ref[...], preferred_element_type=F32)
    x2 = _layer_norm(alpha * x1 + xa, g2_ref[...], b2_ref[...])
    x_ref[...] = x2
    xp_ref[...] = _pack_halves(x2)

    half = x2.shape[1] // 2
    nt = (((1,), (1,)), ((), ()))
    lg = (lax.dot_general(x2[:, :half].astype(BF16), wrt_ref[:, :half], nt, preferred_element_type=F32)
          + lax.dot_general(x2[:, half:].astype(BF16), wrt_ref[:, half:], nt, preferred_element_type=F32)
          + br_ref[...])
    m = jnp.max(lg, axis=-1, keepdims=True)
    ex = jnp.exp(lg - m)
    probs = ex / jnp.sum(ex, axis=-1, keepdims=True)
    n_e = probs.shape[-1]
    e1, e2, w1, w2 = _route([probs[:, e:e + 1] for e in range(n_e)])
    lane = lax.broadcasted_iota(I32, (tm, LANES), 1)
    rw_ref[...] = jnp.where(lane == 0, w1, jnp.where(lane == 1, w2, 0.0))

    @pl.when(i == 0)
    def _():
        rr = lax.broadcasted_iota(I32, (tm, tm), 0)
        cc = lax.broadcasted_iota(I32, (tm, tm), 1)
        tril_s[...] = jnp.where(cc <= rr, 1.0, 0.0).astype(BF16)
        carry_s[...] = jnp.zeros(carry_s.shape, F32)

    lane_e = lax.broadcasted_iota(I32, (tm, n_e), 1)
    onehot = jnp.where((lane_e == e1) | (lane_e == e2), 1.0, 0.0)
    incl = jnp.dot(tril_s[...], onehot.astype(BF16), preferred_element_type=F32)
    rank = incl - onehot + carry_s[...]
    r1 = jnp.sum(jnp.where(lane_e == e1, rank, 0.0), axis=-1, keepdims=True).astype(I32)
    r2 = jnp.sum(jnp.where(lane_e == e2, rank, 0.0), axis=-1, keepdims=True).astype(I32)
    carry_s[...] = carry_s[...] + incl[tm - 1:tm, :]
    cnt_ref[...] = carry_s[...]

    lane8 = lax.broadcasted_iota(I32, (tm, ROUTE_ROWS), 1)
    cols = [e1, e2, r1 >> 8, r1 & 255, r2 >> 8, r2 & 255]
    tok = jnp.zeros((tm, ROUTE_ROWS), F32)
    for c, val in enumerate(cols):
        tok = jnp.where(lane8 == c, val.astype(F32), tok)
    ro = lax.broadcasted_iota(I32, (ROUTE_ROWS, ROUTE_ROWS), 0)
    co = lax.broadcasted_iota(I32, (ROUTE_ROWS, ROUTE_ROWS), 1)
    mix = jnp.where(((ro == 0) & (co == 0)) | ((ro == 1) & (co == 1)) | ((ro == 2) & (co == 3)) | ((ro == 3) & (co == 5)),
                    1.0, jnp.where(((ro == 2) & (co == 2)) | ((ro == 3) & (co == 4)), 256.0, 0.0))
    ridx_ref[...] = lax.dot_general(mix.astype(BF16), tok.astype(BF16), nt, preferred_element_type=F32).astype(I32)


def _xattn(s1, kv, g1, b1, wq, wo, g2, b2, wrt, br, layer, seq, alpha, *, tm, name):
    t, d = s1.shape
    tm = _pick(seq, tm)
    n_mem, kvw = kv.shape[1:]
    mw = wq.shape[-1]
    n_e = wrt.shape[0]
    heads = mw // MEM_HEAD_DIM
    assert 2 * t + n_e * EXPERT_TILE < 65536
    assert tm % LN_ROWS == 0
    vec = pl.BlockSpec((None, 1, d), lambda i: (layer, 0, 0))
    hbm = pl.BlockSpec(memory_space=pl.ANY)
    row = lambda w: pl.BlockSpec((tm, w), lambda i: (i, 0))
    pieces = 4
    assert d % pieces == 0 and mw % pieces == 0
    return pl.pallas_call(
        functools.partial(_xattn_kernel, heads=heads, alpha=alpha, scale=MEM_HEAD_DIM ** -0.5, tm=tm, layer=layer),
        grid=(t // tm,),
        in_specs=[row(d), vec, vec, hbm,
                  pl.BlockSpec((None, n_mem, kvw), lambda i: (i // (seq // tm), 0, 0)),
                  hbm, vec, vec,
                  pl.BlockSpec((n_e, d), lambda i: (0, 0)),
                  pl.BlockSpec((1, n_e), lambda i: (0, 0))],
        out_specs=[row(d), row(d // 2), row(LANES),
                   pl.BlockSpec((ROUTE_ROWS, tm), lambda i: (0, i)),
                   pl.BlockSpec((1, n_e), lambda i: (0, 0))],
        out_shape=[jax.ShapeDtypeStruct((t, d), F32), jax.ShapeDtypeStruct((t, d // 2), U32),
                   jax.ShapeDtypeStruct((t, LANES), F32), jax.ShapeDtypeStruct((ROUTE_ROWS, t), I32),
                   jax.ShapeDtypeStruct((1, n_e), F32)],
        scratch_shapes=[pltpu.VMEM((tm, tm), BF16), pltpu.VMEM((1, n_e), F32),
                        pltpu.VMEM((d, mw), BF16), pltpu.VMEM((mw, d), BF16),
                        pltpu.VMEM((d // pieces, mw), F32), pltpu.VMEM((mw // pieces, d), F32),
                        pltpu.SemaphoreType.DMA(())],
        compiler_params=_params("arbitrary"),
        name=name,
    )(s1, g1, b1, wq, kv, wo, g2, b2, wrt, br)


def _row_copy(src, s, dst, p, sem):
    return pltpu.make_async_copy(src.at[pl.ds(s, 1), :], dst.at[pl.ds(p, 1), :], sem)


def _dispatch_kernel(off_ref, cnt_ref, pad_ref, na_ref, x_ref, idx_ref, xs_hbm, sem, *, tm, n_e, n_tiles):
    for s in range(tm):
        _row_copy(x_ref, s, xs_hbm, idx_ref[0, s], sem).start()
        _row_copy(x_ref, s, xs_hbm, idx_ref[1, s], sem).start()

    @pl.when(pl.program_id(0) == 0)
    def _():
        for e in range(n_e):
            base = off_ref[e] + cnt_ref[e]
            n_pad = pad_ref[e] - cnt_ref[e]

            def fill(s, c):
                _row_copy(x_ref, 0, xs_hbm, base + s, sem).start()
                return c

            def drain(s, c):
                _row_copy(x_ref, 0, xs_hbm, base, sem).wait()
                return c

            lax.fori_loop(0, n_pad, fill, 0)
            lax.fori_loop(0, n_pad, drain, 0)

        def tail_copy(i):
            return pltpu.make_async_copy(x_ref.at[pl.ds(0, EXPERT_TILE), :],
                                         xs_hbm.at[pl.ds(i * EXPERT_TILE, EXPERT_TILE), :], sem)

        def tail_fill(i, c):
            tail_copy(i).start()
            return c

        def tail_drain(i, c):
            tail_copy(i).wait()
            return c

        lax.fori_loop(na_ref[0], n_tiles, tail_fill, 0)
        lax.fori_loop(na_ref[0], n_tiles, tail_drain, 0)

    pltpu.make_async_copy(xs_hbm.at[pl.ds(0, 2 * tm), :], xs_hbm.at[pl.ds(0, 2 * tm), :], sem).wait()


def _dispatch(xp, ridx, off, cnt, padded, na, n_slots, *, tm, name):
    t, w = xp.shape
    tm = _pick(t, tm)
    n_e = off.shape[0]
    assert tm >= EXPERT_TILE
    return pl.pallas_call(
        functools.partial(_dispatch_kernel, tm=tm, n_e=n_e, n_tiles=n_slots // EXPERT_TILE),
        grid_spec=pltpu.PrefetchScalarGridSpec(
            num_scalar_prefetch=4, grid=(t // tm,),
            in_specs=[pl.BlockSpec((tm, w), lambda i, *_: (i, 0)),
                      pl.BlockSpec((ROUTE_ROWS, tm), lambda i, *_: (0, i), memory_space=pltpu.SMEM)],
            out_specs=pl.BlockSpec(memory_space=pl.ANY),
            scratch_shapes=[pltpu.SemaphoreType.DMA(())]),
        out_shape=jax.ShapeDtypeStruct((n_slots, w), U32),
        compiler_params=_params("arbitrary"),
        name=name,
    )(off, cnt, padded, na, xp, ridx)


N_WEIGHT_PIECES = 6


def _experts_kernel(te_ref, nxt_ref, na_ref, xs_ref, wg_hbm, wu_hbm, wd_hbm, ys_ref,
                    wgb, wub, wdb, st_in, st_out, state, sem, *, layer):
    i = pl.program_id(0)
    hd, hf = st_in.shape[0], st_out.shape[0]

    def piece_copy(e, k):
        if k < 4:
            src = (wg_hbm if k < 2 else wu_hbm).at[layer, e, pl.ds((k % 2) * hd, hd), :]
            return pltpu.make_async_copy(src, st_in, sem)
        return pltpu.make_async_copy(wd_hbm.at[layer, e, pl.ds((k - 4) * hf, hf), :], st_out, sem)

    def for_piece(k, fn):
        for kk in range(N_WEIGHT_PIECES):
            pl.when(k == kk)(functools.partial(fn, kk))

    def start_piece(e, k):
        for_piece(k, lambda kk: piece_copy(e, kk).start())

    def finish_piece(e, k, slot):
        def fn(kk):
            piece_copy(e, kk).wait()
            if kk < 4:
                dst = wgb if kk < 2 else wub
                dst[slot, pl.ds((kk % 2) * hd, hd), :] = st_in[...].astype(BF16)
            else:
                wdb[slot, pl.ds((kk - 4) * hf, hf), :] = st_out[...].astype(BF16)

        for_piece(k, fn)

    def advance(e, slot):
        k = state[1]
        finish_piece(e, k, slot)
        state[1] = k + 1

        @pl.when(k + 1 < N_WEIGHT_PIECES)
        def _():
            start_piece(e, k + 1)

    def finish_all(e, slot):
        lax.fori_loop(state[1], N_WEIGHT_PIECES, lambda _, c: (advance(e, slot), c)[1], 0)

    def begin(e):
        state[1] = 0
        state[2] = e

        @pl.when(e >= 0)
        def _():
            start_piece(e, 0)

    active = i < na_ref[0]

    @pl.when(i == 0)
    def _():
        state[0] = 1
        begin(te_ref[0])

    switch = active & ((i == 0) | (te_ref[i] != te_ref[jnp.maximum(i - 1, 0)]))

    @pl.when(switch)
    def _():
        slot = 1 - state[0]
        finish_all(te_ref[i], slot)
        state[0] = slot
        begin(nxt_ref[i])

    @pl.when(active & jnp.logical_not(switch) & (state[2] >= 0) & (state[1] < N_WEIGHT_PIECES))
    def _():
        advance(state[2], 1 - state[0])

    @pl.when(active)
    def _():
        slot = state[0]
        hi, lo = _unpack_halves(xs_ref[...])
        hi, lo = hi.astype(BF16), lo.astype(BF16)
        half = hi.shape[1]
        hg = (jnp.dot(hi, wgb[slot, :half], preferred_element_type=F32)
              + jnp.dot(lo, wgb[slot, half:], preferred_element_type=F32))
        hu = (jnp.dot(hi, wub[slot, :half], preferred_element_type=F32)
              + jnp.dot(lo, wub[slot, half:], preferred_element_type=F32))
        hid = (hg * _sigmoid(hg) * hu).astype(BF16)
        ys_ref[...] = _pack_halves(jnp.dot(hid, wdb[slot], preferred_element_type=F32))

    @pl.when(jnp.logical_not(active))
    def _():
        ys_ref[...] = jnp.zeros(ys_ref.shape, ys_ref.dtype)


def _experts(xs, te, nxt, na, wg, wu, wd, layer, *, name):
    n_slots, w = xs.shape
    d, f = wg.shape[2:]
    tm = EXPERT_TILE
    nt = n_slots // tm
    assert d % 2 == 0 and f % 2 == 0

    def tile(i, te_ref, nxt_ref, na_ref):
        return (jnp.minimum(i, na_ref[0] - 1), 0)

    hbm = pl.BlockSpec(memory_space=pl.ANY)
    return pl.pallas_call(
        functools.partial(_experts_kernel, layer=layer),
        grid_spec=pltpu.PrefetchScalarGridSpec(
            num_scalar_prefetch=3, grid=(nt,),
            in_specs=[pl.BlockSpec((tm, w), tile), hbm, hbm, hbm],
            out_specs=pl.BlockSpec((tm, w), lambda i, *_: (i, 0)),
            scratch_shapes=[pltpu.VMEM((2, d, f), BF16), pltpu.VMEM((2, d, f), BF16), pltpu.VMEM((2, f, d), BF16),
                            pltpu.VMEM((d // 2, f), F32), pltpu.VMEM((f // 2, d), F32),
                            pltpu.SMEM((3,), I32), pltpu.SemaphoreType.DMA(())]),
        out_shape=jax.ShapeDtypeStruct((n_slots, w), U32),
        compiler_params=_params("arbitrary"),
        name=name,
    )(te, nxt, na, xs, wg, wu, wd)


def _combine_kernel(idx_ref, nidx_ref, x_ref, rw_ref, g_ref, b_ref, ys_hbm, o_ref, ob_ref, buf_a, buf_b, sem, *,
                    tm, alpha):
    i = pl.program_id(0)
    n = pl.num_programs(0)

    def start_rows(ref, buf, slot_sem, r0, rows):
        for s in range(r0, r0 + rows):
            _row_copy(ys_hbm, ref[0, s], buf, s, slot_sem).start()
            _row_copy(ys_hbm, ref[1, s], buf, tm + s, slot_sem).start()

    def wait_tile(buf, slot_sem):
        pltpu.make_async_copy(ys_hbm.at[pl.ds(0, 2 * tm), :], buf, slot_sem).wait()

    @pl.when(i == 0)
    def _():
        def body(s, c):
            _row_copy(ys_hbm, idx_ref[0, s], buf_a, s, sem.at[0]).start()
            _row_copy(ys_hbm, idx_ref[1, s], buf_a, tm + s, sem.at[0]).start()
            return c

        lax.fori_loop(0, tm, body, 0, unroll=DMA_UNROLL)

    for parity, (cur, nxt) in enumerate(((buf_a, buf_b), (buf_b, buf_a))):
        @pl.when(i % 2 == parity)
        def _(cur=cur, nxt=nxt, parity=parity):
            wait_tile(cur, sem.at[parity])
            _combine_rows(cur, nidx_ref, nxt, sem.at[1 - parity], start_rows, x_ref, rw_ref, g_ref, b_ref,
                          o_ref, ob_ref, tm=tm, alpha=alpha)

            @pl.when(i == n - 1)
            def _():
                wait_tile(nxt, sem.at[1 - parity])


def _combine_rows(buf, nidx_ref, nxt_buf, nxt_sem, start_rows, x_ref, rw_ref, g_ref, b_ref, o_ref, ob_ref, *,
                  tm, alpha):
    rc = LN_ROWS
    half = buf.shape[-1]
    d = 2 * half

    for r0 in range(0, tm, rc):
        start_rows(nidx_ref, nxt_buf, nxt_sem, r0, rc)
        sl = pl.ds(r0, rc)
        hi1, lo1 = _unpack_halves(buf[sl, :])
        hi2, lo2 = _unpack_halves(buf[pl.ds(tm + r0, rc), :])
        w1 = rw_ref[sl, 0:1]
        w2 = rw_ref[sl, 1:2]
        s_hi = alpha * x_ref[sl, :half] + (w1 * hi1 + w2 * hi2)
        s_lo = alpha * x_ref[sl, half:] + (w1 * lo1 + w2 * lo2)
        mu = (jnp.sum(s_hi, axis=-1, keepdims=True) + jnp.sum(s_lo, axis=-1, keepdims=True)) / d
        c_hi, c_lo = s_hi - mu, s_lo - mu
        var = (jnp.sum(c_hi * c_hi, axis=-1, keepdims=True) + jnp.sum(c_lo * c_lo, axis=-1, keepdims=True)) / d
        inv = lax.rsqrt(var + LN_EPS)
        y_hi = c_hi * inv * g_ref[:, :half] + b_ref[:, :half]
        y_lo = c_lo * inv * g_ref[:, half:] + b_ref[:, half:]
        o_ref[sl, :half] = y_hi
        o_ref[sl, half:] = y_lo
        ob_ref[sl, :half] = y_hi.astype(BF16)
        ob_ref[sl, half:] = y_lo.astype(BF16)


def _combine(x2, rw, pos, ys, g, b, layer, alpha, *, tm, name):
    t, d = x2.shape
    tm = _pick(t, tm)
    nsteps = t // tm
    w = ys.shape[1]
    assert tm % LN_ROWS == 0
    row = lambda width: pl.BlockSpec((tm, width), lambda i: (i, 0))
    vec = pl.BlockSpec((None, 1, d), lambda i: (layer, 0, 0))
    return pl.pallas_call(
        functools.partial(_combine_kernel, tm=tm, alpha=alpha),
        grid=(nsteps,),
        in_specs=[pl.BlockSpec((ROUTE_ROWS, tm), lambda i: (0, i), memory_space=pltpu.SMEM),
                  pl.BlockSpec((ROUTE_ROWS, tm), lambda i: (0, jnp.minimum(i + 1, nsteps - 1)),
                               memory_space=pltpu.SMEM),
                  row(d), row(LANES), vec, vec,
                  pl.BlockSpec(memory_space=pl.ANY)],
        out_specs=[row(d), row(d)],
        out_shape=[jax.ShapeDtypeStruct((t, d), F32), jax.ShapeDtypeStruct((t, d), BF16)],
        scratch_shapes=[pltpu.VMEM((2 * tm, w), U32), pltpu.VMEM((2 * tm, w), U32), pltpu.SemaphoreType.DMA((2,))],
        compiler_params=_params("arbitrary"),
        name=name,
    )(pos, pos, x2, rw, g, b, ys)


def _expert_layout(counts, ridx, n_tiles):
    cnt = counts.reshape(-1).astype(I32)
    padded = (cnt + EXPERT_TILE - 1) // EXPERT_TILE * EXPERT_TILE
    ends = jnp.cumsum(padded)
    off = ends - padded
    tile_ends = ends // EXPERT_TILE
    na = tile_ends[-1:]
    tiles = jnp.minimum(jnp.arange(n_tiles, dtype=I32), na[0] - 1)
    expert_of = lambda tile: jnp.sum((tile[:, None] >= tile_ends[None, :]).astype(I32), axis=1)
    te = expert_of(tiles)
    seg_end = jnp.sum(jnp.where(te[:, None] == jnp.arange(cnt.shape[0], dtype=I32)[None, :], tile_ends[None, :], 0), axis=1)
    nxt = jnp.where(seg_end < na[0], expert_of(seg_end), -1)
    e, rank = ridx[0:2], ridx[2:4]
    base = jnp.sum(jnp.where(e[:, None, :] == jnp.arange(cnt.shape[0], dtype=I32)[None, :, None], off[None, :, None], 0),
                   axis=1)
    pos = jnp.concatenate([base + rank, jnp.zeros((ROUTE_ROWS - 2, ridx.shape[1]), I32)], axis=0)
    return off, cnt, padded, te, nxt, na, pos


def kernel(x, mem, w_in, conv_w, w_conv_proj, w_attn_proj, w_out, rel_bias, ln1_g, ln1_b, w_mem_q, w_mem_kv,
           w_mem_o, ln2_g, ln2_b, w_router, b_router, w_gate, w_up, w_down, ln3_g, ln3_b):
    batch, seq, d = x.shape
    depth = w_in.shape[0]
    t = batch * seq
    alpha = (2 * depth) ** 0.25
    cc = conv_w.shape[-1]
    n_groups = len(ATTN_GROUPS)
    aw = w_attn_proj.shape[1]
    hpg = aw // HEAD_DIM
    n_e = w_router.shape[1]
    assert rel_bias.shape[1] == n_groups * hpg
    qkv_w = 3 * n_groups * aw
    n_tiles = -(-(2 * t) // EXPERT_TILE) + n_e
    n_slots = n_tiles * EXPERT_TILE

    bf = lambda a: a.astype(BF16)
    wc_b, wa_b = bf(w_conv_proj), bf(w_attn_proj)
    wrt_b = bf(w_router.T)
    br = b_router.reshape(1, -1).astype(F32)
    vec3 = lambda a: a.reshape(depth, 1, d)
    ln1_g, ln1_b, ln2_g, ln2_b, ln3_g, ln3_b = map(vec3, (ln1_g, ln1_b, ln2_g, ln2_b, ln3_g, ln3_b))
    tbl = rel_bias.T.astype(F32)
    mem_b = bf(mem).reshape(batch * mem.shape[1], d)

    xf = x.reshape(t, d)
    xb = bf(xf)
    for l in range(depth):
        cin = _matmul(xb, w_in, l, BF16, tm=1024, tn=768, col0=0, ncols=3 * cc, name=f"inproj_conv{l}")
        qkv = _matmul(xb, w_in, l, F32, tm=1024, tn=768, col0=3 * cc, ncols=qkv_w, name=f"inproj_qkv{l}")
        gates = _matmul(xb, w_in, l, BF16, tm=1024, tn=1024, col0=3 * cc + qkv_w, ncols=2 * d, sigmoid=True,
                        name=f"inproj_gate{l}")
        others = []
        for g, (window, r) in enumerate(ATTN_GROUPS):
            last = g == n_groups - 1
            res = _dilated_attention(qkv, tbl[g * hpg:(g + 1) * hpg], g, n_groups, hpg, batch, seq, window, r,
                                     others if last else [], name=f"dattn{l}_{g}")
            if not last:
                others += list(res)
        merged = _merge(cin, gates, res, conv_w, wc_b, wa_b, l, seq, d, tm=1024, tn=512, name=f"merge{l}")
        s1 = _matmul(merged, w_out, l, F32, tm=512, tn=1024, resid=xf, alpha=alpha, name=f"outproj{l}")
        kv = _matmul(mem_b, w_mem_kv, l, BF16, tm=512, tn=1024, name=f"memkv{l}")
        kv = kv.reshape(batch, mem.shape[1], kv.shape[-1])
        x2, x2p, rw, ridx, counts = _xattn(s1, kv, ln1_g, ln1_b, w_mem_q, w_mem_o, ln2_g, ln2_b, wrt_b, br, l, seq, alpha,
                                           tm=256, name=f"xattn{l}")
        off, cnt, padded, te, nxt, na, pos = _expert_layout(counts, ridx, n_tiles)
        xs = _dispatch(x2p, pos, off, cnt, padded, na, n_slots, tm=512, name=f"dispatch{l}")
        ys = _experts(xs, te, nxt, na, w_gate, w_up, w_down, l, name=f"experts{l}")
        xf, xb = _combine(x2, rw, pos, ys, ln3_g, ln3_b, l, alpha, tm=256, name=f"combine{l}")
    return xf.reshape(batch, seq, d)
```

```python
import functools
import math

import jax
import jax.numpy as jnp
from jax import lax
from jax.experimental import pallas as pl
from jax.experimental.pallas import tpu as pltpu

HEAD_DIM = 128
ATTN_GROUPS = ((128, 1), (512, 4), (2048, 16))
ATTN_BLOCK = 128
N_BUCKETS = 32
MAX_DISTANCE = 2048
MEM_HEAD_DIM = 128
N_EXPERT_GROUPS = 4
LN_EPS = 1e-5
NEG_INF = -1e30

VMEM_LIMIT_BYTES = 56 * 1024 * 1024
LANES = 128
ROUTE_ROWS = 8
EXPERT_TILE = 256
LN_ROWS = 64
MERGE_ROWS = 32
CONV_ROWS = 64
DMA_UNROLL = 8
PACK_ROWS = 64

F32, BF16, U32, I32 = jnp.float32, jnp.bfloat16, jnp.uint32, jnp.int32


def _params(*sem):
    return pltpu.CompilerParams(dimension_semantics=sem, vmem_limit_bytes=VMEM_LIMIT_BYTES)


def _pick(n, pref):
    if n <= pref:
        return n
    t = pref
    while n % t:
        t //= 2
    return t


def _sigmoid(x):
    return 1.0 / (1.0 + jnp.exp(-x))


def _layer_norm(x, g, b):
    mu = jnp.mean(x, axis=-1, keepdims=True)
    xc = x - mu
    var = jnp.mean(xc * xc, axis=-1, keepdims=True)
    return xc * lax.rsqrt(var + LN_EPS) * g + b


def _pack_halves(x):
    half = x.shape[1] // 2
    hi = lax.bitcast_convert_type(x[:, :half].astype(BF16).astype(F32), U32)
    lo = lax.bitcast_convert_type(x[:, half:].astype(BF16).astype(F32), U32)
    return hi | (lo >> 16)


def _unpack_halves(p):
    hi = lax.bitcast_convert_type(p & jnp.uint32(0xFFFF0000), F32)
    lo = lax.bitcast_convert_type(p << 16, F32)
    return hi, lo


def _mm_kernel(*refs, sigmoid, alpha, layer, col0, tn):
    if alpha is None:
        a_ref, w_hbm, o_ref, wst_s, wb_s, sem = refs
    else:
        a_ref, w_hbm, r_ref, o_ref, wst_s, wb_s, sem = refs
    j = pl.program_id(0)

    def w_copy(jj):
        cols = pl.ds(pl.multiple_of(col0 + jj * tn, LANES), tn)
        return pltpu.make_async_copy(w_hbm.at[layer, :, cols], wst_s, sem)

    @pl.when(pl.program_id(1) == 0)
    def _():
        @pl.when(j == 0)
        def _():
            w_copy(0).start()

        w_copy(j).wait()
        wb_s[...] = wst_s[...].astype(BF16)

        @pl.when(j + 1 < pl.num_programs(0))
        def _():
            w_copy(j + 1).start()

    acc = jnp.dot(a_ref[...], wb_s[...], preferred_element_type=F32)
    if alpha is not None:
        acc = acc + alpha * r_ref[...]
    if sigmoid:
        acc = _sigmoid(acc)
    o_ref[...] = acc.astype(o_ref.dtype)


def _matmul(a, w, layer, out_dtype, *, tm, tn, col0=0, ncols=None, sigmoid=False, resid=None, alpha=None, name):
    m, k = a.shape
    n = w.shape[-1] - col0 if ncols is None else ncols
    tm, tn = _pick(m, tm), _pick(n, tn)
    assert col0 % LANES == 0 and tn % LANES == 0
    in_specs = [pl.BlockSpec((tm, k), lambda j, i: (i, 0)),
                pl.BlockSpec(memory_space=pl.ANY)]
    args = [a, w]
    if resid is not None:
        in_specs.append(pl.BlockSpec((tm, tn), lambda j, i: (i, j)))
        args.append(resid)
    return pl.pallas_call(
        functools.partial(_mm_kernel, sigmoid=sigmoid, alpha=alpha, layer=layer, col0=col0, tn=tn),
        grid=(n // tn, m // tm),
        in_specs=in_specs,
        out_specs=pl.BlockSpec((tm, tn), lambda j, i: (i, j)),
        out_shape=jax.ShapeDtypeStruct((m, n), out_dtype),
        scratch_shapes=[pltpu.VMEM((k, tn), F32), pltpu.VMEM((k, tn), BF16), pltpu.SemaphoreType.DMA(())],
        compiler_params=_params("arbitrary", "arbitrary"),
        name=name,
    )(*args)


def _dattn_kernel(bucket_ref, tbl_ref, q_ref, k_ref, v_ref, kp_ref, vp_ref, *rest, r, pt, span, scale, n_other):
    others = rest[:2 * n_other]
    if n_other:
        oatt_ref, bias_s, qs, kcat, vcat, o_ref, lse_ref = rest[2 * n_other:]
    else:
        o_ref, lse_ref, bias_s, qs, kcat, vcat = rest
    blk = ATTN_BLOCK
    n = pl.program_id(1)
    h = pl.program_id(2)
    ch = pt // r
    nb = ch // blk

    @pl.when((pl.program_id(0) == 0) & (n == 0))
    def _():
        bk = bucket_ref[...]
        row = lax.broadcasted_iota(I32, (blk, 2 * blk), 0)
        col = lax.broadcasted_iota(I32, (blk, 2 * blk), 1)
        rel = row + blk - col
        band = (rel >= 0) & (rel <= span)
        bias = jnp.zeros((blk, 2 * blk), F32)
        for b in range(N_BUCKETS):
            bias = jnp.where(bk == b, tbl_ref[h, b], bias)
        bias_s[h, 0] = jnp.where(band, bias, NEG_INF)
        bias_s[h, 1] = jnp.where(band & (col >= blk), bias, NEG_INF)

    def rows(ref, start, size):
        if r == 1:
            return ref[pl.ds(start, size), :]
        return ref[pl.ds(start, size, stride=r), :]

    for rho in range(r):
        qs[...] = rows(q_ref, rho, ch).astype(BF16)
        kcat[0:blk] = rows(kp_ref, rho, blk).astype(BF16)
        kcat[blk:] = rows(k_ref, rho, ch).astype(BF16)
        vcat[0:blk] = rows(vp_ref, rho, blk).astype(BF16)
        vcat[blk:] = rows(v_ref, rho, ch).astype(BF16)
        for b in range(nb):
            first = jnp.where(n == 0, 1, 0) if b == 0 else 0
            q = qs[b * blk:(b + 1) * blk]
            k = kcat[b * blk:(b + 2) * blk]
            v = vcat[b * blk:(b + 2) * blk]
            lg = lax.dot_general(q, k, (((1,), (1,)), ((), ())), preferred_element_type=F32) * scale
            lg = lg + bias_s[h, first]
            m = jnp.max(lg, axis=-1, keepdims=True)
            p = jnp.exp(lg - m)
            l = jnp.sum(p, axis=-1, keepdims=True)
            o = jnp.dot(p.astype(BF16), v, preferred_element_type=F32) / l
            lse = jnp.broadcast_to(m + jnp.log(l), (blk, LANES))
            start = rho + b * blk * r
            if r == 1:
                o_ref[pl.ds(start, blk), :] = o
                lse_ref[pl.ds(start, blk), :] = lse
            else:
                o_ref[pl.ds(start, blk, stride=r), :] = o
                lse_ref[pl.ds(start, blk, stride=r), :] = lse

    if n_other:
        rc = MERGE_ROWS

        def merge_rows(c, carry):
            sl = pl.ds(pl.multiple_of(c * rc, rc), rc)
            ls = [others[2 * g + 1][sl, :] for g in range(n_other)] + [lse_ref[sl, :]]
            os_ = [others[2 * g][sl, :] for g in range(n_other)] + [o_ref[sl, :]]
            mx = functools.reduce(jnp.maximum, ls)
            es = [jnp.exp(x - mx) for x in ls]
            den = functools.reduce(lambda a, b: a + b, es)
            acc = es[0] * os_[0]
            for e, o in zip(es[1:], os_[1:]):
                acc = acc + e * o
            oatt_ref[sl, :] = (acc / den).astype(oatt_ref.dtype)
            return carry

        lax.fori_loop(0, pt // rc, merge_rows, 0, unroll=4)


def _t5_bucket(dist):
    max_exact = N_BUCKETS // 2
    d = jnp.maximum(dist, 0)
    log_part = jnp.log(jnp.maximum(d, 1).astype(F32) / max_exact) / math.log(MAX_DISTANCE / max_exact)
    large = jnp.minimum(max_exact + (log_part * (N_BUCKETS - max_exact)).astype(I32), N_BUCKETS - 1)
    return jnp.where(d < max_exact, d, large)


def _dilated_attention(qkv, tbl, g, n_groups, hpg, batch, seq, window, r, others, name):
    blk = ATTN_BLOCK
    t = qkv.shape[0]
    pt = blk * ATTN_GROUPS[-1][1]
    assert seq % pt == 0 and pt % (blk * r) == 0 and HEAD_DIM == LANES
    tiles = seq // pt
    span = window // r
    qi = jnp.arange(blk)[:, None] + blk
    ki = jnp.arange(2 * blk)[None, :]
    bucket = _t5_bucket((qi - ki) * r).astype(I32)
    heads = n_groups * hpg
    prev_rows = blk * r
    ppt = pt // prev_rows

    def cur(which):
        return pl.BlockSpec((pt, LANES), lambda b, n, h: (b * tiles + n, which * heads + g * hpg + h))

    def prev(which):
        return pl.BlockSpec((prev_rows, LANES),
                            lambda b, n, h: (jnp.maximum((b * tiles + n) * ppt - 1, 0), which * heads + g * hpg + h))

    out = pl.BlockSpec((pt, LANES), lambda b, n, h: (b * tiles + n, h))
    scratch = [pltpu.VMEM((hpg, 2, blk, 2 * blk), F32),
               pltpu.VMEM((pt // r, LANES), BF16),
               pltpu.VMEM((pt // r + blk, LANES), BF16),
               pltpu.VMEM((pt // r + blk, LANES), BF16)]
    if others:
        assert pt % MERGE_ROWS == 0
        out_specs, out_shape = out, jax.ShapeDtypeStruct((t, hpg * LANES), BF16)
        scratch += [pltpu.VMEM((pt, LANES), F32)] * 2
    else:
        out_specs, out_shape = [out, out], [jax.ShapeDtypeStruct((t, hpg * LANES), F32)] * 2
    return pl.pallas_call(
        functools.partial(_dattn_kernel, r=r, pt=pt, span=span, scale=HEAD_DIM ** -0.5, n_other=len(others) // 2),
        grid=(batch, tiles, hpg),
        in_specs=[pl.BlockSpec((blk, 2 * blk), lambda b, n, h: (0, 0)),
                  pl.BlockSpec(memory_space=pltpu.SMEM),
                  cur(0), cur(1), cur(2), prev(1), prev(2)] + [out] * len(others),
        out_specs=out_specs,
        out_shape=out_shape,
        scratch_shapes=scratch,
        compiler_params=_params("arbitrary", "arbitrary", "arbitrary"),
        name=name,
    )(bucket, tbl, qkv, qkv, qkv, qkv, qkv, *others)


def _merge_kernel(b_ref, c_ref, h_ref, cp_ref, hp_ref, cw_ref, oatt_ref, sgc_ref, sga_ref, wc_ref, wa_ref, out_ref,
                  u_s, aconv_s, *, conv_k, tm, halo, seq_tiles):
    i = pl.program_id(0)

    @pl.when(pl.program_id(1) == 0)
    def _():
        seq_start = (i % seq_tiles) == 0
        up = cp_ref[...].astype(F32) * hp_ref[...].astype(F32)
        u_s[0:halo] = jnp.where(seq_start, 0.0, up)
        rc = CONV_ROWS
        for r0 in range(0, tm, rc):
            u_s[halo + r0:halo + r0 + rc] = c_ref[r0:r0 + rc].astype(F32) * h_ref[r0:r0 + rc].astype(F32)
        for r0 in range(0, tm, rc):
            lo = halo - (conv_k - 1) + r0
            conv = cw_ref[0:1, :] * u_s[lo:lo + rc]
            for j in range(1, conv_k):
                conv = conv + cw_ref[j:j + 1, :] * u_s[lo + j:lo + j + rc]
            aconv_s[r0:r0 + rc] = (b_ref[r0:r0 + rc].astype(F32) * conv).astype(BF16)

    yc = jnp.dot(aconv_s[...], wc_ref[...], preferred_element_type=F32)
    ya = jnp.dot(oatt_ref[...], wa_ref[...], preferred_element_type=F32)
    out_ref[...] = (sgc_ref[...].astype(F32) * yc + sga_ref[...].astype(F32) * ya).astype(out_ref.dtype)


def _merge(cin, gates, oatt, conv_w, wc, wa, layer, seq, d_model, *, tm, tn, name):
    t = cin.shape[0]
    conv_k, cc = conv_w.shape[-2:]
    aw = oatt.shape[-1]
    tm, tn = _pick(seq, tm), _pick(d_model, tn)
    halo = 16
    assert cc % LANES == 0 and tm % halo == 0 and conv_k - 1 <= halo and tm % CONV_ROWS == 0
    nj = d_model // tn

    def colblk(c):
        return pl.BlockSpec((tm, cc), lambda i, j: (i, c))

    def halo_blk(c):
        return pl.BlockSpec((halo, cc), lambda i, j: (jnp.maximum(i * (tm // halo) - 1, 0), c))

    in_specs = [colblk(0), colblk(1), colblk(2), halo_blk(1), halo_blk(2),
                pl.BlockSpec((None, conv_k, cc), lambda i, j: (layer, 0, 0)),
                pl.BlockSpec((tm, aw), lambda i, j: (i, 0)),
                pl.BlockSpec((tm, tn), lambda i, j: (i, j)),
                pl.BlockSpec((tm, tn), lambda i, j: (i, nj + j)),
                pl.BlockSpec((None, cc, tn), lambda i, j: (layer, 0, j)),
                pl.BlockSpec((None, aw, tn), lambda i, j: (layer, 0, j))]
    return pl.pallas_call(
        functools.partial(_merge_kernel, conv_k=conv_k, tm=tm, halo=halo, seq_tiles=seq // tm),
        grid=(t // tm, nj),
        in_specs=in_specs,
        out_specs=pl.BlockSpec((tm, tn), lambda i, j: (i, j)),
        out_shape=jax.ShapeDtypeStruct((t, d_model), BF16),
        scratch_shapes=[pltpu.VMEM((tm + halo, cc), F32), pltpu.VMEM((tm, cc), BF16)],
        compiler_params=_params("parallel", "arbitrary"),
        name=name,
    )(cin, cin, cin, cin, cin, conv_w, oatt, gates, gates, wc, wa)


def _route(p):
    epg = len(p) // N_EXPERT_GROUPS
    assert epg == 4
    scores = []
    for g in range(N_EXPERT_GROUPS):
        a, b, c, d = p[g * epg:(g + 1) * epg]
        hi1, lo1, hi2, lo2 = jnp.maximum(a, b), jnp.minimum(a, b), jnp.maximum(c, d), jnp.minimum(c, d)
        scores.append(jnp.maximum(hi1, hi2) + jnp.maximum(jnp.minimum(hi1, hi2), jnp.maximum(lo1, lo2)))
    best, gidx = scores[0], jnp.zeros(scores[0].shape, I32)
    for g in range(1, N_EXPERT_GROUPS):
        take = scores[g] > best
        best = jnp.where(take, scores[g], best)
        gidx = jnp.where(take, g, gidx)
    sel = []
    for k in range(epg):
        s = p[k]
        for g in range(1, N_EXPERT_GROUPS):
            s = jnp.where(gidx == g, p[g * epg + k], s)
        sel.append(s)
    v1, i1 = sel[0], jnp.zeros(sel[0].shape, I32)
    for k in range(1, epg):
        take = sel[k] > v1
        v1 = jnp.where(take, sel[k], v1)
        i1 = jnp.where(take, k, i1)
    v2, i2 = jnp.full(v1.shape, -1.0, F32), jnp.zeros(v1.shape, I32)
    for k in range(epg):
        take = (i1 != k) & (sel[k] > v2)
        v2 = jnp.where(take, sel[k], v2)
        i2 = jnp.where(take, k, i2)
    tot = v1 + v2
    return gidx * epg + i1, gidx * epg + i2, v1 / tot, v2 / tot


def _xattn_kernel(s1_ref, g1_ref, b1_ref, wq_hbm, kv_ref, wo_hbm, g2_ref, b2_ref, wrt_ref, br_ref,
                  x_ref, xb_ref, rw_ref, ridx_ref, cnt_ref, tril_s, carry_s, wq_ref, wo_ref, stq_s, sto_s, wsem, *,
                  heads, alpha, scale, tm, layer):
    i = pl.program_id(0)

    @pl.when(i == 0)
    def _():
        for src, dst, stage in ((wq_hbm, wq_ref, stq_s), (wo_hbm, wo_ref, sto_s)):
            rows = stage.shape[0]
            for r0 in range(0, dst.shape[0], rows):
                cp = pltpu.make_async_copy(src.at[layer, pl.ds(r0, rows), :], stage, wsem)
                cp.start()
                cp.wait()
                dst[pl.ds(r0, rows), :] = stage[...].astype(BF16)

    x1 = _layer_norm(s1_ref[...], g1_ref[...], b1_ref[...])
    q = jnp.dot(x1.astype(BF16), wq_ref[...], preferred_element_type=F32).astype(BF16)
    mw = heads * MEM_HEAD_DIM
    outs = []
    for h in range(heads):
        cs = slice(h * MEM_HEAD_DIM, (h + 1) * MEM_HEAD_DIM)
        k = kv_ref[:, cs]
        v = kv_ref[:, mw + h * MEM_HEAD_DIM:mw + (h + 1) * MEM_HEAD_DIM]
        lg = lax.dot_general(q[:, cs], k, (((1,), (1,)), ((), ())), preferred_element_type=F32) * scale
        m = jnp.max(lg, axis=-1, keepdims=True)
        p = jnp.exp(lg - m)
        l = jnp.sum(p, axis=-1, keepdims=True)
        outs.append((jnp.dot(p.astype(BF16), v, preferred_element_type=F32) / l).astype(BF16))
    o = jnp.concatenate(outs, axis=-1)
    xa = jnp.dot(o, wo_ref[...], preferred_element_type=F32)
    x2 = _layer_norm(alpha * x1 + xa, g2_ref[...], b2_ref[...])
    x_ref[...] = x2
    half = x2.shape[1] // 2
    x2_hi, x2_lo = x2[:, :half].astype(BF16), x2[:, half:].astype(BF16)
    xb_ref[:, :half] = x2_hi
    xb_ref[:, half:] = x2_lo

    nt = (((1,), (1,)), ((), ()))
    lg = (lax.dot_general(x2_hi, wrt_ref[:, :half], nt, preferred_element_type=F32)
          + lax.dot_general(x2_lo, wrt_ref[:, half:], nt, preferred_element_type=F32)
          + br_ref[...])
    m = jnp.max(lg, axis=-1, keepdims=True)
    ex = jnp.exp(lg - m)
    probs = ex / jnp.sum(ex, axis=-1, keepdims=True)
    n_e = probs.shape[-1]
    e1, e2, w1, w2 = _route([probs[:, e:e + 1] for e in range(n_e)])
    lane = lax.broadcasted_iota(I32, (tm, LANES), 1)
    rw_ref[...] = jnp.where(lane == 0, w1, jnp.where(lane == 1, w2, 0.0))

    @pl.when(i == 0)
    def _():
        rr = lax.broadcasted_iota(I32, (tm, tm), 0)
        cc = lax.broadcasted_iota(I32, (tm, tm), 1)
        tril_s[...] = jnp.where(cc <= rr, 1.0, 0.0).astype(BF16)
        carry_s[...] = jnp.zeros(carry_s.shape, F32)

    lane_e = lax.broadcasted_iota(I32, (tm, n_e), 1)
    onehot = jnp.where((lane_e == e1) | (lane_e == e2), 1.0, 0.0)
    incl = jnp.dot(tril_s[...], onehot.astype(BF16), preferred_element_type=F32)
    rank = incl - onehot + carry_s[...]
    r1 = jnp.sum(jnp.where(lane_e == e1, rank, 0.0), axis=-1, keepdims=True).astype(I32)
    r2 = jnp.sum(jnp.where(lane_e == e2, rank, 0.0), axis=-1, keepdims=True).astype(I32)
    carry_s[...] = carry_s[...] + incl[tm - 1:tm, :]
    cnt_ref[...] = carry_s[...]

    lane8 = lax.broadcasted_iota(I32, (tm, ROUTE_ROWS), 1)
    cols = [e1, e2, r1 >> 8, r1 & 255, r2 >> 8, r2 & 255]
    tok = jnp.zeros((tm, ROUTE_ROWS), F32)
    for c, val in enumerate(cols):
        tok = jnp.where(lane8 == c, val.astype(F32), tok)
    ro = lax.broadcasted_iota(I32, (ROUTE_ROWS, ROUTE_ROWS), 0)
    co = lax.broadcasted_iota(I32, (ROUTE_ROWS, ROUTE_ROWS), 1)
    mix = jnp.where(((ro == 0) & (co == 0)) | ((ro == 1) & (co == 1)) | ((ro == 2) & (co == 3)) | ((ro == 3) & (co == 5)),
                    1.0, jnp.where(((ro == 2) & (co == 2)) | ((ro == 3) & (co == 4)), 256.0, 0.0))
    ridx_ref[...] = lax.dot_general(mix.astype(BF16), tok.astype(BF16), nt, preferred_element_type=F32).astype(I32)


def _xattn(s1, kv, g1, b1, wq, wo, g2, b2, wrt, br, layer, seq, alpha, *, tm, name):
    t, d = s1.shape
    tm = _pick(seq, tm)
    n_mem, kvw = kv.shape[1:]
    mw = wq.shape[-1]
    n_e = wrt.shape[0]
    heads = mw // MEM_HEAD_DIM
    assert 2 * t + n_e * EXPERT_TILE < 65536
    assert tm % LN_ROWS == 0
    vec = pl.BlockSpec((None, 1, d), lambda i: (layer, 0, 0))
    hbm = pl.BlockSpec(memory_space=pl.ANY)
    row = lambda w: pl.BlockSpec((tm, w), lambda i: (i, 0))
    pieces = 4
    assert d % pieces == 0 and mw % pieces == 0
    return pl.pallas_call(
        functools.partial(_xattn_kernel, heads=heads, alpha=alpha, scale=MEM_HEAD_DIM ** -0.5, tm=tm, layer=layer),
        grid=(t // tm,),
        in_specs=[row(d), vec, vec, hbm,
                  pl.BlockSpec((None, n_mem, kvw), lambda i: (i // (seq // tm), 0, 0)),
                  hbm, vec, vec,
                  pl.BlockSpec((n_e, d), lambda i: (0, 0)),
                  pl.BlockSpec((1, n_e), lambda i: (0, 0))],
        out_specs=[row(d), row(d), row(LANES),
                   pl.BlockSpec((ROUTE_ROWS, tm), lambda i: (0, i)),
                   pl.BlockSpec((1, n_e), lambda i: (0, 0))],
        out_shape=[jax.ShapeDtypeStruct((t, d), F32), jax.ShapeDtypeStruct((t, d), BF16),
                   jax.ShapeDtypeStruct((t, LANES), F32), jax.ShapeDtypeStruct((ROUTE_ROWS, t), I32),
                   jax.ShapeDtypeStruct((1, n_e), F32)],
        scratch_shapes=[pltpu.VMEM((tm, tm), BF16), pltpu.VMEM((1, n_e), F32),
                        pltpu.VMEM((d, mw), BF16), pltpu.VMEM((mw, d), BF16),
                        pltpu.VMEM((d // pieces, mw), F32), pltpu.VMEM((mw // pieces, d), F32),
                        pltpu.SemaphoreType.DMA(())],
        compiler_params=_params("arbitrary"),
        name=name,
    )(s1, g1, b1, wq, kv, wo, g2, b2, wrt, br)


def _row_copy(src, s, dst, p, sem):
    return pltpu.make_async_copy(src.at[pl.ds(s, 1), :], dst.at[pl.ds(p, 1), :], sem)


def _slot_rows(d):
    return d // 2 // LANES + 1


def _slot_copy(src, s, dst, p, sem, ch):
    return pltpu.make_async_copy(src.at[pl.ds(s * ch, ch), :], dst.at[pl.ds(p * ch, ch), :], sem)


def _dispatch_kernel(off_ref, cnt_ref, pad_ref, na_ref, x_ref, idx_ref, xs_hbm, pk, sem, *, tm, n_e, n_tiles, ch):
    half = x_ref.shape[1] // 2
    rc = PACK_ROWS
    for r0 in range(0, tm, rc):
        hi = lax.bitcast_convert_type(x_ref[r0:r0 + rc, :half].astype(F32), U32)
        lo = lax.bitcast_convert_type(x_ref[r0:r0 + rc, half:].astype(F32), U32)
        p = hi | (lo >> 16)
        for c in range(ch - 1):
            pk[pl.ds(r0 * ch + c, rc, stride=ch), :] = p[:, c * LANES:(c + 1) * LANES]
        pk[pl.ds(r0 * ch + ch - 1, rc, stride=ch), :] = jnp.zeros((rc, LANES), U32)
        for s in range(r0, r0 + rc):
            _slot_copy(pk, s, xs_hbm, idx_ref[0, s], sem, ch).start()
            _slot_copy(pk, s, xs_hbm, idx_ref[1, s], sem, ch).start()

    @pl.when(pl.program_id(0) == 0)
    def _():
        for e in range(n_e):
            base = off_ref[e] + cnt_ref[e]
            n_pad = pad_ref[e] - cnt_ref[e]

            def fill(s, c):
                _slot_copy(pk, 0, xs_hbm, base + s, sem, ch).start()
                return c

            def drain(s, c):
                _slot_copy(pk, 0, xs_hbm, base, sem, ch).wait()
                return c

            lax.fori_loop(0, n_pad, fill, 0)
            lax.fori_loop(0, n_pad, drain, 0)

        rows = EXPERT_TILE * ch

        def tail_copy(i):
            return pltpu.make_async_copy(pk.at[pl.ds(0, rows), :],
                                         xs_hbm.at[pl.ds(pl.multiple_of(i * rows, rows), rows), :], sem)

        def tail_fill(i, c):
            tail_copy(i).start()
            return c

        def tail_drain(i, c):
            tail_copy(i).wait()
            return c

        lax.fori_loop(na_ref[0], n_tiles, tail_fill, 0)
        lax.fori_loop(na_ref[0], n_tiles, tail_drain, 0)

    pltpu.make_async_copy(xs_hbm.at[pl.ds(0, 2 * tm * ch), :], xs_hbm.at[pl.ds(0, 2 * tm * ch), :], sem).wait()


def _dispatch(xb, pos, off, cnt, padded, na, n_slots, *, tm, name):
    t, d = xb.shape
    tm = _pick(t, tm)
    n_e = off.shape[0]
    ch = _slot_rows(d)
    assert tm >= EXPERT_TILE and tm % PACK_ROWS == 0 and d % (2 * LANES) == 0
    return pl.pallas_call(
        functools.partial(_dispatch_kernel, tm=tm, n_e=n_e, n_tiles=n_slots // EXPERT_TILE, ch=ch),
        grid_spec=pltpu.PrefetchScalarGridSpec(
            num_scalar_prefetch=4, grid=(t // tm,),
            in_specs=[pl.BlockSpec((tm, d), lambda i, *_: (i, 0)),
                      pl.BlockSpec((ROUTE_ROWS, tm), lambda i, *_: (0, i), memory_space=pltpu.SMEM)],
            out_specs=pl.BlockSpec(memory_space=pl.ANY),
            scratch_shapes=[pltpu.VMEM((tm * ch, LANES), U32), pltpu.SemaphoreType.DMA(())]),
        out_shape=jax.ShapeDtypeStruct((n_slots * ch, LANES), U32),
        compiler_params=_params("arbitrary"),
        name=name,
    )(off, cnt, padded, na, xb, pos)


N_WEIGHT_PIECES = 6


def _experts_kernel(te_ref, nxt_ref, na_ref, xs_ref, wg_hbm, wu_hbm, wd_hbm, ys_ref,
                    wgb, wub, wdb, st_in, st_out, state, sem, *, layer):
    i = pl.program_id(0)
    hd, hf = st_in.shape[0], st_out.shape[0]

    def piece_copy(e, k):
        if k < 4:
            src = (wg_hbm if k < 2 else wu_hbm).at[layer, e, pl.ds((k % 2) * hd, hd), :]
            return pltpu.make_async_copy(src, st_in, sem)
        return pltpu.make_async_copy(wd_hbm.at[layer, e, pl.ds((k - 4) * hf, hf), :], st_out, sem)

    def for_piece(k, fn):
        for kk in range(N_WEIGHT_PIECES):
            pl.when(k == kk)(functools.partial(fn, kk))

    def start_piece(e, k):
        for_piece(k, lambda kk: piece_copy(e, kk).start())

    def finish_piece(e, k, slot):
        def fn(kk):
            piece_copy(e, kk).wait()
            if kk < 4:
                dst = wgb if kk < 2 else wub
                dst[slot, pl.ds((kk % 2) * hd, hd), :] = st_in[...].astype(BF16)
            else:
                wdb[slot, pl.ds((kk - 4) * hf, hf), :] = st_out[...].astype(BF16)

        for_piece(k, fn)

    def advance(e, slot):
        k = state[1]
        finish_piece(e, k, slot)
        state[1] = k + 1

        @pl.when(k + 1 < N_WEIGHT_PIECES)
        def _():
            start_piece(e, k + 1)

    def finish_all(e, slot):
        lax.fori_loop(state[1], N_WEIGHT_PIECES, lambda _, c: (advance(e, slot), c)[1], 0)

    def begin(e):
        state[1] = 0
        state[2] = e

        @pl.when(e >= 0)
        def _():
            start_piece(e, 0)

    active = i < na_ref[0]

    @pl.when(i == 0)
    def _():
        state[0] = 1
        begin(te_ref[0])

    switch = active & ((i == 0) | (te_ref[i] != te_ref[jnp.maximum(i - 1, 0)]))

    @pl.when(switch)
    def _():
        slot = 1 - state[0]
        finish_all(te_ref[i], slot)
        state[0] = slot
        begin(nxt_ref[i])

    @pl.when(active & jnp.logical_not(switch) & (state[2] >= 0) & (state[1] < N_WEIGHT_PIECES))
    def _():
        advance(state[2], 1 - state[0])

    @pl.when(active)
    def _():
        slot = state[0]
        ch = xs_ref.shape[0] // ys_ref.shape[0]
        parts = [_unpack_halves(xs_ref[pl.ds(c, ys_ref.shape[0], stride=ch), :]) for c in range(ch - 1)]
        hi = jnp.concatenate([p[0].astype(BF16) for p in parts], axis=1)
        lo = jnp.concatenate([p[1].astype(BF16) for p in parts], axis=1)
        half = hi.shape[1]
        hg = (jnp.dot(hi, wgb[slot, :half], preferred_element_type=F32)
              + jnp.dot(lo, wgb[slot, half:], preferred_element_type=F32))
        hu = (jnp.dot(hi, wub[slot, :half], preferred_element_type=F32)
              + jnp.dot(lo, wub[slot, half:], preferred_element_type=F32))
        hid = (hg * _sigmoid(hg) * hu).astype(BF16)
        ys_ref[...] = _pack_halves(jnp.dot(hid, wdb[slot], preferred_element_type=F32))

    @pl.when(jnp.logical_not(active))
    def _():
        ys_ref[...] = jnp.zeros(ys_ref.shape, ys_ref.dtype)


def _experts(xs, te, nxt, na, wg, wu, wd, layer, *, name):
    d, f = wg.shape[2:]
    w = d // 2
    ch = _slot_rows(d)
    n_slots = xs.shape[0] // ch
    tm = EXPERT_TILE
    nt = n_slots // tm
    assert d % 2 == 0 and f % 2 == 0 and xs.shape[1] == LANES

    def tile(i, te_ref, nxt_ref, na_ref):
        return (jnp.minimum(i, na_ref[0] - 1), 0)

    hbm = pl.BlockSpec(memory_space=pl.ANY)
    return pl.pallas_call(
        functools.partial(_experts_kernel, layer=layer),
        grid_spec=pltpu.PrefetchScalarGridSpec(
            num_scalar_prefetch=3, grid=(nt,),
            in_specs=[pl.BlockSpec((tm * ch, LANES), tile), hbm, hbm, hbm],
            out_specs=pl.BlockSpec((tm, w), lambda i, *_: (i, 0)),
            scratch_shapes=[pltpu.VMEM((2, d, f), BF16), pltpu.VMEM((2, d, f), BF16), pltpu.VMEM((2, f, d), BF16),
                            pltpu.VMEM((d // 2, f), F32), pltpu.VMEM((f // 2, d), F32),
                            pltpu.SMEM((3,), I32), pltpu.SemaphoreType.DMA(())]),
        out_shape=jax.ShapeDtypeStruct((n_slots, w), U32),
        compiler_params=_params("arbitrary"),
        name=name,
    )(te, nxt, na, xs, wg, wu, wd)


def _combine_kernel(idx_ref, nidx_ref, x_ref, rw_ref, g_ref, b_ref, ys_hbm, o_ref, ob_ref, buf_a, buf_b, sem, *,
                    tm, alpha):
    i = pl.program_id(0)
    n = pl.num_programs(0)

    def start_rows(ref, buf, slot_sem, r0, rows):
        for s in range(r0, r0 + rows):
            _row_copy(ys_hbm, ref[0, s], buf, s, slot_sem).start()
            _row_copy(ys_hbm, ref[1, s], buf, tm + s, slot_sem).start()

    def wait_tile(buf, slot_sem):
        pltpu.make_async_copy(ys_hbm.at[pl.ds(0, 2 * tm), :], buf, slot_sem).wait()

    @pl.when(i == 0)
    def _():
        def body(s, c):
            _row_copy(ys_hbm, idx_ref[0, s], buf_a, s, sem.at[0]).start()
            _row_copy(ys_hbm, idx_ref[1, s], buf_a, tm + s, sem.at[0]).start()
            return c

        lax.fori_loop(0, tm, body, 0, unroll=DMA_UNROLL)

    for parity, (cur, nxt) in enumerate(((buf_a, buf_b), (buf_b, buf_a))):
        @pl.when(i % 2 == parity)
        def _(cur=cur, nxt=nxt, parity=parity):
            wait_tile(cur, sem.at[parity])
            _combine_rows(cur, nidx_ref, nxt, sem.at[1 - parity], start_rows, x_ref, rw_ref, g_ref, b_ref,
                          o_ref, ob_ref, tm=tm, alpha=alpha)

            @pl.when(i == n - 1)
            def _():
                wait_tile(nxt, sem.at[1 - parity])


def _combine_rows(buf, nidx_ref, nxt_buf, nxt_sem, start_rows, x_ref, rw_ref, g_ref, b_ref, o_ref, ob_ref, *,
                  tm, alpha):
    rc = LN_ROWS
    half = buf.shape[-1]
    d = 2 * half

    for r0 in range(0, tm, rc):
        start_rows(nidx_ref, nxt_buf, nxt_sem, r0, rc)
        sl = pl.ds(r0, rc)
        hi1, lo1 = _unpack_halves(buf[sl, :])
        hi2, lo2 = _unpack_halves(buf[pl.ds(tm + r0, rc), :])
        w1 = rw_ref[sl, 0:1]
        w2 = rw_ref[sl, 1:2]
        s_hi = alpha * x_ref[sl, :half] + (w1 * hi1 + w2 * hi2)
        s_lo = alpha * x_ref[sl, half:] + (w1 * lo1 + w2 * lo2)
        mu = (jnp.sum(s_hi, axis=-1, keepdims=True) + jnp.sum(s_lo, axis=-1, keepdims=True)) / d
        c_hi, c_lo = s_hi - mu, s_lo - mu
        var = (jnp.sum(c_hi * c_hi, axis=-1, keepdims=True) + jnp.sum(c_lo * c_lo, axis=-1, keepdims=True)) / d
        inv = lax.rsqrt(var + LN_EPS)
        y_hi = c_hi * inv * g_ref[:, :half] + b_ref[:, :half]
        y_lo = c_lo * inv * g_ref[:, half:] + b_ref[:, half:]
        o_ref[sl, :half] = y_hi
        o_ref[sl, half:] = y_lo
        ob_ref[sl, :half] = y_hi.astype(BF16)
        ob_ref[sl, half:] = y_lo.astype(BF16)


def _combine(x2, rw, pos, ys, g, b, layer, alpha, *, tm, name):
    t, d = x2.shape
    tm = _pick(t, tm)
    nsteps = t // tm
    w = ys.shape[1]
    assert tm % LN_ROWS == 0
    row = lambda width: pl.BlockSpec((tm, width), lambda i: (i, 0))
    vec = pl.BlockSpec((None, 1, d), lambda i: (layer, 0, 0))
    return pl.pallas_call(
        functools.partial(_combine_kernel, tm=tm, alpha=alpha),
        grid=(nsteps,),
        in_specs=[pl.BlockSpec((ROUTE_ROWS, tm), lambda i: (0, i), memory_space=pltpu.SMEM),
                  pl.BlockSpec((ROUTE_ROWS, tm), lambda i: (0, jnp.minimum(i + 1, nsteps - 1)),
                               memory_space=pltpu.SMEM),
                  row(d), row(LANES), vec, vec,
                  pl.BlockSpec(memory_space=pl.ANY)],
        out_specs=[row(d), row(d)],
        out_shape=[jax.ShapeDtypeStruct((t, d), F32), jax.ShapeDtypeStruct((t, d), BF16)],
        scratch_shapes=[pltpu.VMEM((2 * tm, w), U32), pltpu.VMEM((2 * tm, w), U32), pltpu.SemaphoreType.DMA((2,))],
        compiler_params=_params("arbitrary"),
        name=name,
    )(pos, pos, x2, rw, g, b, ys)


def _expert_layout(counts, ridx, n_tiles):
    cnt = counts.reshape(-1).astype(I32)
    padded = (cnt + EXPERT_TILE - 1) // EXPERT_TILE * EXPERT_TILE
    ends = jnp.cumsum(padded)
    off = ends - padded
    tile_ends = ends // EXPERT_TILE
    na = tile_ends[-1:]
    tiles = jnp.minimum(jnp.arange(n_tiles, dtype=I32), na[0] - 1)
    expert_of = lambda tile: jnp.sum((tile[:, None] >= tile_ends[None, :]).astype(I32), axis=1)
    te = expert_of(tiles)
    seg_end = jnp.sum(jnp.where(te[:, None] == jnp.arange(cnt.shape[0], dtype=I32)[None, :], tile_ends[None, :], 0), axis=1)
    nxt = jnp.where(seg_end < na[0], expert_of(seg_end), -1)
    e, rank = ridx[0:2], ridx[2:4]
    base = jnp.sum(jnp.where(e[:, None, :] == jnp.arange(cnt.shape[0], dtype=I32)[None, :, None], off[None, :, None], 0),
                   axis=1)
    pos = jnp.concatenate([base + rank, jnp.zeros((ROUTE_ROWS - 2, ridx.shape[1]), I32)], axis=0)
    return off, cnt, padded, te, nxt, na, pos


def kernel(x, mem, w_in, conv_w, w_conv_proj, w_attn_proj, w_out, rel_bias, ln1_g, ln1_b, w_mem_q, w_mem_kv,
           w_mem_o, ln2_g, ln2_b, w_router, b_router, w_gate, w_up, w_down, ln3_g, ln3_b):
    batch, seq, d = x.shape
    depth = w_in.shape[0]
    t = batch * seq
    alpha = (2 * depth) ** 0.25
    cc = conv_w.shape[-1]
    n_groups = len(ATTN_GROUPS)
    aw = w_attn_proj.shape[1]
    hpg = aw // HEAD_DIM
    n_e = w_router.shape[1]
    assert rel_bias.shape[1] == n_groups * hpg
    qkv_w = 3 * n_groups * aw
    n_tiles = -(-(2 * t) // EXPERT_TILE) + n_e
    n_slots = n_tiles * EXPERT_TILE

    bf = lambda a: a.astype(BF16)
    wc_b, wa_b = bf(w_conv_proj), bf(w_attn_proj)
    wrt_b = bf(w_router.T)
    br = b_router.reshape(1, -1).astype(F32)
    vec3 = lambda a: a.reshape(depth, 1, d)
    ln1_g, ln1_b, ln2_g, ln2_b, ln3_g, ln3_b = map(vec3, (ln1_g, ln1_b, ln2_g, ln2_b, ln3_g, ln3_b))
    tbl = rel_bias.T.astype(F32)
    mem_b = bf(mem).reshape(batch * mem.shape[1], d)

    xf = x.reshape(t, d)
    xb = bf(xf)
    for l in range(depth):
        cin = _matmul(xb, w_in, l, BF16, tm=1024, tn=768, col0=0, ncols=3 * cc, name=f"inproj_conv{l}")
        qkv = _matmul(xb, w_in, l, F32, tm=1024, tn=768, col0=3 * cc, ncols=qkv_w, name=f"inproj_qkv{l}")
        gates = _matmul(xb, w_in, l, BF16, tm=1024, tn=1024, col0=3 * cc + qkv_w, ncols=2 * d, sigmoid=True,
                        name=f"inproj_gate{l}")
        others = []
        for g, (window, r) in enumerate(ATTN_GROUPS):
            last = g == n_groups - 1
            res = _dilated_attention(qkv, tbl[g * hpg:(g + 1) * hpg], g, n_groups, hpg, batch, seq, window, r,
                                     others if last else [], name=f"dattn{l}_{g}")
            if not last:
                others += list(res)
        merged = _merge(cin, gates, res, conv_w, wc_b, wa_b, l, seq, d, tm=1024, tn=512, name=f"merge{l}")
        s1 = _matmul(merged, w_out, l, F32, tm=512, tn=1024, resid=xf, alpha=alpha, name=f"outproj{l}")
        kv = _matmul(mem_b, w_mem_kv, l, BF16, tm=512, tn=1024, name=f"memkv{l}")
        kv = kv.reshape(batch, mem.shape[1], kv.shape[-1])
        x2, x2b, rw, ridx, counts = _xattn(s1, kv, ln1_g, ln1_b, w_mem_q, w_mem_o, ln2_g, ln2_b, wrt_b, br, l, seq, alpha,
                                           tm=256, name=f"xattn{l}")
        off, cnt, padded, te, nxt, na, pos = _expert_layout(counts, ridx, n_tiles)
        xs = _dispatch(x2b, pos, off, cnt, padded, na, n_slots, tm=1024, name=f"dispatch{l}")
        ys = _experts(xs, te, nxt, na, w_gate, w_up, w_down, l, name=f"experts{l}")
        xf, xb = _combine(x2, rw, pos, ys, ln3_g, ln3_b, l, alpha, tm=256, name=f"combine{l}")
    return xf.reshape(batch, seq, d)
```

```python
import functools
import math

import jax
import jax.numpy as jnp
from jax import lax
from jax.experimental import pallas as pl
from jax.experimental.pallas import tpu as pltpu

HEAD_DIM = 128
ATTN_GROUPS = ((128, 1), (512, 4), (2048, 16))
ATTN_BLOCK = 128
N_BUCKETS = 32
MAX_DISTANCE = 2048
MEM_HEAD_DIM = 128
N_EXPERT_GROUPS = 4
LN_EPS = 1e-5
NEG_INF = -1e30

VMEM_LIMIT_BYTES = 56 * 1024 * 1024
LANES = 128
ROUTE_ROWS = 8
EXPERT_TILE = 256
LN_ROWS = 64
MERGE_ROWS = 32
CONV_ROWS = 64
DMA_UNROLL = 8
PACK_ROWS = 64

F32, BF16, U32, I32 = jnp.float32, jnp.bfloat16, jnp.uint32, jnp.int32


def _params(*sem):
    return pltpu.CompilerParams(dimension_semantics=sem, vmem_limit_bytes=VMEM_LIMIT_BYTES)


def _pick(n, pref):
    if n <= pref:
        return n
    t = pref
    while n % t:
        t //= 2
    return t


def _sigmoid(x):
    return 1.0 / (1.0 + jnp.exp(-x))


def _layer_norm(x, g, b):
    mu = jnp.mean(x, axis=-1, keepdims=True)
    xc = x - mu
    var = jnp.mean(xc * xc, axis=-1, keepdims=True)
    return xc * lax.rsqrt(var + LN_EPS) * g + b


def _pack_halves(x):
    half = x.shape[1] // 2
    hi = lax.bitcast_convert_type(x[:, :half].astype(BF16).astype(F32), U32)
    lo = lax.bitcast_convert_type(x[:, half:].astype(BF16).astype(F32), U32)
    return hi | (lo >> 16)


def _unpack_halves(p):
    hi = lax.bitcast_convert_type(p & jnp.uint32(0xFFFF0000), F32)
    lo = lax.bitcast_convert_type(p << 16, F32)
    return hi, lo


def _mm_kernel(*refs, sigmoid, alpha, layer, col0, tn):
    if alpha is None:
        a_ref, w_hbm, o_ref, wst_s, wb_s, sem = refs
    else:
        a_ref, w_hbm, r_ref, o_ref, wst_s, wb_s, sem = refs
    j = pl.program_id(0)

    def w_copy(jj):
        cols = pl.ds(pl.multiple_of(col0 + jj * tn, LANES), tn)
        return pltpu.make_async_copy(w_hbm.at[layer, :, cols], wst_s, sem)

    @pl.when(pl.program_id(1) == 0)
    def _():
        @pl.when(j == 0)
        def _():
            w_copy(0).start()

        w_copy(j).wait()
        wb_s[...] = wst_s[...].astype(BF16)

        @pl.when(j + 1 < pl.num_programs(0))
        def _():
            w_copy(j + 1).start()

    acc = jnp.dot(a_ref[...], wb_s[...], preferred_element_type=F32)
    if alpha is not None:
        acc = acc + alpha * r_ref[...]
    if sigmoid:
        acc = _sigmoid(acc)
    o_ref[...] = acc.astype(o_ref.dtype)


def _matmul(a, w, layer, out_dtype, *, tm, tn, col0=0, ncols=None, sigmoid=False, resid=None, alpha=None, name):
    m, k = a.shape
    n = w.shape[-1] - col0 if ncols is None else ncols
    tm, tn = _pick(m, tm), _pick(n, tn)
    assert col0 % LANES == 0 and tn % LANES == 0
    in_specs = [pl.BlockSpec((tm, k), lambda j, i: (i, 0)),
                pl.BlockSpec(memory_space=pl.ANY)]
    args = [a, w]
    if resid is not None:
        in_specs.append(pl.BlockSpec((tm, tn), lambda j, i: (i, j)))
        args.append(resid)
    return pl.pallas_call(
        functools.partial(_mm_kernel, sigmoid=sigmoid, alpha=alpha, layer=layer, col0=col0, tn=tn),
        grid=(n // tn, m // tm),
        in_specs=in_specs,
        out_specs=pl.BlockSpec((tm, tn), lambda j, i: (i, j)),
        out_shape=jax.ShapeDtypeStruct((m, n), out_dtype),
        scratch_shapes=[pltpu.VMEM((k, tn), F32), pltpu.VMEM((k, tn), BF16), pltpu.SemaphoreType.DMA(())],
        compiler_params=_params("arbitrary", "arbitrary"),
        name=name,
    )(*args)


def _dattn_kernel(bucket_ref, tbl_ref, q_ref, k_ref, v_ref, kp_ref, vp_ref, *rest, r, pt, span, scale, n_other):
    others = rest[:2 * n_other]
    if n_other:
        oatt_ref, bias_s, qs, kcat, vcat, o_ref, lse_ref = rest[2 * n_other:]
    else:
        o_ref, lse_ref, bias_s, qs, kcat, vcat = rest
    blk = ATTN_BLOCK
    n = pl.program_id(1)
    h = pl.program_id(2)
    ch = pt // r
    nb = ch // blk

    @pl.when((pl.program_id(0) == 0) & (n == 0))
    def _():
        bk = bucket_ref[...]
        row = lax.broadcasted_iota(I32, (blk, 2 * blk), 0)
        col = lax.broadcasted_iota(I32, (blk, 2 * blk), 1)
        rel = row + blk - col
        band = (rel >= 0) & (rel <= span)
        bias = jnp.zeros((blk, 2 * blk), F32)
        for b in range(N_BUCKETS):
            bias = jnp.where(bk == b, tbl_ref[h, b], bias)
        bias_s[h, 0] = jnp.where(band, bias, NEG_INF)
        bias_s[h, 1] = jnp.where(band & (col >= blk), bias, NEG_INF)

    def rows(ref, start, size):
        if r == 1:
            return ref[pl.ds(start, size), :]
        return ref[pl.ds(start, size, stride=r), :]

    for rho in range(r):
        qs[...] = rows(q_ref, rho, ch).astype(BF16)
        kcat[0:blk] = rows(kp_ref, rho, blk).astype(BF16)
        kcat[blk:] = rows(k_ref, rho, ch).astype(BF16)
        vcat[0:blk] = rows(vp_ref, rho, blk).astype(BF16)
        vcat[blk:] = rows(v_ref, rho, ch).astype(BF16)
        for b in range(nb):
            first = jnp.where(n == 0, 1, 0) if b == 0 else 0
            q = qs[b * blk:(b + 1) * blk]
            k = kcat[b * blk:(b + 2) * blk]
            v = vcat[b * blk:(b + 2) * blk]
            lg = lax.dot_general(q, k, (((1,), (1,)), ((), ())), preferred_element_type=F32) * scale
            lg = lg + bias_s[h, first]
            m = jnp.max(lg, axis=-1, keepdims=True)
            p = jnp.exp(lg - m)
            l = jnp.sum(p, axis=-1, keepdims=True)
            o = jnp.dot(p.astype(BF16), v, preferred_element_type=F32) / l
            lse = jnp.broadcast_to(m + jnp.log(l), (blk, LANES))
            start = rho + b * blk * r
            if r == 1:
                o_ref[pl.ds(start, blk), :] = o
                lse_ref[pl.ds(start, blk), :] = lse
            else:
                o_ref[pl.ds(start, blk, stride=r), :] = o
                lse_ref[pl.ds(start, blk, stride=r), :] = lse

    if n_other:
        rc = MERGE_ROWS

        def merge_rows(c, carry):
            sl = pl.ds(pl.multiple_of(c * rc, rc), rc)
            ls = [others[2 * g + 1][sl, :] for g in range(n_other)] + [lse_ref[sl, :]]
            os_ = [others[2 * g][sl, :] for g in range(n_other)] + [o_ref[sl, :]]
            mx = functools.reduce(jnp.maximum, ls)
            es = [jnp.exp(x - mx) for x in ls]
            den = functools.reduce(lambda a, b: a + b, es)
            acc = es[0] * os_[0]
            for e, o in zip(es[1:], os_[1:]):
                acc = acc + e * o
            oatt_ref[sl, :] = (acc / den).astype(oatt_ref.dtype)
            return carry

        lax.fori_loop(0, pt // rc, merge_rows, 0, unroll=4)


def _t5_bucket(dist):
    max_exact = N_BUCKETS // 2
    d = jnp.maximum(dist, 0)
    log_part = jnp.log(jnp.maximum(d, 1).astype(F32) / max_exact) / math.log(MAX_DISTANCE / max_exact)
    large = jnp.minimum(max_exact + (log_part * (N_BUCKETS - max_exact)).astype(I32), N_BUCKETS - 1)
    return jnp.where(d < max_exact, d, large)


def _dilated_attention(qkv, tbl, g, n_groups, hpg, batch, seq, window, r, others, name):
    blk = ATTN_BLOCK
    t = qkv.shape[0]
    pt = blk * ATTN_GROUPS[-1][1]
    assert seq % pt == 0 and pt % (blk * r) == 0 and HEAD_DIM == LANES
    tiles = seq // pt
    span = window // r
    qi = jnp.arange(blk)[:, None] + blk
    ki = jnp.arange(2 * blk)[None, :]
    bucket = _t5_bucket((qi - ki) * r).astype(I32)
    heads = n_groups * hpg
    prev_rows = blk * r
    ppt = pt // prev_rows

    def cur(which):
        return pl.BlockSpec((pt, LANES), lambda b, n, h: (b * tiles + n, which * heads + g * hpg + h))

    def prev(which):
        return pl.BlockSpec((prev_rows, LANES),
                            lambda b, n, h: (jnp.maximum((b * tiles + n) * ppt - 1, 0), which * heads + g * hpg + h))

    out = pl.BlockSpec((pt, LANES), lambda b, n, h: (b * tiles + n, h))
    scratch = [pltpu.VMEM((hpg, 2, blk, 2 * blk), F32),
               pltpu.VMEM((pt // r, LANES), BF16),
               pltpu.VMEM((pt // r + blk, LANES), BF16),
               pltpu.VMEM((pt // r + blk, LANES), BF16)]
    if others:
        assert pt % MERGE_ROWS == 0
        out_specs, out_shape = out, jax.ShapeDtypeStruct((t, hpg * LANES), BF16)
        scratch += [pltpu.VMEM((pt, LANES), F32)] * 2
    else:
        out_specs, out_shape = [out, out], [jax.ShapeDtypeStruct((t, hpg * LANES), F32)] * 2
    return pl.pallas_call(
        functools.partial(_dattn_kernel, r=r, pt=pt, span=span, scale=HEAD_DIM ** -0.5, n_other=len(others) // 2),
        grid=(batch, tiles, hpg),
        in_specs=[pl.BlockSpec((blk, 2 * blk), lambda b, n, h: (0, 0)),
                  pl.BlockSpec(memory_space=pltpu.SMEM),
                  cur(0), cur(1), cur(2), prev(1), prev(2)] + [out] * len(others),
        out_specs=out_specs,
        out_shape=out_shape,
        scratch_shapes=scratch,
        compiler_params=_params("arbitrary", "arbitrary", "arbitrary"),
        name=name,
    )(bucket, tbl, qkv, qkv, qkv, qkv, qkv, *others)


def _merge_kernel(b_ref, c_ref, h_ref, cp_ref, hp_ref, cw_ref, oatt_ref, sgc_ref, sga_ref, wc_ref, wa_ref, out_ref,
                  u_s, aconv_s, *, conv_k, tm, halo, seq_tiles):
    i = pl.program_id(0)

    @pl.when(pl.program_id(1) == 0)
    def _():
        seq_start = (i % seq_tiles) == 0
        up = cp_ref[...].astype(F32) * hp_ref[...].astype(F32)
        u_s[0:halo] = jnp.where(seq_start, 0.0, up)
        rc = CONV_ROWS
        for r0 in range(0, tm, rc):
            u_s[halo + r0:halo + r0 + rc] = c_ref[r0:r0 + rc].astype(F32) * h_ref[r0:r0 + rc].astype(F32)
        for r0 in range(0, tm, rc):
            lo = halo - (conv_k - 1) + r0
            conv = cw_ref[0:1, :] * u_s[lo:lo + rc]
            for j in range(1, conv_k):
                conv = conv + cw_ref[j:j + 1, :] * u_s[lo + j:lo + j + rc]
            aconv_s[r0:r0 + rc] = (b_ref[r0:r0 + rc].astype(F32) * conv).astype(BF16)

    yc = jnp.dot(aconv_s[...], wc_ref[...], preferred_element_type=F32)
    ya = jnp.dot(oatt_ref[...], wa_ref[...], preferred_element_type=F32)
    out_ref[...] = (sgc_ref[...].astype(F32) * yc + sga_ref[...].astype(F32) * ya).astype(out_ref.dtype)


def _merge(cin, gates, oatt, conv_w, wc, wa, layer, seq, d_model, *, tm, tn, name):
    t = cin.shape[0]
    conv_k, cc = conv_w.shape[-2:]
    aw = oatt.shape[-1]
    tm, tn = _pick(seq, tm), _pick(d_model, tn)
    halo = 16
    assert cc % LANES == 0 and tm % halo == 0 and conv_k - 1 <= halo and tm % CONV_ROWS == 0
    nj = d_model // tn

    def colblk(c):
        return pl.BlockSpec((tm, cc), lambda i, j: (i, c))

    def halo_blk(c):
        return pl.BlockSpec((halo, cc), lambda i, j: (jnp.maximum(i * (tm // halo) - 1, 0), c))

    in_specs = [colblk(0), colblk(1), colblk(2), halo_blk(1), halo_blk(2),
                pl.BlockSpec((None, conv_k, cc), lambda i, j: (layer, 0, 0)),
                pl.BlockSpec((tm, aw), lambda i, j: (i, 0)),
                pl.BlockSpec((tm, tn), lambda i, j: (i, j)),
                pl.BlockSpec((tm, tn), lambda i, j: (i, nj + j)),
                pl.BlockSpec((None, cc, tn), lambda i, j: (layer, 0, j)),
                pl.BlockSpec((None, aw, tn), lambda i, j: (layer, 0, j))]
    return pl.pallas_call(
        functools.partial(_merge_kernel, conv_k=conv_k, tm=tm, halo=halo, seq_tiles=seq // tm),
        grid=(t // tm, nj),
        in_specs=in_specs,
        out_specs=pl.BlockSpec((tm, tn), lambda i, j: (i, j)),
        out_shape=jax.ShapeDtypeStruct((t, d_model), BF16),
        scratch_shapes=[pltpu.VMEM((tm + halo, cc), F32), pltpu.VMEM((tm, cc), BF16)],
        compiler_params=_params("parallel", "arbitrary"),
        name=name,
    )(cin, cin, cin, cin, cin, conv_w, oatt, gates, gates, wc, wa)


def _route(p):
    epg = len(p) // N_EXPERT_GROUPS
    assert epg == 4
    scores = []
    for g in range(N_EXPERT_GROUPS):
        a, b, c, d = p[g * epg:(g + 1) * epg]
        hi1, lo1, hi2, lo2 = jnp.maximum(a, b), jnp.minimum(a, b), jnp.maximum(c, d), jnp.minimum(c, d)
        scores.append(jnp.maximum(hi1, hi2) + jnp.maximum(jnp.minimum(hi1, hi2), jnp.maximum(lo1, lo2)))
    best, gidx = scores[0], jnp.zeros(scores[0].shape, I32)
    for g in range(1, N_EXPERT_GROUPS):
        take = scores[g] > best
        best = jnp.where(take, scores[g], best)
        gidx = jnp.where(take, g, gidx)
    sel = []
    for k in range(epg):
        s = p[k]
        for g in range(1, N_EXPERT_GROUPS):
            s = jnp.where(gidx == g, p[g * epg + k], s)
        sel.append(s)
    v1, i1 = sel[0], jnp.zeros(sel[0].shape, I32)
    for k in range(1, epg):
        take = sel[k] > v1
        v1 = jnp.where(take, sel[k], v1)
        i1 = jnp.where(take, k, i1)
    v2, i2 = jnp.full(v1.shape, -1.0, F32), jnp.zeros(v1.shape, I32)
    for k in range(epg):
        take = (i1 != k) & (sel[k] > v2)
        v2 = jnp.where(take, sel[k], v2)
        i2 = jnp.where(take, k, i2)
    tot = v1 + v2
    return gidx * epg + i1, gidx * epg + i2, v1 / tot, v2 / tot


def _xattn_kernel(s1_ref, g1_ref, b1_ref, wq_hbm, kv_ref, wo_hbm, g2_ref, b2_ref, wrt_ref, br_ref,
                  x_ref, xp_ref, rw_ref, ridx_ref, cnt_ref, tril_s, carry_s, wq_ref, wo_ref, stq_s, sto_s, wsem, *,
                  heads, alpha, scale, tm, layer):
    i = pl.program_id(0)

    @pl.when(i == 0)
    def _():
        for src, dst, stage in ((wq_hbm, wq_ref, stq_s), (wo_hbm, wo_ref, sto_s)):
            rows = stage.shape[0]
            for r0 in range(0, dst.shape[0], rows):
                cp = pltpu.make_async_copy(src.at[layer, pl.ds(r0, rows), :], stage, wsem)
                cp.start()
                cp.wait()
                dst[pl.ds(r0, rows), :] = stage[...].astype(BF16)

    x1 = _layer_norm(s1_ref[...], g1_ref[...], b1_ref[...])
    q = jnp.dot(x1.astype(BF16), wq_ref[...], preferred_element_type=F32).astype(BF16)
    mw = heads * MEM_HEAD_DIM
    outs = []
    for h in range(heads):
        cs = slice(h * MEM_HEAD_DIM, (h + 1) * MEM_HEAD_DIM)
        k = kv_ref[:, cs]
        v = kv_ref[:, mw + h * MEM_HEAD_DIM:mw + (h + 1) * MEM_HEAD_DIM]
        lg = lax.dot_general(q[:, cs], k, (((1,), (1,)), ((), ())), preferred_element_type=F32) * scale
        m = jnp.max(lg, axis=-1, keepdims=True)
        p = jnp.exp(lg - m)
        l = jnp.sum(p, axis=-1, keepdims=True)
        outs.append((jnp.dot(p.astype(BF16), v, preferred_element_type=F32) / l).astype(BF16))
    o = jnp.concatenate(outs, axis=-1)
    xa = jnp.dot(o, wo_ref[...], preferred_element_type=F32)
    x2 = _layer_norm(alpha * x1 + xa, g2_ref[...], b2_ref[...])
    x_ref[...] = x2
    half = x2.shape[1] // 2
    x2_hi, x2_lo = x2[:, :half].astype(BF16), x2[:, half:].astype(BF16)
    xp_ref[...] = (lax.bitcast_convert_type(x2_hi.astype(F32), U32)
                   | (lax.bitcast_convert_type(x2_lo.astype(F32), U32) >> 16))

    nt = (((1,), (1,)), ((), ()))
    lg = (lax.dot_general(x2_hi, wrt_ref[:, :half], nt, preferred_element_type=F32)
          + lax.dot_general(x2_lo, wrt_ref[:, half:], nt, preferred_element_type=F32)
          + br_ref[...])
    m = jnp.max(lg, axis=-1, keepdims=True)
    ex = jnp.exp(lg - m)
    probs = ex / jnp.sum(ex, axis=-1, keepdims=True)
    n_e = probs.shape[-1]
    e1, e2, w1, w2 = _route([probs[:, e:e + 1] for e in range(n_e)])
    lane = lax.broadcasted_iota(I32, (tm, LANES), 1)
    rw_ref[...] = jnp.where(lane == 0, w1, jnp.where(lane == 1, w2, 0.0))

    @pl.when(i == 0)
    def _():
        rr = lax.broadcasted_iota(I32, (tm, tm), 0)
        cc = lax.broadcasted_iota(I32, (tm, tm), 1)
        tril_s[...] = jnp.where(cc <= rr, 1.0, 0.0).astype(BF16)
        carry_s[...] = jnp.zeros(carry_s.shape, F32)

    lane_e = lax.broadcasted_iota(I32, (tm, n_e), 1)
    onehot = jnp.where((lane_e == e1) | (lane_e == e2), 1.0, 0.0)
    incl = jnp.dot(tril_s[...], onehot.astype(BF16), preferred_element_type=F32)
    rank = incl - onehot + carry_s[...]
    r1 = jnp.sum(jnp.where(lane_e == e1, rank, 0.0), axis=-1, keepdims=True).astype(I32)
    r2 = jnp.sum(jnp.where(lane_e == e2, rank, 0.0), axis=-1, keepdims=True).astype(I32)
    carry_s[...] = carry_s[...] + incl[tm - 1:tm, :]
    cnt_ref[...] = carry_s[...]

    lane8 = lax.broadcasted_iota(I32, (tm, ROUTE_ROWS), 1)
    cols = [e1, e2, r1 >> 8, r1 & 255, r2 >> 8, r2 & 255]
    tok = jnp.zeros((tm, ROUTE_ROWS), F32)
    for c, val in enumerate(cols):
        tok = jnp.where(lane8 == c, val.astype(F32), tok)
    ro = lax.broadcasted_iota(I32, (ROUTE_ROWS, ROUTE_ROWS), 0)
    co = lax.broadcasted_iota(I32, (ROUTE_ROWS, ROUTE_ROWS), 1)
    mix = jnp.where(((ro == 0) & (co == 0)) | ((ro == 1) & (co == 1)) | ((ro == 2) & (co == 3)) | ((ro == 3) & (co == 5)),
                    1.0, jnp.where(((ro == 2) & (co == 2)) | ((ro == 3) & (co == 4)), 256.0, 0.0))
    ridx_ref[...] = lax.dot_general(mix.astype(BF16), tok.astype(BF16), nt, preferred_element_type=F32).astype(I32)


def _xattn(s1, kv, g1, b1, wq, wo, g2, b2, wrt, br, layer, seq, alpha, *, tm, name):
    t, d = s1.shape
    tm = _pick(seq, tm)
    n_mem, kvw = kv.shape[1:]
    mw = wq.shape[-1]
    n_e = wrt.shape[0]
    heads = mw // MEM_HEAD_DIM
    assert 2 * t + n_e * EXPERT_TILE < 65536
    assert tm % LN_ROWS == 0
    vec = pl.BlockSpec((None, 1, d), lambda i: (layer, 0, 0))
    hbm = pl.BlockSpec(memory_space=pl.ANY)
    row = lambda w: pl.BlockSpec((tm, w), lambda i: (i, 0))
    pieces = 4
    assert d % pieces == 0 and mw % pieces == 0
    return pl.pallas_call(
        functools.partial(_xattn_kernel, heads=heads, alpha=alpha, scale=MEM_HEAD_DIM ** -0.5, tm=tm, layer=layer),
        grid=(t // tm,),
        in_specs=[row(d), vec, vec, hbm,
                  pl.BlockSpec((None, n_mem, kvw), lambda i: (i // (seq // tm), 0, 0)),
                  hbm, vec, vec,
                  pl.BlockSpec((n_e, d), lambda i: (0, 0)),
                  pl.BlockSpec((1, n_e), lambda i: (0, 0))],
        out_specs=[row(d), row(d // 2), row(LANES),
                   pl.BlockSpec((ROUTE_ROWS, tm), lambda i: (0, i)),
                   pl.BlockSpec((1, n_e), lambda i: (0, 0))],
        out_shape=[jax.ShapeDtypeStruct((t, d), F32), jax.ShapeDtypeStruct((t, d // 2), U32),
                   jax.ShapeDtypeStruct((t, LANES), F32), jax.ShapeDtypeStruct((ROUTE_ROWS, t), I32),
                   jax.ShapeDtypeStruct((1, n_e), F32)],
        scratch_shapes=[pltpu.VMEM((tm, tm), BF16), pltpu.VMEM((1, n_e), F32),
                        pltpu.VMEM((d, mw), BF16), pltpu.VMEM((mw, d), BF16),
                        pltpu.VMEM((d // pieces, mw), F32), pltpu.VMEM((mw // pieces, d), F32),
                        pltpu.SemaphoreType.DMA(())],
        compiler_params=_params("arbitrary"),
        name=name,
    )(s1, g1, b1, wq, kv, wo, g2, b2, wrt, br)


def _row_copy(src, s, dst, p, sem):
    return pltpu.make_async_copy(src.at[pl.ds(s, 1), :], dst.at[pl.ds(p, 1), :], sem)


def _invert_kernel(off_ref, cnt_ref, pad_ref, na_ref, pos_ref, tok_ref, *, tm, n_e, n_slots):
    i = pl.program_id(0)

    @pl.when(i == 0)
    def _():
        def zero(lo, hi):
            def body(s, c):
                tok_ref[s] = 0
                return c

            lax.fori_loop(lo, hi, body, 0)

        for e in range(n_e):
            zero(off_ref[e] + cnt_ref[e], off_ref[e] + pad_ref[e])
        zero(na_ref[0] * EXPERT_TILE, n_slots)

    base = i * tm
    for s in range(tm):
        tok_ref[pos_ref[0, s]] = base + s
        tok_ref[pos_ref[1, s]] = base + s


def _invert(pos, off, cnt, padded, na, n_slots, *, tm, name):
    t = pos.shape[1]
    tm = _pick(t, tm)
    return pl.pallas_call(
        functools.partial(_invert_kernel, tm=tm, n_e=off.shape[0], n_slots=n_slots),
        grid_spec=pltpu.PrefetchScalarGridSpec(
            num_scalar_prefetch=4, grid=(t // tm,),
            in_specs=[pl.BlockSpec((ROUTE_ROWS, tm), lambda i, *_: (0, i), memory_space=pltpu.SMEM)],
            out_specs=pl.BlockSpec(memory_space=pltpu.SMEM)),
        out_shape=jax.ShapeDtypeStruct((n_slots,), I32),
        compiler_params=_params("arbitrary"),
        name=name,
    )(off, cnt, padded, na, pos)


N_WEIGHT_PIECES = 6


def _experts_kernel(te_ref, nxt_ref, na_ref, tok_ref, ntok_ref, xp_hbm, wg_hbm, wu_hbm, wd_hbm, ys_ref,
                    wgb, wub, wdb, st_in, st_out, xs_a, xs_b, state, sem, gsem, *, layer):
    i = pl.program_id(0)
    hd, hf = st_in.shape[0], st_out.shape[0]

    def piece_copy(e, k):
        if k < 4:
            src = (wg_hbm if k < 2 else wu_hbm).at[layer, e, pl.ds((k % 2) * hd, hd), :]
            return pltpu.make_async_copy(src, st_in, sem)
        return pltpu.make_async_copy(wd_hbm.at[layer, e, pl.ds((k - 4) * hf, hf), :], st_out, sem)

    def for_piece(k, fn):
        for kk in range(N_WEIGHT_PIECES):
            pl.when(k == kk)(functools.partial(fn, kk))

    def start_piece(e, k):
        for_piece(k, lambda kk: piece_copy(e, kk).start())

    def finish_piece(e, k, slot):
        def fn(kk):
            piece_copy(e, kk).wait()
            if kk < 4:
                dst = wgb if kk < 2 else wub
                dst[slot, pl.ds((kk % 2) * hd, hd), :] = st_in[...].astype(BF16)
            else:
                wdb[slot, pl.ds((kk - 4) * hf, hf), :] = st_out[...].astype(BF16)

        for_piece(k, fn)

    def advance(e, slot):
        k = state[1]
        finish_piece(e, k, slot)
        state[1] = k + 1

        @pl.when(k + 1 < N_WEIGHT_PIECES)
        def _():
            start_piece(e, k + 1)

    def finish_all(e, slot):
        lax.fori_loop(state[1], N_WEIGHT_PIECES, lambda _, c: (advance(e, slot), c)[1], 0)

    def begin(e):
        state[1] = 0
        state[2] = e

        @pl.when(e >= 0)
        def _():
            start_piece(e, 0)

    active = i < na_ref[0]

    @pl.when(i == 0)
    def _():
        state[0] = 1
        begin(te_ref[0])

    switch = active & ((i == 0) | (te_ref[i] != te_ref[jnp.maximum(i - 1, 0)]))

    @pl.when(switch)
    def _():
        slot = 1 - state[0]
        finish_all(te_ref[i], slot)
        state[0] = slot
        begin(nxt_ref[i])

    @pl.when(active & jnp.logical_not(switch) & (state[2] >= 0) & (state[1] < N_WEIGHT_PIECES))
    def _():
        advance(state[2], 1 - state[0])

    tm = ys_ref.shape[0]

    def gather_wait(buf, k):
        pltpu.make_async_copy(xp_hbm.at[pl.ds(0, tm), :], buf, gsem.at[k]).wait()

    @pl.when(i == 0)
    def _():
        def body(s, c):
            _row_copy(xp_hbm, tok_ref[0, s], xs_a, s, gsem.at[0]).start()
            return c

        lax.fori_loop(0, tm, body, 0, unroll=DMA_UNROLL)

    def compute(xs_buf):
        slot = state[0]
        hi, lo = _unpack_halves(xs_buf[...])
        hi, lo = hi.astype(BF16), lo.astype(BF16)
        half = hi.shape[1]
        hg = (jnp.dot(hi, wgb[slot, :half], preferred_element_type=F32)
              + jnp.dot(lo, wgb[slot, half:], preferred_element_type=F32))
        hu = (jnp.dot(hi, wub[slot, :half], preferred_element_type=F32)
              + jnp.dot(lo, wub[slot, half:], preferred_element_type=F32))
        hid = (hg * _sigmoid(hg) * hu).astype(BF16)
        ys_ref[...] = _pack_halves(jnp.dot(hid, wdb[slot], preferred_element_type=F32))

    for parity, (cur, other) in enumerate(((xs_a, xs_b), (xs_b, xs_a))):
        @pl.when(active & (i % 2 == parity))
        def _(cur=cur, other=other, parity=parity):
            gather_wait(cur, parity)
            for s in range(tm):
                _row_copy(xp_hbm, ntok_ref[0, s], other, s, gsem.at[1 - parity]).start()
            compute(cur)

            @pl.when(i == na_ref[0] - 1)
            def _():
                gather_wait(other, 1 - parity)

    @pl.when(jnp.logical_not(active))
    def _():
        ys_ref[...] = jnp.zeros(ys_ref.shape, ys_ref.dtype)


def _experts(xp, tok, te, nxt, na, wg, wu, wd, layer, *, name):
    d, f = wg.shape[2:]
    w = d // 2
    tm = EXPERT_TILE
    nt = tok.shape[0]
    assert d % 2 == 0 and f % 2 == 0 and xp.shape[1] == w and tok.shape[2] == tm

    def tile(i, te_ref, nxt_ref, na_ref):
        return (jnp.minimum(i, na_ref[0] - 1), 0, 0)

    def next_tile(i, te_ref, nxt_ref, na_ref):
        return (jnp.minimum(i + 1, na_ref[0] - 1), 0, 0)

    hbm = pl.BlockSpec(memory_space=pl.ANY)
    return pl.pallas_call(
        functools.partial(_experts_kernel, layer=layer),
        grid_spec=pltpu.PrefetchScalarGridSpec(
            num_scalar_prefetch=3, grid=(nt,),
            in_specs=[pl.BlockSpec((None, 1, tm), tile, memory_space=pltpu.SMEM),
                      pl.BlockSpec((None, 1, tm), next_tile, memory_space=pltpu.SMEM),
                      hbm, hbm, hbm, hbm],
            out_specs=pl.BlockSpec((tm, w), lambda i, *_: (i, 0)),
            scratch_shapes=[pltpu.VMEM((2, d, f), BF16), pltpu.VMEM((2, d, f), BF16), pltpu.VMEM((2, f, d), BF16),
                            pltpu.VMEM((d // 2, f), F32), pltpu.VMEM((f // 2, d), F32),
                            pltpu.VMEM((tm, w), U32), pltpu.VMEM((tm, w), U32),
                            pltpu.SMEM((3,), I32), pltpu.SemaphoreType.DMA(()), pltpu.SemaphoreType.DMA((2,))]),
        out_shape=jax.ShapeDtypeStruct((nt * tm, w), U32),
        compiler_params=_params("arbitrary"),
        name=name,
    )(te, nxt, na, tok, tok, xp, wg, wu, wd)


def _combine_kernel(idx_ref, nidx_ref, x_ref, rw_ref, g_ref, b_ref, ys_hbm, o_ref, ob_ref, buf_a, buf_b, sem, *,
                    tm, alpha):
    i = pl.program_id(0)
    n = pl.num_programs(0)

    def start_rows(ref, buf, slot_sem, r0, rows):
        for s in range(r0, r0 + rows):
            _row_copy(ys_hbm, ref[0, s], buf, s, slot_sem).start()
            _row_copy(ys_hbm, ref[1, s], buf, tm + s, slot_sem).start()

    def wait_tile(buf, slot_sem):
        pltpu.make_async_copy(ys_hbm.at[pl.ds(0, 2 * tm), :], buf, slot_sem).wait()

    @pl.when(i == 0)
    def _():
        def body(s, c):
            _row_copy(ys_hbm, idx_ref[0, s], buf_a, s, sem.at[0]).start()
            _row_copy(ys_hbm, idx_ref[1, s], buf_a, tm + s, sem.at[0]).start()
            return c

        lax.fori_loop(0, tm, body, 0, unroll=DMA_UNROLL)

    for parity, (cur, nxt) in enumerate(((buf_a, buf_b), (buf_b, buf_a))):
        @pl.when(i % 2 == parity)
        def _(cur=cur, nxt=nxt, parity=parity):
            wait_tile(cur, sem.at[parity])
            _combine_rows(cur, nidx_ref, nxt, sem.at[1 - parity], start_rows, x_ref, rw_ref, g_ref, b_ref,
                          o_ref, ob_ref, tm=tm, alpha=alpha)

            @pl.when(i == n - 1)
            def _():
                wait_tile(nxt, sem.at[1 - parity])


def _combine_rows(buf, nidx_ref, nxt_buf, nxt_sem, start_rows, x_ref, rw_ref, g_ref, b_ref, o_ref, ob_ref, *,
                  tm, alpha):
    rc = LN_ROWS
    half = buf.shape[-1]
    d = 2 * half

    for r0 in range(0, tm, rc):
        start_rows(nidx_ref, nxt_buf, nxt_sem, r0, rc)
        sl = pl.ds(r0, rc)
        hi1, lo1 = _unpack_halves(buf[sl, :])
        hi2, lo2 = _unpack_halves(buf[pl.ds(tm + r0, rc), :])
        w1 = rw_ref[sl, 0:1]
        w2 = rw_ref[sl, 1:2]
        s_hi = alpha * x_ref[sl, :half] + (w1 * hi1 + w2 * hi2)
        s_lo = alpha * x_ref[sl, half:] + (w1 * lo1 + w2 * lo2)
        mu = (jnp.sum(s_hi, axis=-1, keepdims=True) + jnp.sum(s_lo, axis=-1, keepdims=True)) / d
        c_hi, c_lo = s_hi - mu, s_lo - mu
        var = (jnp.sum(c_hi * c_hi, axis=-1, keepdims=True) + jnp.sum(c_lo * c_lo, axis=-1, keepdims=True)) / d
        inv = lax.rsqrt(var + LN_EPS)
        y_hi = c_hi * inv * g_ref[:, :half] + b_ref[:, :half]
        y_lo = c_lo * inv * g_ref[:, half:] + b_ref[:, half:]
        o_ref[sl, :half] = y_hi
        o_ref[sl, half:] = y_lo
        ob_ref[sl, :half] = y_hi.astype(BF16)
        ob_ref[sl, half:] = y_lo.astype(BF16)


def _combine(x2, rw, pos, ys, g, b, layer, alpha, *, tm, name):
    t, d = x2.shape
    tm = _pick(t, tm)
    nsteps = t // tm
    w = ys.shape[1]
    assert tm % LN_ROWS == 0
    row = lambda width: pl.BlockSpec((tm, width), lambda i: (i, 0))
    vec = pl.BlockSpec((None, 1, d), lambda i: (layer, 0, 0))
    return pl.pallas_call(
        functools.partial(_combine_kernel, tm=tm, alpha=alpha),
        grid=(nsteps,),
        in_specs=[pl.BlockSpec((ROUTE_ROWS, tm), lambda i: (0, i), memory_space=pltpu.SMEM),
                  pl.BlockSpec((ROUTE_ROWS, tm), lambda i: (0, jnp.minimum(i + 1, nsteps - 1)),
                               memory_space=pltpu.SMEM),
                  row(d), row(LANES), vec, vec,
                  pl.BlockSpec(memory_space=pl.ANY)],
        out_specs=[row(d), row(d)],
        out_shape=[jax.ShapeDtypeStruct((t, d), F32), jax.ShapeDtypeStruct((t, d), BF16)],
        scratch_shapes=[pltpu.VMEM((2 * tm, w), U32), pltpu.VMEM((2 * tm, w), U32), pltpu.SemaphoreType.DMA((2,))],
        compiler_params=_params("arbitrary"),
        name=name,
    )(pos, pos, x2, rw, g, b, ys)


def _expert_layout(counts, ridx, n_tiles):
    cnt = counts.reshape(-1).astype(I32)
    padded = (cnt + EXPERT_TILE - 1) // EXPERT_TILE * EXPERT_TILE
    ends = jnp.cumsum(padded)
    off = ends - padded
    tile_ends = ends // EXPERT_TILE
    na = tile_ends[-1:]
    tiles = jnp.minimum(jnp.arange(n_tiles, dtype=I32), na[0] - 1)
    expert_of = lambda tile: jnp.sum((tile[:, None] >= tile_ends[None, :]).astype(I32), axis=1)
    te = expert_of(tiles)
    seg_end = jnp.sum(jnp.where(te[:, None] == jnp.arange(cnt.shape[0], dtype=I32)[None, :], tile_ends[None, :], 0), axis=1)
    nxt = jnp.where(seg_end < na[0], expert_of(seg_end), -1)
    e, rank = ridx[0:2], ridx[2:4]
    base = jnp.sum(jnp.where(e[:, None, :] == jnp.arange(cnt.shape[0], dtype=I32)[None, :, None], off[None, :, None], 0),
                   axis=1)
    pos = jnp.concatenate([base + rank, jnp.zeros((ROUTE_ROWS - 2, ridx.shape[1]), I32)], axis=0)
    return off, cnt, padded, te, nxt, na, pos


def kernel(x, mem, w_in, conv_w, w_conv_proj, w_attn_proj, w_out, rel_bias, ln1_g, ln1_b, w_mem_q, w_mem_kv,
           w_mem_o, ln2_g, ln2_b, w_router, b_router, w_gate, w_up, w_down, ln3_g, ln3_b):
    batch, seq, d = x.shape
    depth = w_in.shape[0]
    t = batch * seq
    alpha = (2 * depth) ** 0.25
    cc = conv_w.shape[-1]
    n_groups = len(ATTN_GROUPS)
    aw = w_attn_proj.shape[1]
    hpg = aw // HEAD_DIM
    n_e = w_router.shape[1]
    assert rel_bias.shape[1] == n_groups * hpg
    qkv_w = 3 * n_groups * aw
    n_tiles = -(-(2 * t) // EXPERT_TILE) + n_e
    n_slots = n_tiles * EXPERT_TILE

    bf = lambda a: a.astype(BF16)
    wc_b, wa_b = bf(w_conv_proj), bf(w_attn_proj)
    wrt_b = bf(w_router.T)
    br = b_router.reshape(1, -1).astype(F32)
    vec3 = lambda a: a.reshape(depth, 1, d)
    ln1_g, ln1_b, ln2_g, ln2_b, ln3_g, ln3_b = map(vec3, (ln1_g, ln1_b, ln2_g, ln2_b, ln3_g, ln3_b))
    tbl = rel_bias.T.astype(F32)
    mem_b = bf(mem).reshape(batch * mem.shape[1], d)

    xf = x.reshape(t, d)
    xb = bf(xf)
    for l in range(depth):
        cin = _matmul(xb, w_in, l, BF16, tm=1024, tn=768, col0=0, ncols=3 * cc, name=f"inproj_conv{l}")
        qkv = _matmul(xb, w_in, l, F32, tm=1024, tn=768, col0=3 * cc, ncols=qkv_w, name=f"inproj_qkv{l}")
        gates = _matmul(xb, w_in, l, BF16, tm=1024, tn=1024, col0=3 * cc + qkv_w, ncols=2 * d, sigmoid=True,
                        name=f"inproj_gate{l}")
        others = []
        for g, (window, r) in enumerate(ATTN_GROUPS):
            last = g == n_groups - 1
            res = _dilated_attention(qkv, tbl[g * hpg:(g + 1) * hpg], g, n_groups, hpg, batch, seq, window, r,
                                     others if last else [], name=f"dattn{l}_{g}")
            if not last:
                others += list(res)
        merged = _merge(cin, gates, res, conv_w, wc_b, wa_b, l, seq, d, tm=1024, tn=512, name=f"merge{l}")
        s1 = _matmul(merged, w_out, l, F32, tm=512, tn=1024, resid=xf, alpha=alpha, name=f"outproj{l}")
        kv = _matmul(mem_b, w_mem_kv, l, BF16, tm=512, tn=1024, name=f"memkv{l}")
        kv = kv.reshape(batch, mem.shape[1], kv.shape[-1])
        x2, x2p, rw, ridx, counts = _xattn(s1, kv, ln1_g, ln1_b, w_mem_q, w_mem_o, ln2_g, ln2_b, wrt_b, br, l, seq, alpha,
                                           tm=256, name=f"xattn{l}")
        off, cnt, padded, te, nxt, na, pos = _expert_layout(counts, ridx, n_tiles)
        tok = _invert(pos, off, cnt, padded, na, n_slots, tm=1024, name=f"invert{l}")
        ys = _experts(x2p, tok.reshape(n_tiles, 1, EXPERT_TILE), te, nxt, na, w_gate, w_up, w_down, l,
                      name=f"experts{l}")
        xf, xb = _combine(x2, rw, pos, ys, ln3_g, ln3_b, l, alpha, tm=256, name=f"combine{l}")
    return xf.reshape(batch, seq, d)
```

```python
import functools
import math

import jax
import jax.numpy as jnp
from jax import lax
from jax.experimental import pallas as pl
from jax.experimental.pallas import tpu as pltpu

HEAD_DIM = 128
ATTN_GROUPS = ((128, 1), (512, 4), (2048, 16))
ATTN_BLOCK = 128
N_BUCKETS = 32
MAX_DISTANCE = 2048
MEM_HEAD_DIM = 128
N_EXPERT_GROUPS = 4
LN_EPS = 1e-5
NEG_INF = -1e30

VMEM_LIMIT_BYTES = 56 * 1024 * 1024
LANES = 128
ROUTE_ROWS = 8
EXPERT_TILE = 256
LN_ROWS = 64
MERGE_ROWS = 32
CONV_ROWS = 64
DMA_UNROLL = 8
PACK_ROWS = 64
STRIDE_STEP = 4

F32, BF16, U32, I32 = jnp.float32, jnp.bfloat16, jnp.uint32, jnp.int32


def _params(*sem):
    return pltpu.CompilerParams(dimension_semantics=sem, vmem_limit_bytes=VMEM_LIMIT_BYTES)


def _pick(n, pref):
    if n <= pref:
        return n
    t = pref
    while n % t:
        t //= 2
    return t


def _sigmoid(x):
    return 1.0 / (1.0 + jnp.exp(-x))


def _layer_norm(x, g, b):
    mu = jnp.mean(x, axis=-1, keepdims=True)
    xc = x - mu
    var = jnp.mean(xc * xc, axis=-1, keepdims=True)
    return xc * lax.rsqrt(var + LN_EPS) * g + b


def _pack_halves(x):
    half = x.shape[1] // 2
    hi = lax.bitcast_convert_type(x[:, :half].astype(BF16).astype(F32), U32)
    lo = lax.bitcast_convert_type(x[:, half:].astype(BF16).astype(F32), U32)
    return hi | (lo >> 16)


def _unpack_halves(p):
    hi = lax.bitcast_convert_type(p & jnp.uint32(0xFFFF0000), F32)
    lo = lax.bitcast_convert_type(p << 16, F32)
    return hi, lo


def _mm_kernel(*refs, sigmoid, alpha, layer, col0, tn):
    if alpha is None:
        a_ref, w_hbm, o_ref, wst_s, wb_s, sem = refs
    else:
        a_ref, w_hbm, r_ref, o_ref, wst_s, wb_s, sem = refs
    j = pl.program_id(0)

    def w_copy(jj):
        cols = pl.ds(pl.multiple_of(col0 + jj * tn, LANES), tn)
        return pltpu.make_async_copy(w_hbm.at[layer, :, cols], wst_s, sem)

    @pl.when(pl.program_id(1) == 0)
    def _():
        @pl.when(j == 0)
        def _():
            w_copy(0).start()

        w_copy(j).wait()
        wb_s[...] = wst_s[...].astype(BF16)

        @pl.when(j + 1 < pl.num_programs(0))
        def _():
            w_copy(j + 1).start()

    acc = jnp.dot(a_ref[...], wb_s[...], preferred_element_type=F32)
    if alpha is not None:
        acc = acc + alpha * r_ref[...]
    if sigmoid:
        acc = _sigmoid(acc)
    o_ref[...] = acc.astype(o_ref.dtype)


def _matmul(a, w, layer, out_dtype, *, tm, tn, col0=0, ncols=None, sigmoid=False, resid=None, alpha=None, name):
    m, k = a.shape
    n = w.shape[-1] - col0 if ncols is None else ncols
    tm, tn = _pick(m, tm), _pick(n, tn)
    assert col0 % LANES == 0 and tn % LANES == 0
    in_specs = [pl.BlockSpec((tm, k), lambda j, i: (i, 0)),
                pl.BlockSpec(memory_space=pl.ANY)]
    args = [a, w]
    if resid is not None:
        in_specs.append(pl.BlockSpec((tm, tn), lambda j, i: (i, j)))
        args.append(resid)
    return pl.pallas_call(
        functools.partial(_mm_kernel, sigmoid=sigmoid, alpha=alpha, layer=layer, col0=col0, tn=tn),
        grid=(n // tn, m // tm),
        in_specs=in_specs,
        out_specs=pl.BlockSpec((tm, tn), lambda j, i: (i, j)),
        out_shape=jax.ShapeDtypeStruct((m, n), out_dtype),
        scratch_shapes=[pltpu.VMEM((k, tn), F32), pltpu.VMEM((k, tn), BF16), pltpu.SemaphoreType.DMA(())],
        compiler_params=_params("arbitrary", "arbitrary"),
        name=name,
    )(*args)


def _dattn_kernel(bucket_ref, tbl_ref, q_ref, k_ref, v_ref, kp_ref, vp_ref, *rest, r, pt, span, scale, n_other):
    others = rest[:2 * n_other]
    two_level = r > STRIDE_STEP
    if two_level:
        rest, lvl = rest[:-1], rest[-1]
    if n_other:
        oatt_ref, bias_s, qs, kcat, vcat, o_ref, lse_ref = rest[2 * n_other:]
    else:
        o_ref, lse_ref, bias_s, qs, kcat, vcat = rest
    blk = ATTN_BLOCK
    n = pl.program_id(1)
    h = pl.program_id(2)
    ch = pt // r
    nb = ch // blk

    @pl.when((pl.program_id(0) == 0) & (n == 0))
    def _():
        bk = bucket_ref[...]
        row = lax.broadcasted_iota(I32, (blk, 2 * blk), 0)
        col = lax.broadcasted_iota(I32, (blk, 2 * blk), 1)
        rel = row + blk - col
        band = (rel >= 0) & (rel <= span)
        bias = jnp.zeros((blk, 2 * blk), F32)
        for b in range(N_BUCKETS):
            bias = jnp.where(bk == b, tbl_ref[h, b], bias)
        bias_s[h, 0] = jnp.where(band, bias, NEG_INF)
        bias_s[h, 1] = jnp.where(band & (col >= blk), bias, NEG_INF)

    srcs = (q_ref, k_ref, v_ref, kp_ref, vp_ref)
    if two_level:
        for idx, ref in enumerate(srcs):
            n4 = ref.shape[0] // STRIDE_STEP
            for rho1 in range(STRIDE_STEP):
                lvl[idx, rho1, 0:n4] = ref[pl.ds(rho1, n4, stride=STRIDE_STEP), :]

    def rows(ref, start, size):
        if r == 1:
            return ref[pl.ds(start, size), :]
        if two_level:
            idx = [i for i, s in enumerate(srcs) if s is ref][0]
            return lvl[idx, start % STRIDE_STEP, pl.ds(start // STRIDE_STEP, size, stride=r // STRIDE_STEP), :]
        return ref[pl.ds(start, size, stride=r), :]

    for rho in range(r):
        qs[...] = rows(q_ref, rho, ch).astype(BF16)
        kcat[0:blk] = rows(kp_ref, rho, blk).astype(BF16)
        kcat[blk:] = rows(k_ref, rho, ch).astype(BF16)
        vcat[0:blk] = rows(vp_ref, rho, blk).astype(BF16)
        vcat[blk:] = rows(v_ref, rho, ch).astype(BF16)
        for b in range(nb):
            first = jnp.where(n == 0, 1, 0) if b == 0 else 0
            q = qs[b * blk:(b + 1) * blk]
            k = kcat[b * blk:(b + 2) * blk]
            v = vcat[b * blk:(b + 2) * blk]
            lg = lax.dot_general(q, k, (((1,), (1,)), ((), ())), preferred_element_type=F32) * scale
            lg = lg + bias_s[h, first]
            m = jnp.max(lg, axis=-1, keepdims=True)
            p = jnp.exp(lg - m)
            l = jnp.sum(p, axis=-1, keepdims=True)
            o = jnp.dot(p.astype(BF16), v, preferred_element_type=F32) / l
            lse = jnp.broadcast_to(m + jnp.log(l), (blk, LANES))
            start = rho + b * blk * r
            if r == 1:
                o_ref[pl.ds(start, blk), :] = o
                lse_ref[pl.ds(start, blk), :] = lse
            else:
                o_ref[pl.ds(start, blk, stride=r), :] = o
                lse_ref[pl.ds(start, blk, stride=r), :] = lse

    if n_other:
        rc = MERGE_ROWS

        def merge_rows(c, carry):
            sl = pl.ds(pl.multiple_of(c * rc, rc), rc)
            ls = [others[2 * g + 1][sl, :] for g in range(n_other)] + [lse_ref[sl, :]]
            os_ = [others[2 * g][sl, :] for g in range(n_other)] + [o_ref[sl, :]]
            mx = functools.reduce(jnp.maximum, ls)
            es = [jnp.exp(x - mx) for x in ls]
            den = functools.reduce(lambda a, b: a + b, es)
            acc = es[0] * os_[0]
            for e, o in zip(es[1:], os_[1:]):
                acc = acc + e * o
            oatt_ref[sl, :] = (acc / den).astype(oatt_ref.dtype)
            return carry

        lax.fori_loop(0, pt // rc, merge_rows, 0, unroll=4)


def _t5_bucket(dist):
    max_exact = N_BUCKETS // 2
    d = jnp.maximum(dist, 0)
    log_part = jnp.log(jnp.maximum(d, 1).astype(F32) / max_exact) / math.log(MAX_DISTANCE / max_exact)
    large = jnp.minimum(max_exact + (log_part * (N_BUCKETS - max_exact)).astype(I32), N_BUCKETS - 1)
    return jnp.where(d < max_exact, d, large)


def _dilated_attention(qkv, tbl, g, n_groups, hpg, batch, seq, window, r, others, name):
    blk = ATTN_BLOCK
    t = qkv.shape[0]
    pt = blk * ATTN_GROUPS[-1][1]
    assert seq % pt == 0 and pt % (blk * r) == 0 and HEAD_DIM == LANES
    tiles = seq // pt
    span = window // r
    qi = jnp.arange(blk)[:, None] + blk
    ki = jnp.arange(2 * blk)[None, :]
    bucket = _t5_bucket((qi - ki) * r).astype(I32)
    heads = n_groups * hpg
    prev_rows = blk * r
    ppt = pt // prev_rows

    def cur(which):
        return pl.BlockSpec((pt, LANES), lambda b, n, h: (b * tiles + n, which * heads + g * hpg + h))

    def prev(which):
        return pl.BlockSpec((prev_rows, LANES),
                            lambda b, n, h: (jnp.maximum((b * tiles + n) * ppt - 1, 0), which * heads + g * hpg + h))

    out = pl.BlockSpec((pt, LANES), lambda b, n, h: (b * tiles + n, h))
    scratch = [pltpu.VMEM((hpg, 2, blk, 2 * blk), F32),
               pltpu.VMEM((pt // r, LANES), BF16),
               pltpu.VMEM((pt // r + blk, LANES), BF16),
               pltpu.VMEM((pt // r + blk, LANES), BF16)]
    if others:
        assert pt % MERGE_ROWS == 0
        out_specs, out_shape = out, jax.ShapeDtypeStruct((t, hpg * LANES), BF16)
        scratch += [pltpu.VMEM((pt, LANES), F32)] * 2
    else:
        out_specs, out_shape = [out, out], [jax.ShapeDtypeStruct((t, hpg * LANES), F32)] * 2
    if r > STRIDE_STEP:
        assert r % STRIDE_STEP == 0 and prev_rows <= pt
        scratch += [pltpu.VMEM((5, STRIDE_STEP, pt // STRIDE_STEP, LANES), F32)]
    return pl.pallas_call(
        functools.partial(_dattn_kernel, r=r, pt=pt, span=span, scale=HEAD_DIM ** -0.5, n_other=len(others) // 2),
        grid=(batch, tiles, hpg),
        in_specs=[pl.BlockSpec((blk, 2 * blk), lambda b, n, h: (0, 0)),
                  pl.BlockSpec(memory_space=pltpu.SMEM),
                  cur(0), cur(1), cur(2), prev(1), prev(2)] + [out] * len(others),
        out_specs=out_specs,
        out_shape=out_shape,
        scratch_shapes=scratch,
        compiler_params=_params("arbitrary", "arbitrary", "arbitrary"),
        name=name,
    )(bucket, tbl, qkv, qkv, qkv, qkv, qkv, *others)


def _merge_kernel(b_ref, c_ref, h_ref, cp_ref, hp_ref, cw_ref, oatt_ref, sgc_ref, sga_ref, wc_ref, wa_ref, out_ref,
                  u_s, aconv_s, *, conv_k, tm, halo, seq_tiles):
    i = pl.program_id(0)

    @pl.when(pl.program_id(1) == 0)
    def _():
        seq_start = (i % seq_tiles) == 0
        up = cp_ref[...].astype(F32) * hp_ref[...].astype(F32)
        u_s[0:halo] = jnp.where(seq_start, 0.0, up)
        rc = CONV_ROWS
        for r0 in range(0, tm, rc):
            u_s[halo + r0:halo + r0 + rc] = c_ref[r0:r0 + rc].astype(F32) * h_ref[r0:r0 + rc].astype(F32)
        for r0 in range(0, tm, rc):
            lo = halo - (conv_k - 1) + r0
            conv = cw_ref[0:1, :] * u_s[lo:lo + rc]
            for j in range(1, conv_k):
                conv = conv + cw_ref[j:j + 1, :] * u_s[lo + j:lo + j + rc]
            aconv_s[r0:r0 + rc] = (b_ref[r0:r0 + rc].astype(F32) * conv).astype(BF16)

    yc = jnp.dot(aconv_s[...], wc_ref[...], preferred_element_type=F32)
    ya = jnp.dot(oatt_ref[...], wa_ref[...], preferred_element_type=F32)
    out_ref[...] = (sgc_ref[...].astype(F32) * yc + sga_ref[...].astype(F32) * ya).astype(out_ref.dtype)


def _merge(cin, gates, oatt, conv_w, wc, wa, layer, seq, d_model, *, tm, tn, name):
    t = cin.shape[0]
    conv_k, cc = conv_w.shape[-2:]
    aw = oatt.shape[-1]
    tm, tn = _pick(seq, tm), _pick(d_model, tn)
    halo = 16
    assert cc % LANES == 0 and tm % halo == 0 and conv_k - 1 <= halo and tm % CONV_ROWS == 0
    nj = d_model // tn

    def colblk(c):
        return pl.BlockSpec((tm, cc), lambda i, j: (i, c))

    def halo_blk(c):
        return pl.BlockSpec((halo, cc), lambda i, j: (jnp.maximum(i * (tm // halo) - 1, 0), c))

    in_specs = [colblk(0), colblk(1), colblk(2), halo_blk(1), halo_blk(2),
                pl.BlockSpec((None, conv_k, cc), lambda i, j: (layer, 0, 0)),
                pl.BlockSpec((tm, aw), lambda i, j: (i, 0)),
                pl.BlockSpec((tm, tn), lambda i, j: (i, j)),
                pl.BlockSpec((tm, tn), lambda i, j: (i, nj + j)),
                pl.BlockSpec((None, cc, tn), lambda i, j: (layer, 0, j)),
                pl.BlockSpec((None, aw, tn), lambda i, j: (layer, 0, j))]
    return pl.pallas_call(
        functools.partial(_merge_kernel, conv_k=conv_k, tm=tm, halo=halo, seq_tiles=seq // tm),
        grid=(t // tm, nj),
        in_specs=in_specs,
        out_specs=pl.BlockSpec((tm, tn), lambda i, j: (i, j)),
        out_shape=jax.ShapeDtypeStruct((t, d_model), BF16),
        scratch_shapes=[pltpu.VMEM((tm + halo, cc), F32), pltpu.VMEM((tm, cc), BF16)],
        compiler_params=_params("parallel", "arbitrary"),
        name=name,
    )(cin, cin, cin, cin, cin, conv_w, oatt, gates, gates, wc, wa)


def _route(p):
    epg = len(p) // N_EXPERT_GROUPS
    assert epg == 4
    scores = []
    for g in range(N_EXPERT_GROUPS):
        a, b, c, d = p[g * epg:(g + 1) * epg]
        hi1, lo1, hi2, lo2 = jnp.maximum(a, b), jnp.minimum(a, b), jnp.maximum(c, d), jnp.minimum(c, d)
        scores.append(jnp.maximum(hi1, hi2) + jnp.maximum(jnp.minimum(hi1, hi2), jnp.maximum(lo1, lo2)))
    best, gidx = scores[0], jnp.zeros(scores[0].shape, I32)
    for g in range(1, N_EXPERT_GROUPS):
        take = scores[g] > best
        best = jnp.where(take, scores[g], best)
        gidx = jnp.where(take, g, gidx)
    sel = []
    for k in range(epg):
        s = p[k]
        for g in range(1, N_EXPERT_GROUPS):
            s = jnp.where(gidx == g, p[g * epg + k], s)
        sel.append(s)
    v1, i1 = sel[0], jnp.zeros(sel[0].shape, I32)
    for k in range(1, epg):
        take = sel[k] > v1
        v1 = jnp.where(take, sel[k], v1)
        i1 = jnp.where(take, k, i1)
    v2, i2 = jnp.full(v1.shape, -1.0, F32), jnp.zeros(v1.shape, I32)
    for k in range(epg):
        take = (i1 != k) & (sel[k] > v2)
        v2 = jnp.where(take, sel[k], v2)
        i2 = jnp.where(take, k, i2)
    tot = v1 + v2
    return gidx * epg + i1, gidx * epg + i2, v1 / tot, v2 / tot


def _xattn_kernel(s1_ref, g1_ref, b1_ref, wq_hbm, kv_ref, wo_hbm, g2_ref, b2_ref, wrt_ref, br_ref,
                  x_ref, xb_ref, rw_ref, ridx_ref, cnt_ref, tril_s, carry_s, wq_ref, wo_ref, stq_s, sto_s, wsem, *,
                  heads, alpha, scale, tm, layer):
    i = pl.program_id(0)

    @pl.when(i == 0)
    def _():
        for src, dst, stage in ((wq_hbm, wq_ref, stq_s), (wo_hbm, wo_ref, sto_s)):
            rows = stage.shape[0]
            for r0 in range(0, dst.shape[0], rows):
                cp = pltpu.make_async_copy(src.at[layer, pl.ds(r0, rows), :], stage, wsem)
                cp.start()
                cp.wait()
                dst[pl.ds(r0, rows), :] = stage[...].astype(BF16)

    x1 = _layer_norm(s1_ref[...], g1_ref[...], b1_ref[...])
    q = jnp.dot(x1.astype(BF16), wq_ref[...], preferred_element_type=F32).astype(BF16)
    mw = heads * MEM_HEAD_DIM
    outs = []
    for h in range(heads):
        cs = slice(h * MEM_HEAD_DIM, (h + 1) * MEM_HEAD_DIM)
        k = kv_ref[:, cs]
        v = kv_ref[:, mw + h * MEM_HEAD_DIM:mw + (h + 1) * MEM_HEAD_DIM]
        lg = lax.dot_general(q[:, cs], k, (((1,), (1,)), ((), ())), preferred_element_type=F32) * scale
        m = jnp.max(lg, axis=-1, keepdims=True)
        p = jnp.exp(lg - m)
        l = jnp.sum(p, axis=-1, keepdims=True)
        outs.append((jnp.dot(p.astype(BF16), v, preferred_element_type=F32) / l).astype(BF16))
    o = jnp.concatenate(outs, axis=-1)
    xa = jnp.dot(o, wo_ref[...], preferred_element_type=F32)
    x2 = _layer_norm(alpha * x1 + xa, g2_ref[...], b2_ref[...])
    x_ref[...] = x2
    half = x2.shape[1] // 2
    x2_hi, x2_lo = x2[:, :half].astype(BF16), x2[:, half:].astype(BF16)
    xb_ref[:, :half] = x2_hi
    xb_ref[:, half:] = x2_lo

    nt = (((1,), (1,)), ((), ()))
    lg = (lax.dot_general(x2_hi, wrt_ref[:, :half], nt, preferred_element_type=F32)
          + lax.dot_general(x2_lo, wrt_ref[:, half:], nt, preferred_element_type=F32)
          + br_ref[...])
    m = jnp.max(lg, axis=-1, keepdims=True)
    ex = jnp.exp(lg - m)
    probs = ex / jnp.sum(ex, axis=-1, keepdims=True)
    n_e = probs.shape[-1]
    e1, e2, w1, w2 = _route([probs[:, e:e + 1] for e in range(n_e)])
    lane = lax.broadcasted_iota(I32, (tm, LANES), 1)
    rw_ref[...] = jnp.where(lane == 0, w1, jnp.where(lane == 1, w2, 0.0))

    @pl.when(i == 0)
    def _():
        rr = lax.broadcasted_iota(I32, (tm, tm), 0)
        cc = lax.broadcasted_iota(I32, (tm, tm), 1)
        tril_s[...] = jnp.where(cc <= rr, 1.0, 0.0).astype(BF16)
        carry_s[...] = jnp.zeros(carry_s.shape, F32)

    lane_e = lax.broadcasted_iota(I32, (tm, n_e), 1)
    onehot = jnp.where((lane_e == e1) | (lane_e == e2), 1.0, 0.0)
    incl = jnp.dot(tril_s[...], onehot.astype(BF16), preferred_element_type=F32)
    rank = incl - onehot + carry_s[...]
    r1 = jnp.sum(jnp.where(lane_e == e1, rank, 0.0), axis=-1, keepdims=True).astype(I32)
    r2 = jnp.sum(jnp.where(lane_e == e2, rank, 0.0), axis=-1, keepdims=True).astype(I32)
    carry_s[...] = carry_s[...] + incl[tm - 1:tm, :]
    cnt_ref[...] = carry_s[...]

    lane8 = lax.broadcasted_iota(I32, (tm, ROUTE_ROWS), 1)
    cols = [e1, e2, r1 >> 8, r1 & 255, r2 >> 8, r2 & 255]
    tok = jnp.zeros((tm, ROUTE_ROWS), F32)
    for c, val in enumerate(cols):
        tok = jnp.where(lane8 == c, val.astype(F32), tok)
    ro = lax.broadcasted_iota(I32, (ROUTE_ROWS, ROUTE_ROWS), 0)
    co = lax.broadcasted_iota(I32, (ROUTE_ROWS, ROUTE_ROWS), 1)
    mix = jnp.where(((ro == 0) & (co == 0)) | ((ro == 1) & (co == 1)) | ((ro == 2) & (co == 3)) | ((ro == 3) & (co == 5)),
                    1.0, jnp.where(((ro == 2) & (co == 2)) | ((ro == 3) & (co == 4)), 256.0, 0.0))
    ridx_ref[...] = lax.dot_general(mix.astype(BF16), tok.astype(BF16), nt, preferred_element_type=F32).astype(I32)


def _xattn(s1, kv, g1, b1, wq, wo, g2, b2, wrt, br, layer, seq, alpha, *, tm, name):
    t, d = s1.shape
    tm = _pick(seq, tm)
    n_mem, kvw = kv.shape[1:]
    mw = wq.shape[-1]
    n_e = wrt.shape[0]
    heads = mw // MEM_HEAD_DIM
    assert 2 * t + n_e * EXPERT_TILE < 65536
    assert tm % LN_ROWS == 0
    vec = pl.BlockSpec((None, 1, d), lambda i: (layer, 0, 0))
    hbm = pl.BlockSpec(memory_space=pl.ANY)
    row = lambda w: pl.BlockSpec((tm, w), lambda i: (i, 0))
    pieces = 4
    assert d % pieces == 0 and mw % pieces == 0
    return pl.pallas_call(
        functools.partial(_xattn_kernel, heads=heads, alpha=alpha, scale=MEM_HEAD_DIM ** -0.5, tm=tm, layer=layer),
        grid=(t // tm,),
        in_specs=[row(d), vec, vec, hbm,
                  pl.BlockSpec((None, n_mem, kvw), lambda i: (i // (seq // tm), 0, 0)),
                  hbm, vec, vec,
                  pl.BlockSpec((n_e, d), lambda i: (0, 0)),
                  pl.BlockSpec((1, n_e), lambda i: (0, 0))],
        out_specs=[row(d), row(d), row(LANES),
                   pl.BlockSpec((ROUTE_ROWS, tm), lambda i: (0, i)),
                   pl.BlockSpec((1, n_e), lambda i: (0, 0))],
        out_shape=[jax.ShapeDtypeStruct((t, d), F32), jax.ShapeDtypeStruct((t, d), BF16),
                   jax.ShapeDtypeStruct((t, LANES), F32), jax.ShapeDtypeStruct((ROUTE_ROWS, t), I32),
                   jax.ShapeDtypeStruct((1, n_e), F32)],
        scratch_shapes=[pltpu.VMEM((tm, tm), BF16), pltpu.VMEM((1, n_e), F32),
                        pltpu.VMEM((d, mw), BF16), pltpu.VMEM((mw, d), BF16),
                        pltpu.VMEM((d // pieces, mw), F32), pltpu.VMEM((mw // pieces, d), F32),
                        pltpu.SemaphoreType.DMA(())],
        compiler_params=_params("arbitrary"),
        name=name,
    )(s1, g1, b1, wq, kv, wo, g2, b2, wrt, br)


def _row_copy(src, s, dst, p, sem):
    return pltpu.make_async_copy(src.at[pl.ds(s, 1), :], dst.at[pl.ds(p, 1), :], sem)


def _slot_rows(d):
    return d // 2 // LANES + 1


def _slot_copy(src, s, dst, p, sem, ch):
    return pltpu.make_async_copy(src.at[pl.ds(s * ch, ch), :], dst.at[pl.ds(p * ch, ch), :], sem)


def _dispatch_kernel(off_ref, cnt_ref, pad_ref, na_ref, x_ref, idx_ref, xs_hbm, pk, sem, *, tm, n_e, n_tiles, ch):
    half = x_ref.shape[1] // 2
    rc = PACK_ROWS
    for r0 in range(0, tm, rc):
        hi = lax.bitcast_convert_type(x_ref[r0:r0 + rc, :half].astype(F32), U32)
        lo = lax.bitcast_convert_type(x_ref[r0:r0 + rc, half:].astype(F32), U32)
        p = hi | (lo >> 16)
        for c in range(ch - 1):
            pk[pl.ds(r0 * ch + c, rc, stride=ch), :] = p[:, c * LANES:(c + 1) * LANES]
        pk[pl.ds(r0 * ch + ch - 1, rc, stride=ch), :] = jnp.zeros((rc, LANES), U32)
        for s in range(r0, r0 + rc):
            _slot_copy(pk, s, xs_hbm, idx_ref[0, s], sem, ch).start()
            _slot_copy(pk, s, xs_hbm, idx_ref[1, s], sem, ch).start()

    @pl.when(pl.program_id(0) == 0)
    def _():
        for e in range(n_e):
            base = off_ref[e] + cnt_ref[e]
            n_pad = pad_ref[e] - cnt_ref[e]

            def fill(s, c):
                _slot_copy(pk, 0, xs_hbm, base + s, sem, ch).start()
                return c

            def drain(s, c):
                _slot_copy(pk, 0, xs_hbm, base, sem, ch).wait()
                return c

            lax.fori_loop(0, n_pad, fill, 0)
            lax.fori_loop(0, n_pad, drain, 0)

        rows = EXPERT_TILE * ch

        def tail_copy(i):
            return pltpu.make_async_copy(pk.at[pl.ds(0, rows), :],
                                         xs_hbm.at[pl.ds(pl.multiple_of(i * rows, rows), rows), :], sem)

        def tail_fill(i, c):
            tail_copy(i).start()
            return c

        def tail_drain(i, c):
            tail_copy(i).wait()
            return c

        lax.fori_loop(na_ref[0], n_tiles, tail_fill, 0)
        lax.fori_loop(na_ref[0], n_tiles, tail_drain, 0)

    pltpu.make_async_copy(xs_hbm.at[pl.ds(0, 2 * tm * ch), :], xs_hbm.at[pl.ds(0, 2 * tm * ch), :], sem).wait()


def _dispatch(xb, pos, off, cnt, padded, na, n_slots, *, tm, name):
    t, d = xb.shape
    tm = _pick(t, tm)
    n_e = off.shape[0]
    ch = _slot_rows(d)
    assert tm >= EXPERT_TILE and tm % PACK_ROWS == 0 and d % (2 * LANES) == 0
    return pl.pallas_call(
        functools.partial(_dispatch_kernel, tm=tm, n_e=n_e, n_tiles=n_slots // EXPERT_TILE, ch=ch),
        grid_spec=pltpu.PrefetchScalarGridSpec(
            num_scalar_prefetch=4, grid=(t // tm,),
            in_specs=[pl.BlockSpec((tm, d), lambda i, *_: (i, 0)),
                      pl.BlockSpec((ROUTE_ROWS, tm), lambda i, *_: (0, i), memory_space=pltpu.SMEM)],
            out_specs=pl.BlockSpec(memory_space=pl.ANY),
            scratch_shapes=[pltpu.VMEM((tm * ch, LANES), U32), pltpu.SemaphoreType.DMA(())]),
        out_shape=jax.ShapeDtypeStruct((n_slots * ch, LANES), U32),
        compiler_params=_params("arbitrary"),
        name=name,
    )(off, cnt, padded, na, xb, pos)


N_WEIGHT_PIECES = 6


def _experts_kernel(te_ref, nxt_ref, na_ref, xs_ref, wg_hbm, wu_hbm, wd_hbm, ys_ref,
                    wgb, wub, wdb, st_in, st_out, state, sem, *, layer):
    i = pl.program_id(0)
    hd, hf = st_in.shape[0], st_out.shape[0]

    def piece_copy(e, k):
        if k < 4:
            src = (wg_hbm if k < 2 else wu_hbm).at[layer, e, pl.ds((k % 2) * hd, hd), :]
            return pltpu.make_async_copy(src, st_in, sem)
        return pltpu.make_async_copy(wd_hbm.at[layer, e, pl.ds((k - 4) * hf, hf), :], st_out, sem)

    def for_piece(k, fn):
        for kk in range(N_WEIGHT_PIECES):
            pl.when(k == kk)(functools.partial(fn, kk))

    def start_piece(e, k):
        for_piece(k, lambda kk: piece_copy(e, kk).start())

    def finish_piece(e, k, slot):
        def fn(kk):
            piece_copy(e, kk).wait()
            if kk < 4:
                dst = wgb if kk < 2 else wub
                dst[slot, pl.ds((kk % 2) * hd, hd), :] = st_in[...].astype(BF16)
            else:
                wdb[slot, pl.ds((kk - 4) * hf, hf), :] = st_out[...].astype(BF16)

        for_piece(k, fn)

    def advance(e, slot):
        k = state[1]
        finish_piece(e, k, slot)
        state[1] = k + 1

        @pl.when(k + 1 < N_WEIGHT_PIECES)
        def _():
            start_piece(e, k + 1)

    def finish_all(e, slot):
        lax.fori_loop(state[1], N_WEIGHT_PIECES, lambda _, c: (advance(e, slot), c)[1], 0)

    def begin(e):
        state[1] = 0
        state[2] = e

        @pl.when(e >= 0)
        def _():
            start_piece(e, 0)

    active = i < na_ref[0]

    @pl.when(i == 0)
    def _():
        state[0] = 1
        begin(te_ref[0])

    switch = active & ((i == 0) | (te_ref[i] != te_ref[jnp.maximum(i - 1, 0)]))

    @pl.when(switch)
    def _():
        slot = 1 - state[0]
        finish_all(te_ref[i], slot)
        state[0] = slot
        begin(nxt_ref[i])

    @pl.when(active & jnp.logical_not(switch) & (state[2] >= 0) & (state[1] < N_WEIGHT_PIECES))
    def _():
        advance(state[2], 1 - state[0])

    @pl.when(active)
    def _():
        slot = state[0]
        ch = xs_ref.shape[0] // ys_ref.shape[0]
        parts = [_unpack_halves(xs_ref[pl.ds(c, ys_ref.shape[0], stride=ch), :]) for c in range(ch - 1)]
        hi = jnp.concatenate([p[0].astype(BF16) for p in parts], axis=1)
        lo = jnp.concatenate([p[1].astype(BF16) for p in parts], axis=1)
        half = hi.shape[1]
        hg = (jnp.dot(hi, wgb[slot, :half], preferred_element_type=F32)
              + jnp.dot(lo, wgb[slot, half:], preferred_element_type=F32))
        hu = (jnp.dot(hi, wub[slot, :half], preferred_element_type=F32)
              + jnp.dot(lo, wub[slot, half:], preferred_element_type=F32))
        hid = (hg * _sigmoid(hg) * hu).astype(BF16)
        ys_ref[...] = _pack_halves(jnp.dot(hid, wdb[slot], preferred_element_type=F32))

    @pl.when(jnp.logical_not(active))
    def _():
        ys_ref[...] = jnp.zeros(ys_ref.shape, ys_ref.dtype)


def _experts(xs, te, nxt, na, wg, wu, wd, layer, *, name):
    d, f = wg.shape[2:]
    w = d // 2
    ch = _slot_rows(d)
    n_slots = xs.shape[0] // ch
    tm = EXPERT_TILE
    nt = n_slots // tm
    assert d % 2 == 0 and f % 2 == 0 and xs.shape[1] == LANES

    def tile(i, te_ref, nxt_ref, na_ref):
        return (jnp.minimum(i, na_ref[0] - 1), 0)

    hbm = pl.BlockSpec(memory_space=pl.ANY)
    return pl.pallas_call(
        functools.partial(_experts_kernel, layer=layer),
        grid_spec=pltpu.PrefetchScalarGridSpec(
            num_scalar_prefetch=3, grid=(nt,),
            in_specs=[pl.BlockSpec((tm * ch, LANES), tile), hbm, hbm, hbm],
            out_specs=pl.BlockSpec((tm, w), lambda i, *_: (i, 0)),
            scratch_shapes=[pltpu.VMEM((2, d, f), BF16), pltpu.VMEM((2, d, f), BF16), pltpu.VMEM((2, f, d), BF16),
                            pltpu.VMEM((d // 2, f), F32), pltpu.VMEM((f // 2, d), F32),
                            pltpu.SMEM((3,), I32), pltpu.SemaphoreType.DMA(())]),
        out_shape=jax.ShapeDtypeStruct((n_slots, w), U32),
        compiler_params=_params("arbitrary"),
        name=name,
    )(te, nxt, na, xs, wg, wu, wd)


def _combine_kernel(idx_ref, nidx_ref, x_ref, rw_ref, g_ref, b_ref, ys_hbm, o_ref, ob_ref, buf_a, buf_b, sem, *,
                    tm, alpha):
    i = pl.program_id(0)
    n = pl.num_programs(0)

    def start_rows(ref, buf, slot_sem, r0, rows):
        for s in range(r0, r0 + rows):
            _row_copy(ys_hbm, ref[0, s], buf, s, slot_sem).start()
            _row_copy(ys_hbm, ref[1, s], buf, tm + s, slot_sem).start()

    def wait_tile(buf, slot_sem):
        pltpu.make_async_copy(ys_hbm.at[pl.ds(0, 2 * tm), :], buf, slot_sem).wait()

    @pl.when(i == 0)
    def _():
        def body(s, c):
            _row_copy(ys_hbm, idx_ref[0, s], buf_a, s, sem.at[0]).start()
            _row_copy(ys_hbm, idx_ref[1, s], buf_a, tm + s, sem.at[0]).start()
            return c

        lax.fori_loop(0, tm, body, 0, unroll=DMA_UNROLL)

    for parity, (cur, nxt) in enumerate(((buf_a, buf_b), (buf_b, buf_a))):
        @pl.when(i % 2 == parity)
        def _(cur=cur, nxt=nxt, parity=parity):
            wait_tile(cur, sem.at[parity])
            _combine_rows(cur, nidx_ref, nxt, sem.at[1 - parity], start_rows, x_ref, rw_ref, g_ref, b_ref,
                          o_ref, ob_ref, tm=tm, alpha=alpha)

            @pl.when(i == n - 1)
            def _():
                wait_tile(nxt, sem.at[1 - parity])


def _combine_rows(buf, nidx_ref, nxt_buf, nxt_sem, start_rows, x_ref, rw_ref, g_ref, b_ref, o_ref, ob_ref, *,
                  tm, alpha):
    rc = LN_ROWS
    half = buf.shape[-1]
    d = 2 * half

    for r0 in range(0, tm, rc):
        start_rows(nidx_ref, nxt_buf, nxt_sem, r0, rc)
        sl = pl.ds(r0, rc)
        hi1, lo1 = _unpack_halves(buf[sl, :])
        hi2, lo2 = _unpack_halves(buf[pl.ds(tm + r0, rc), :])
        w1 = rw_ref[sl, 0:1]
        w2 = rw_ref[sl, 1:2]
        s_hi = alpha * x_ref[sl, :half] + (w1 * hi1 + w2 * hi2)
        s_lo = alpha * x_ref[sl, half:] + (w1 * lo1 + w2 * lo2)
        mu = (jnp.sum(s_hi, axis=-1, keepdims=True) + jnp.sum(s_lo, axis=-1, keepdims=True)) / d
        c_hi, c_lo = s_hi - mu, s_lo - mu
        var = (jnp.sum(c_hi * c_hi, axis=-1, keepdims=True) + jnp.sum(c_lo * c_lo, axis=-1, keepdims=True)) / d
        inv = lax.rsqrt(var + LN_EPS)
        y_hi = c_hi * inv * g_ref[:, :half] + b_ref[:, :half]
        y_lo = c_lo * inv * g_ref[:, half:] + b_ref[:, half:]
        o_ref[sl, :half] = y_hi
        o_ref[sl, half:] = y_lo
        ob_ref[sl, :half] = y_hi.astype(BF16)
        ob_ref[sl, half:] = y_lo.astype(BF16)


def _combine(x2, rw, pos, ys, g, b, layer, alpha, *, tm, name):
    t, d = x2.shape
    tm = _pick(t, tm)
    nsteps = t // tm
    w = ys.shape[1]
    assert tm % LN_ROWS == 0
    row = lambda width: pl.BlockSpec((tm, width), lambda i: (i, 0))
    vec = pl.BlockSpec((None, 1, d), lambda i: (layer, 0, 0))
    return pl.pallas_call(
        functools.partial(_combine_kernel, tm=tm, alpha=alpha),
        grid=(nsteps,),
        in_specs=[pl.BlockSpec((ROUTE_ROWS, tm), lambda i: (0, i), memory_space=pltpu.SMEM),
                  pl.BlockSpec((ROUTE_ROWS, tm), lambda i: (0, jnp.minimum(i + 1, nsteps - 1)),
                               memory_space=pltpu.SMEM),
                  row(d), row(LANES), vec, vec,
                  pl.BlockSpec(memory_space=pl.ANY)],
        out_specs=[row(d), row(d)],
        out_shape=[jax.ShapeDtypeStruct((t, d), F32), jax.ShapeDtypeStruct((t, d), BF16)],
        scratch_shapes=[pltpu.VMEM((2 * tm, w), U32), pltpu.VMEM((2 * tm, w), U32), pltpu.SemaphoreType.DMA((2,))],
        compiler_params=_params("arbitrary"),
        name=name,
    )(pos, pos, x2, rw, g, b, ys)


def _expert_layout(counts, ridx, n_tiles):
    cnt = counts.reshape(-1).astype(I32)
    padded = (cnt + EXPERT_TILE - 1) // EXPERT_TILE * EXPERT_TILE
    ends = jnp.cumsum(padded)
    off = ends - padded
    tile_ends = ends // EXPERT_TILE
    na = tile_ends[-1:]
    tiles = jnp.minimum(jnp.arange(n_tiles, dtype=I32), na[0] - 1)
    expert_of = lambda tile: jnp.sum((tile[:, None] >= tile_ends[None, :]).astype(I32), axis=1)
    te = expert_of(tiles)
    seg_end = jnp.sum(jnp.where(te[:, None] == jnp.arange(cnt.shape[0], dtype=I32)[None, :], tile_ends[None, :], 0), axis=1)
    nxt = jnp.where(seg_end < na[0], expert_of(seg_end), -1)
    e, rank = ridx[0:2], ridx[2:4]
    base = jnp.sum(jnp.where(e[:, None, :] == jnp.arange(cnt.shape[0], dtype=I32)[None, :, None], off[None, :, None], 0),
                   axis=1)
    pos = jnp.concatenate([base + rank, jnp.zeros((ROUTE_ROWS - 2, ridx.shape[1]), I32)], axis=0)
    return off, cnt, padded, te, nxt, na, pos


def kernel(x, mem, w_in, conv_w, w_conv_proj, w_attn_proj, w_out, rel_bias, ln1_g, ln1_b, w_mem_q, w_mem_kv,
           w_mem_o, ln2_g, ln2_b, w_router, b_router, w_gate, w_up, w_down, ln3_g, ln3_b):
    batch, seq, d = x.shape
    depth = w_in.shape[0]
    t = batch * seq
    alpha = (2 * depth) ** 0.25
    cc = conv_w.shape[-1]
    n_groups = len(ATTN_GROUPS)
    aw = w_attn_proj.shape[1]
    hpg = aw // HEAD_DIM
    n_e = w_router.shape[1]
    assert rel_bias.shape[1] == n_groups * hpg
    qkv_w = 3 * n_groups * aw
    n_tiles = -(-(2 * t) // EXPERT_TILE) + n_e
    n_slots = n_tiles * EXPERT_TILE

    bf = lambda a: a.astype(BF16)
    wc_b, wa_b = bf(w_conv_proj), bf(w_attn_proj)
    wrt_b = bf(w_router.T)
    br = b_router.reshape(1, -1).astype(F32)
    vec3 = lambda a: a.reshape(depth, 1, d)
    ln1_g, ln1_b, ln2_g, ln2_b, ln3_g, ln3_b = map(vec3, (ln1_g, ln1_b, ln2_g, ln2_b, ln3_g, ln3_b))
    tbl = rel_bias.T.astype(F32)
    mem_b = bf(mem).reshape(batch * mem.shape[1], d)

    xf = x.reshape(t, d)
    xb = bf(xf)
    for l in range(depth):
        cin = _matmul(xb, w_in, l, BF16, tm=1024, tn=768, col0=0, ncols=3 * cc, name=f"inproj_conv{l}")
        qkv = _matmul(xb, w_in, l, F32, tm=1024, tn=768, col0=3 * cc, ncols=qkv_w, name=f"inproj_qkv{l}")
        gates = _matmul(xb, w_in, l, BF16, tm=1024, tn=1024, col0=3 * cc + qkv_w, ncols=2 * d, sigmoid=True,
                        name=f"inproj_gate{l}")
        others = []
        for g, (window, r) in enumerate(ATTN_GROUPS):
            last = g == n_groups - 1
            res = _dilated_attention(qkv, tbl[g * hpg:(g + 1) * hpg], g, n_groups, hpg, batch, seq, window, r,
                                     others if last else [], name=f"dattn{l}_{g}")
            if not last:
                others += list(res)
        merged = _merge(cin, gates, res, conv_w, wc_b, wa_b, l, seq, d, tm=1024, tn=512, name=f"merge{l}")
        s1 = _matmul(merged, w_out, l, F32, tm=512, tn=1024, resid=xf, alpha=alpha, name=f"outproj{l}")
        kv = _matmul(mem_b, w_mem_kv, l, BF16, tm=512, tn=1024, name=f"memkv{l}")
        kv = kv.reshape(batch, mem.shape[1], kv.shape[-1])
        x2, x2b, rw, ridx, counts = _xattn(s1, kv, ln1_g, ln1_b, w_mem_q, w_mem_o, ln2_g, ln2_b, wrt_b, br, l, seq, alpha,
                                           tm=256, name=f"xattn{l}")
        off, cnt, padded, te, nxt, na, pos = _expert_layout(counts, ridx, n_tiles)
        xs = _dispatch(x2b, pos, off, cnt, padded, na, n_slots, tm=1024, name=f"dispatch{l}")
        ys = _experts(xs, te, nxt, na, w_gate, w_up, w_down, l, name=f"experts{l}")
        xf, xb = _combine(x2, rw, pos, ys, ln3_g, ln3_b, l, alpha, tm=256, name=f"combine{l}")
    return xf.reshape(batch, seq, d)
```

```python
import functools
import math

import jax
import jax.numpy as jnp
from jax import lax
from jax.experimental import pallas as pl
from jax.experimental.pallas import tpu as pltpu

HEAD_DIM = 128
ATTN_GROUPS = ((128, 1), (512, 4), (2048, 16))
ATTN_BLOCK = 128
N_BUCKETS = 32
MAX_DISTANCE = 2048
MEM_HEAD_DIM = 128
N_EXPERT_GROUPS = 4
LN_EPS = 1e-5
NEG_INF = -1e30

VMEM_LIMIT_BYTES = 56 * 1024 * 1024
LANES = 128
ROUTE_ROWS = 8
EXPERT_TILE = 256
LN_ROWS = 64
MERGE_ROWS = 32
CONV_ROWS = 64
DMA_UNROLL = 8
PACK_ROWS = 64
STRIDE_STEP = 4

F32, BF16, U32, I32 = jnp.float32, jnp.bfloat16, jnp.uint32, jnp.int32


def _params(*sem):
    return pltpu.CompilerParams(dimension_semantics=sem, vmem_limit_bytes=VMEM_LIMIT_BYTES)


def _pick(n, pref):
    if n <= pref:
        return n
    t = pref
    while n % t:
        t //= 2
    return t


def _sigmoid(x):
    return 1.0 / (1.0 + jnp.exp(-x))


def _layer_norm(x, g, b):
    mu = jnp.mean(x, axis=-1, keepdims=True)
    xc = x - mu
    var = jnp.mean(xc * xc, axis=-1, keepdims=True)
    return xc * lax.rsqrt(var + LN_EPS) * g + b


def _pack_halves(x):
    half = x.shape[1] // 2
    hi = lax.bitcast_convert_type(x[:, :half].astype(BF16).astype(F32), U32)
    lo = lax.bitcast_convert_type(x[:, half:].astype(BF16).astype(F32), U32)
    return hi | (lo >> 16)


def _unpack_halves(p):
    hi = lax.bitcast_convert_type(p & jnp.uint32(0xFFFF0000), F32)
    lo = lax.bitcast_convert_type(p << 16, F32)
    return hi, lo


def _mm_kernel(*refs, sigmoid, alpha, layer, col0, tn):
    if alpha is None:
        a_ref, w_hbm, o_ref, wst_s, wb_s, sem = refs
    else:
        a_ref, w_hbm, r_ref, o_ref, wst_s, wb_s, sem = refs
    j = pl.program_id(0)

    def w_copy(jj):
        cols = pl.ds(pl.multiple_of(col0 + jj * tn, LANES), tn)
        return pltpu.make_async_copy(w_hbm.at[layer, :, cols], wst_s, sem)

    @pl.when(pl.program_id(1) == 0)
    def _():
        @pl.when(j == 0)
        def _():
            w_copy(0).start()

        w_copy(j).wait()
        wb_s[...] = wst_s[...].astype(BF16)

        @pl.when(j + 1 < pl.num_programs(0))
        def _():
            w_copy(j + 1).start()

    acc = jnp.dot(a_ref[...], wb_s[...], preferred_element_type=F32)
    if alpha is not None:
        acc = acc + alpha * r_ref[...]
    if sigmoid:
        acc = _sigmoid(acc)
    o_ref[...] = acc.astype(o_ref.dtype)


def _matmul(a, w, layer, out_dtype, *, tm, tn, col0=0, ncols=None, sigmoid=False, resid=None, alpha=None, name):
    m, k = a.shape
    n = w.shape[-1] - col0 if ncols is None else ncols
    tm, tn = _pick(m, tm), _pick(n, tn)
    assert col0 % LANES == 0 and tn % LANES == 0
    in_specs = [pl.BlockSpec((tm, k), lambda j, i: (i, 0)),
                pl.BlockSpec(memory_space=pl.ANY)]
    args = [a, w]
    if resid is not None:
        in_specs.append(pl.BlockSpec((tm, tn), lambda j, i: (i, j)))
        args.append(resid)
    return pl.pallas_call(
        functools.partial(_mm_kernel, sigmoid=sigmoid, alpha=alpha, layer=layer, col0=col0, tn=tn),
        grid=(n // tn, m // tm),
        in_specs=in_specs,
        out_specs=pl.BlockSpec((tm, tn), lambda j, i: (i, j)),
        out_shape=jax.ShapeDtypeStruct((m, n), out_dtype),
        scratch_shapes=[pltpu.VMEM((k, tn), F32), pltpu.VMEM((k, tn), BF16), pltpu.SemaphoreType.DMA(())],
        compiler_params=_params("arbitrary", "arbitrary"),
        name=name,
    )(*args)


def _dattn_kernel(bucket_ref, tbl_ref, q_ref, k_ref, v_ref, kp_ref, vp_ref, *rest, r, pt, span, scale, n_other):
    others = rest[:2 * n_other]
    two_level = r > STRIDE_STEP
    if two_level:
        rest, lvl = rest[:-1], rest[-1]
    if n_other:
        oatt_ref, bias_s, qs, kcat, vcat, o_ref, lse_ref = rest[2 * n_other:]
    else:
        o_ref, lse_ref, bias_s, qs, kcat, vcat = rest
    blk = ATTN_BLOCK
    n = pl.program_id(1)
    h = pl.program_id(2)
    ch = pt // r
    nb = ch // blk

    @pl.when((pl.program_id(0) == 0) & (n == 0))
    def _():
        bk = bucket_ref[...]
        row = lax.broadcasted_iota(I32, (blk, 2 * blk), 0)
        col = lax.broadcasted_iota(I32, (blk, 2 * blk), 1)
        rel = row + blk - col
        band = (rel >= 0) & (rel <= span)
        bias = jnp.zeros((blk, 2 * blk), F32)
        for b in range(N_BUCKETS):
            bias = jnp.where(bk == b, tbl_ref[h, b], bias)
        bias_s[h, 0] = jnp.where(band, bias, NEG_INF)
        bias_s[h, 1] = jnp.where(band & (col >= blk), bias, NEG_INF)

    srcs = (q_ref, k_ref, v_ref, kp_ref, vp_ref)
    if two_level:
        for idx, ref in enumerate(srcs):
            n4 = ref.shape[0] // STRIDE_STEP
            for rho1 in range(STRIDE_STEP):
                lvl[idx, rho1, 0:n4] = ref[pl.ds(rho1, n4, stride=STRIDE_STEP), :]

    def rows(ref, start, size):
        if r == 1:
            return ref[pl.ds(start, size), :]
        if two_level:
            idx = [i for i, s in enumerate(srcs) if s is ref][0]
            return lvl[idx, start % STRIDE_STEP, pl.ds(start // STRIDE_STEP, size, stride=r // STRIDE_STEP), :]
        return ref[pl.ds(start, size, stride=r), :]

    for rho in range(r):
        qs[...] = rows(q_ref, rho, ch).astype(BF16)
        kcat[0:blk] = rows(kp_ref, rho, blk).astype(BF16)
        kcat[blk:] = rows(k_ref, rho, ch).astype(BF16)
        vcat[0:blk] = rows(vp_ref, rho, blk).astype(BF16)
        vcat[blk:] = rows(v_ref, rho, ch).astype(BF16)
        for b in range(nb):
            first = jnp.where(n == 0, 1, 0) if b == 0 else 0
            q = qs[b * blk:(b + 1) * blk]
            k = kcat[b * blk:(b + 2) * blk]
            v = vcat[b * blk:(b + 2) * blk]
            lg = lax.dot_general(q, k, (((1,), (1,)), ((), ())), preferred_element_type=F32) * scale
            lg = lg + bias_s[h, first]
            m = jnp.max(lg, axis=-1, keepdims=True)
            p = jnp.exp(lg - m)
            l = jnp.sum(p, axis=-1, keepdims=True)
            o = jnp.dot(p.astype(BF16), v, preferred_element_type=F32) / l
            lse = jnp.broadcast_to(m + jnp.log(l), (blk, LANES))
            start = rho + b * blk * r
            if r == 1:
                o_ref[pl.ds(start, blk), :] = o
                lse_ref[pl.ds(start, blk), :] = lse
            else:
                o_ref[pl.ds(start, blk, stride=r), :] = o
                lse_ref[pl.ds(start, blk, stride=r), :] = lse

    if n_other:
        rc = MERGE_ROWS

        def merge_rows(c, carry):
            sl = pl.ds(pl.multiple_of(c * rc, rc), rc)
            ls = [others[2 * g + 1][sl, :] for g in range(n_other)] + [lse_ref[sl, :]]
            os_ = [others[2 * g][sl, :] for g in range(n_other)] + [o_ref[sl, :]]
            mx = functools.reduce(jnp.maximum, ls)
            es = [jnp.exp(x - mx) for x in ls]
            den = functools.reduce(lambda a, b: a + b, es)
            acc = es[0] * os_[0]
            for e, o in zip(es[1:], os_[1:]):
                acc = acc + e * o
            oatt_ref[sl, :] = (acc / den).astype(oatt_ref.dtype)
            return carry

        lax.fori_loop(0, pt // rc, merge_rows, 0, unroll=4)


def _t5_bucket(dist):
    max_exact = N_BUCKETS // 2
    d = jnp.maximum(dist, 0)
    log_part = jnp.log(jnp.maximum(d, 1).astype(F32) / max_exact) / math.log(MAX_DISTANCE / max_exact)
    large = jnp.minimum(max_exact + (log_part * (N_BUCKETS - max_exact)).astype(I32), N_BUCKETS - 1)
    return jnp.where(d < max_exact, d, large)


def _dilated_attention(qkv, tbl, g, n_groups, hpg, batch, seq, window, r, others, name):
    blk = ATTN_BLOCK
    t = qkv.shape[0]
    pt = blk * ATTN_GROUPS[-1][1]
    assert seq % pt == 0 and pt % (blk * r) == 0 and HEAD_DIM == LANES
    tiles = seq // pt
    span = window // r
    qi = jnp.arange(blk)[:, None] + blk
    ki = jnp.arange(2 * blk)[None, :]
    bucket = _t5_bucket((qi - ki) * r).astype(I32)
    heads = n_groups * hpg
    prev_rows = blk * r
    ppt = pt // prev_rows

    def cur(which):
        return pl.BlockSpec((pt, LANES), lambda b, n, h: (b * tiles + n, which * heads + g * hpg + h))

    def prev(which):
        return pl.BlockSpec((prev_rows, LANES),
                            lambda b, n, h: (jnp.maximum((b * tiles + n) * ppt - 1, 0), which * heads + g * hpg + h))

    out = pl.BlockSpec((pt, LANES), lambda b, n, h: (b * tiles + n, h))
    scratch = [pltpu.VMEM((hpg, 2, blk, 2 * blk), F32),
               pltpu.VMEM((pt // r, LANES), BF16),
               pltpu.VMEM((pt // r + blk, LANES), BF16),
               pltpu.VMEM((pt // r + blk, LANES), BF16)]
    if others:
        assert pt % MERGE_ROWS == 0
        out_specs, out_shape = out, jax.ShapeDtypeStruct((t, hpg * LANES), BF16)
        scratch += [pltpu.VMEM((pt, LANES), F32)] * 2
    else:
        out_specs, out_shape = [out, out], [jax.ShapeDtypeStruct((t, hpg * LANES), F32)] * 2
    if r > STRIDE_STEP:
        assert r % STRIDE_STEP == 0 and prev_rows <= pt
        scratch += [pltpu.VMEM((5, STRIDE_STEP, pt // STRIDE_STEP, LANES), F32)]
    return pl.pallas_call(
        functools.partial(_dattn_kernel, r=r, pt=pt, span=span, scale=HEAD_DIM ** -0.5, n_other=len(others) // 2),
        grid=(batch, tiles, hpg),
        in_specs=[pl.BlockSpec((blk, 2 * blk), lambda b, n, h: (0, 0)),
                  pl.BlockSpec(memory_space=pltpu.SMEM),
                  cur(0), cur(1), cur(2), prev(1), prev(2)] + [out] * len(others),
        out_specs=out_specs,
        out_shape=out_shape,
        scratch_shapes=scratch,
        compiler_params=_params("arbitrary", "arbitrary", "arbitrary"),
        name=name,
    )(bucket, tbl, qkv, qkv, qkv, qkv, qkv, *others)


def _merge_kernel(b_ref, c_ref, h_ref, cp_ref, hp_ref, cw_ref, oatt_ref, sgc_ref, sga_ref, wc_ref, wa_ref, out_ref,
                  u_s, aconv_s, *, conv_k, tm, halo, seq_tiles):
    i = pl.program_id(0)

    @pl.when(pl.program_id(1) == 0)
    def _():
        seq_start = (i % seq_tiles) == 0
        up = cp_ref[...].astype(F32) * hp_ref[...].astype(F32)
        u_s[0:halo] = jnp.where(seq_start, 0.0, up)
        rc = CONV_ROWS
        for r0 in range(0, tm, rc):
            u_s[halo + r0:halo + r0 + rc] = c_ref[r0:r0 + rc].astype(F32) * h_ref[r0:r0 + rc].astype(F32)
        for r0 in range(0, tm, rc):
            lo = halo - (conv_k - 1) + r0
            conv = cw_ref[0:1, :] * u_s[lo:lo + rc]
            for j in range(1, conv_k):
                conv = conv + cw_ref[j:j + 1, :] * u_s[lo + j:lo + j + rc]
            aconv_s[r0:r0 + rc] = (b_ref[r0:r0 + rc].astype(F32) * conv).astype(BF16)

    yc = jnp.dot(aconv_s[...], wc_ref[...], preferred_element_type=F32)
    ya = jnp.dot(oatt_ref[...], wa_ref[...], preferred_element_type=F32)
    out_ref[...] = (sgc_ref[...].astype(F32) * yc + sga_ref[...].astype(F32) * ya).astype(out_ref.dtype)


def _merge(cin, gates, oatt, conv_w, wc, wa, layer, seq, d_model, *, tm, tn, name):
    t = cin.shape[0]
    conv_k, cc = conv_w.shape[-2:]
    aw = oatt.shape[-1]
    tm, tn = _pick(seq, tm), _pick(d_model, tn)
    halo = 16
    assert cc % LANES == 0 and tm % halo == 0 and conv_k - 1 <= halo and tm % CONV_ROWS == 0
    nj = d_model // tn

    def colblk(c):
        return pl.BlockSpec((tm, cc), lambda i, j: (i, c))

    def halo_blk(c):
        return pl.BlockSpec((halo, cc), lambda i, j: (jnp.maximum(i * (tm // halo) - 1, 0), c))

    in_specs = [colblk(0), colblk(1), colblk(2), halo_blk(1), halo_blk(2),
                pl.BlockSpec((None, conv_k, cc), lambda i, j: (layer, 0, 0)),
                pl.BlockSpec((tm, aw), lambda i, j: (i, 0)),
                pl.BlockSpec((tm, tn), lambda i, j: (i, j)),
                pl.BlockSpec((tm, tn), lambda i, j: (i, nj + j)),
                pl.BlockSpec((None, cc, tn), lambda i, j: (layer, 0, j)),
                pl.BlockSpec((None, aw, tn), lambda i, j: (layer, 0, j))]
    return pl.pallas_call(
        functools.partial(_merge_kernel, conv_k=conv_k, tm=tm, halo=halo, seq_tiles=seq // tm),
        grid=(t // tm, nj),
        in_specs=in_specs,
        out_specs=pl.BlockSpec((tm, tn), lambda i, j: (i, j)),
        out_shape=jax.ShapeDtypeStruct((t, d_model), BF16),
        scratch_shapes=[pltpu.VMEM((tm + halo, cc), F32), pltpu.VMEM((tm, cc), BF16)],
        compiler_params=_params("parallel", "arbitrary"),
        name=name,
    )(cin, cin, cin, cin, cin, conv_w, oatt, gates, gates, wc, wa)


def _route(p):
    epg = len(p) // N_EXPERT_GROUPS
    assert epg == 4
    scores = []
    for g in range(N_EXPERT_GROUPS):
        a, b, c, d = p[g * epg:(g + 1) * epg]
        hi1, lo1, hi2, lo2 = jnp.maximum(a, b), jnp.minimum(a, b), jnp.maximum(c, d), jnp.minimum(c, d)
        scores.append(jnp.maximum(hi1, hi2) + jnp.maximum(jnp.minimum(hi1, hi2), jnp.maximum(lo1, lo2)))
    best, gidx = scores[0], jnp.zeros(scores[0].shape, I32)
    for g in range(1, N_EXPERT_GROUPS):
        take = scores[g] > best
        best = jnp.where(take, scores[g], best)
        gidx = jnp.where(take, g, gidx)
    sel = []
    for k in range(epg):
        s = p[k]
        for g in range(1, N_EXPERT_GROUPS):
            s = jnp.where(gidx == g, p[g * epg + k], s)
        sel.append(s)
    v1, i1 = sel[0], jnp.zeros(sel[0].shape, I32)
    for k in range(1, epg):
        take = sel[k] > v1
        v1 = jnp.where(take, sel[k], v1)
        i1 = jnp.where(take, k, i1)
    v2, i2 = jnp.full(v1.shape, -1.0, F32), jnp.zeros(v1.shape, I32)
    for k in range(epg):
        take = (i1 != k) & (sel[k] > v2)
        v2 = jnp.where(take, sel[k], v2)
        i2 = jnp.where(take, k, i2)
    tot = v1 + v2
    return gidx * epg + i1, gidx * epg + i2, v1 / tot, v2 / tot


def _xattn_kernel(s1_ref, g1_ref, b1_ref, wq_hbm, kv_ref, wo_hbm, g2_ref, b2_ref, wrt_ref, br_ref,
                  x_ref, xb_ref, rw_ref, ridx_ref, cnt_ref, tril_s, carry_s, wq_ref, wo_ref, stq_s, sto_s, wsem, *,
                  heads, alpha, scale, tm, layer):
    i = pl.program_id(0)

    @pl.when(i == 0)
    def _():
        for src, dst, stage in ((wq_hbm, wq_ref, stq_s), (wo_hbm, wo_ref, sto_s)):
            rows = stage.shape[0]
            for r0 in range(0, dst.shape[0], rows):
                cp = pltpu.make_async_copy(src.at[layer, pl.ds(r0, rows), :], stage, wsem)
                cp.start()
                cp.wait()
                dst[pl.ds(r0, rows), :] = stage[...].astype(BF16)

    x1 = _layer_norm(s1_ref[...], g1_ref[...], b1_ref[...])
    q = jnp.dot(x1.astype(BF16), wq_ref[...], preferred_element_type=F32).astype(BF16)
    mw = heads * MEM_HEAD_DIM
    outs = []
    for h in range(heads):
        cs = slice(h * MEM_HEAD_DIM, (h + 1) * MEM_HEAD_DIM)
        k = kv_ref[:, cs]
        v = kv_ref[:, mw + h * MEM_HEAD_DIM:mw + (h + 1) * MEM_HEAD_DIM]
        lg = lax.dot_general(q[:, cs], k, (((1,), (1,)), ((), ())), preferred_element_type=F32) * scale
        m = jnp.max(lg, axis=-1, keepdims=True)
        p = jnp.exp(lg - m)
        l = jnp.sum(p, axis=-1, keepdims=True)
        outs.append((jnp.dot(p.astype(BF16), v, preferred_element_type=F32) / l).astype(BF16))
    o = jnp.concatenate(outs, axis=-1)
    xa = jnp.dot(o, wo_ref[...], preferred_element_type=F32)
    x2 = _layer_norm(alpha * x1 + xa, g2_ref[...], b2_ref[...])
    x_ref[...] = x2
    half = x2.shape[1] // 2
    x2_hi, x2_lo = x2[:, :half].astype(BF16), x2[:, half:].astype(BF16)
    xb_ref[:, :half] = x2_hi
    xb_ref[:, half:] = x2_lo

    nt = (((1,), (1,)), ((), ()))
    lg = (lax.dot_general(x2_hi, wrt_ref[:, :half], nt, preferred_element_type=F32)
          + lax.dot_general(x2_lo, wrt_ref[:, half:], nt, preferred_element_type=F32)
          + br_ref[...])
    m = jnp.max(lg, axis=-1, keepdims=True)
    ex = jnp.exp(lg - m)
    probs = ex / jnp.sum(ex, axis=-1, keepdims=True)
    n_e = probs.shape[-1]
    e1, e2, w1, w2 = _route([probs[:, e:e + 1] for e in range(n_e)])
    lane = lax.broadcasted_iota(I32, (tm, LANES), 1)
    rw_ref[...] = jnp.where(lane == 0, w1, jnp.where(lane == 1, w2, 0.0))

    @pl.when(i == 0)
    def _():
        rr = lax.broadcasted_iota(I32, (tm, tm), 0)
        cc = lax.broadcasted_iota(I32, (tm, tm), 1)
        tril_s[...] = jnp.where(cc <= rr, 1.0, 0.0).astype(BF16)
        carry_s[...] = jnp.zeros(carry_s.shape, F32)

    lane_e = lax.broadcasted_iota(I32, (tm, n_e), 1)
    onehot = jnp.where((lane_e == e1) | (lane_e == e2), 1.0, 0.0)
    incl = jnp.dot(tril_s[...], onehot.astype(BF16), preferred_element_type=F32)
    rank = incl - onehot + carry_s[...]
    r1 = jnp.sum(jnp.where(lane_e == e1, rank, 0.0), axis=-1, keepdims=True).astype(I32)
    r2 = jnp.sum(jnp.where(lane_e == e2, rank, 0.0), axis=-1, keepdims=True).astype(I32)
    carry_s[...] = carry_s[...] + incl[tm - 1:tm, :]
    cnt_ref[...] = carry_s[...]

    lane8 = lax.broadcasted_iota(I32, (tm, ROUTE_ROWS), 1)
    cols = [e1, e2, r1 >> 8, r1 & 255, r2 >> 8, r2 & 255]
    tok = jnp.zeros((tm, ROUTE_ROWS), F32)
    for c, val in enumerate(cols):
        tok = jnp.where(lane8 == c, val.astype(F32), tok)
    ro = lax.broadcasted_iota(I32, (ROUTE_ROWS, ROUTE_ROWS), 0)
    co = lax.broadcasted_iota(I32, (ROUTE_ROWS, ROUTE_ROWS), 1)
    mix = jnp.where(((ro == 0) & (co == 0)) | ((ro == 1) & (co == 1)) | ((ro == 2) & (co == 3)) | ((ro == 3) & (co == 5)),
                    1.0, jnp.where(((ro == 2) & (co == 2)) | ((ro == 3) & (co == 4)), 256.0, 0.0))
    ridx_ref[...] = lax.dot_general(mix.astype(BF16), tok.astype(BF16), nt, preferred_element_type=F32).astype(I32)


def _xattn(s1, kv, g1, b1, wq, wo, g2, b2, wrt, br, layer, seq, alpha, *, tm, name):
    t, d = s1.shape
    tm = _pick(seq, tm)
    n_mem, kvw = kv.shape[1:]
    mw = wq.shape[-1]
    n_e = wrt.shape[0]
    heads = mw // MEM_HEAD_DIM
    assert 2 * t + n_e * EXPERT_TILE < 65536
    assert tm % LN_ROWS == 0
    vec = pl.BlockSpec((None, 1, d), lambda i: (layer, 0, 0))
    hbm = pl.BlockSpec(memory_space=pl.ANY)
    row = lambda w: pl.BlockSpec((tm, w), lambda i: (i, 0))
    pieces = 4
    assert d % pieces == 0 and mw % pieces == 0
    return pl.pallas_call(
        functools.partial(_xattn_kernel, heads=heads, alpha=alpha, scale=MEM_HEAD_DIM ** -0.5, tm=tm, layer=layer),
        grid=(t // tm,),
        in_specs=[row(d), vec, vec, hbm,
                  pl.BlockSpec((None, n_mem, kvw), lambda i: (i // (seq // tm), 0, 0)),
                  hbm, vec, vec,
                  pl.BlockSpec((n_e, d), lambda i: (0, 0)),
                  pl.BlockSpec((1, n_e), lambda i: (0, 0))],
        out_specs=[row(d), row(d), row(LANES),
                   pl.BlockSpec((ROUTE_ROWS, tm), lambda i: (0, i)),
                   pl.BlockSpec((1, n_e), lambda i: (0, 0))],
        out_shape=[jax.ShapeDtypeStruct((t, d), F32), jax.ShapeDtypeStruct((t, d), BF16),
                   jax.ShapeDtypeStruct((t, LANES), F32), jax.ShapeDtypeStruct((ROUTE_ROWS, t), I32),
                   jax.ShapeDtypeStruct((1, n_e), F32)],
        scratch_shapes=[pltpu.VMEM((tm, tm), BF16), pltpu.VMEM((1, n_e), F32),
                        pltpu.VMEM((d, mw), BF16), pltpu.VMEM((mw, d), BF16),
                        pltpu.VMEM((d // pieces, mw), F32), pltpu.VMEM((mw // pieces, d), F32),
                        pltpu.SemaphoreType.DMA(())],
        compiler_params=_params("arbitrary"),
        name=name,
    )(s1, g1, b1, wq, kv, wo, g2, b2, wrt, br)


def _row_copy(src, s, dst, p, sem):
    return pltpu.make_async_copy(src.at[pl.ds(s, 1), :], dst.at[pl.ds(p, 1), :], sem)


def _slot_rows(d):
    return d // 2 // LANES + 1


def _slot_copy(src, s, dst, p, sem, ch):
    return pltpu.make_async_copy(src.at[pl.ds(s * ch, ch), :], dst.at[pl.ds(p * ch, ch), :], sem)


def _dispatch_kernel(off_ref, cnt_ref, pad_ref, na_ref, x_ref, idx_ref, xs_hbm, pk, sem, *, tm, n_e, n_tiles, ch):
    half = x_ref.shape[1] // 2
    rc = PACK_ROWS
    for r0 in range(0, tm, rc):
        hi = lax.bitcast_convert_type(x_ref[r0:r0 + rc, :half].astype(F32), U32)
        lo = lax.bitcast_convert_type(x_ref[r0:r0 + rc, half:].astype(F32), U32)
        p = hi | (lo >> 16)
        for c in range(ch - 1):
            pk[pl.ds(r0 * ch + c, rc, stride=ch), :] = p[:, c * LANES:(c + 1) * LANES]
        pk[pl.ds(r0 * ch + ch - 1, rc, stride=ch), :] = jnp.zeros((rc, LANES), U32)
        for s in range(r0, r0 + rc):
            _slot_copy(pk, s, xs_hbm, idx_ref[0, s], sem, ch).start(priority=0)
            _slot_copy(pk, s, xs_hbm, idx_ref[1, s], sem, ch).start(priority=1)

    @pl.when(pl.program_id(0) == 0)
    def _():
        for e in range(n_e):
            base = off_ref[e] + cnt_ref[e]
            n_pad = pad_ref[e] - cnt_ref[e]

            def fill(s, c):
                _slot_copy(pk, 0, xs_hbm, base + s, sem, ch).start()
                return c

            def drain(s, c):
                _slot_copy(pk, 0, xs_hbm, base, sem, ch).wait()
                return c

            lax.fori_loop(0, n_pad, fill, 0)
            lax.fori_loop(0, n_pad, drain, 0)

        rows = EXPERT_TILE * ch

        def tail_copy(i):
            return pltpu.make_async_copy(pk.at[pl.ds(0, rows), :],
                                         xs_hbm.at[pl.ds(pl.multiple_of(i * rows, rows), rows), :], sem)

        def tail_fill(i, c):
            tail_copy(i).start()
            return c

        def tail_drain(i, c):
            tail_copy(i).wait()
            return c

        lax.fori_loop(na_ref[0], n_tiles, tail_fill, 0)
        lax.fori_loop(na_ref[0], n_tiles, tail_drain, 0)

    pltpu.make_async_copy(xs_hbm.at[pl.ds(0, 2 * tm * ch), :], xs_hbm.at[pl.ds(0, 2 * tm * ch), :], sem).wait()


def _dispatch(xb, pos, off, cnt, padded, na, n_slots, *, tm, name):
    t, d = xb.shape
    tm = _pick(t, tm)
    n_e = off.shape[0]
    ch = _slot_rows(d)
    assert tm >= EXPERT_TILE and tm % PACK_ROWS == 0 and d % (2 * LANES) == 0
    return pl.pallas_call(
        functools.partial(_dispatch_kernel, tm=tm, n_e=n_e, n_tiles=n_slots // EXPERT_TILE, ch=ch),
        grid_spec=pltpu.PrefetchScalarGridSpec(
            num_scalar_prefetch=4, grid=(t // tm,),
            in_specs=[pl.BlockSpec((tm, d), lambda i, *_: (i, 0)),
                      pl.BlockSpec((ROUTE_ROWS, tm), lambda i, *_: (0, i), memory_space=pltpu.SMEM)],
            out_specs=pl.BlockSpec(memory_space=pl.ANY),
            scratch_shapes=[pltpu.VMEM((tm * ch, LANES), U32), pltpu.SemaphoreType.DMA(())]),
        out_shape=jax.ShapeDtypeStruct((n_slots * ch, LANES), U32),
        compiler_params=_params("arbitrary"),
        name=name,
    )(off, cnt, padded, na, xb, pos)


N_WEIGHT_PIECES = 6


def _experts_kernel(te_ref, nxt_ref, na_ref, xs_ref, wg_hbm, wu_hbm, wd_hbm, ys_ref,
                    wgb, wub, wdb, st_in, st_out, state, sem, *, layer):
    i = pl.program_id(0)
    hd, hf = st_in.shape[0], st_out.shape[0]

    def piece_copy(e, k):
        if k < 4:
            src = (wg_hbm if k < 2 else wu_hbm).at[layer, e, pl.ds((k % 2) * hd, hd), :]
            return pltpu.make_async_copy(src, st_in, sem)
        return pltpu.make_async_copy(wd_hbm.at[layer, e, pl.ds((k - 4) * hf, hf), :], st_out, sem)

    def for_piece(k, fn):
        for kk in range(N_WEIGHT_PIECES):
            pl.when(k == kk)(functools.partial(fn, kk))

    def start_piece(e, k):
        for_piece(k, lambda kk: piece_copy(e, kk).start())

    def finish_piece(e, k, slot):
        def fn(kk):
            piece_copy(e, kk).wait()
            if kk < 4:
                dst = wgb if kk < 2 else wub
                dst[slot, pl.ds((kk % 2) * hd, hd), :] = st_in[...].astype(BF16)
            else:
                wdb[slot, pl.ds((kk - 4) * hf, hf), :] = st_out[...].astype(BF16)

        for_piece(k, fn)

    def advance(e, slot):
        k = state[1]
        finish_piece(e, k, slot)
        state[1] = k + 1

        @pl.when(k + 1 < N_WEIGHT_PIECES)
        def _():
            start_piece(e, k + 1)

    def finish_all(e, slot):
        lax.fori_loop(state[1], N_WEIGHT_PIECES, lambda _, c: (advance(e, slot), c)[1], 0)

    def begin(e):
        state[1] = 0
        state[2] = e

        @pl.when(e >= 0)
        def _():
            start_piece(e, 0)

    active = i < na_ref[0]

    @pl.when(i == 0)
    def _():
        state[0] = 1
        begin(te_ref[0])

    switch = active & ((i == 0) | (te_ref[i] != te_ref[jnp.maximum(i - 1, 0)]))

    @pl.when(switch)
    def _():
        slot = 1 - state[0]
        finish_all(te_ref[i], slot)
        state[0] = slot
        begin(nxt_ref[i])

    @pl.when(active & jnp.logical_not(switch) & (state[2] >= 0) & (state[1] < N_WEIGHT_PIECES))
    def _():
        advance(state[2], 1 - state[0])

    @pl.when(active)
    def _():
        slot = state[0]
        ch = xs_ref.shape[0] // ys_ref.shape[0]
        parts = [_unpack_halves(xs_ref[pl.ds(c, ys_ref.shape[0], stride=ch), :]) for c in range(ch - 1)]
        hi = jnp.concatenate([p[0].astype(BF16) for p in parts], axis=1)
        lo = jnp.concatenate([p[1].astype(BF16) for p in parts], axis=1)
        half = hi.shape[1]
        hg = (jnp.dot(hi, wgb[slot, :half], preferred_element_type=F32)
              + jnp.dot(lo, wgb[slot, half:], preferred_element_type=F32))
        hu = (jnp.dot(hi, wub[slot, :half], preferred_element_type=F32)
              + jnp.dot(lo, wub[slot, half:], preferred_element_type=F32))
        hid = (hg * _sigmoid(hg) * hu).astype(BF16)
        ys_ref[...] = _pack_halves(jnp.dot(hid, wdb[slot], preferred_element_type=F32))

    @pl.when(jnp.logical_not(active))
    def _():
        ys_ref[...] = jnp.zeros(ys_ref.shape, ys_ref.dtype)


def _experts(xs, te, nxt, na, wg, wu, wd, layer, *, name):
    d, f = wg.shape[2:]
    w = d // 2
    ch = _slot_rows(d)
    n_slots = xs.shape[0] // ch
    tm = EXPERT_TILE
    nt = n_slots // tm
    assert d % 2 == 0 and f % 2 == 0 and xs.shape[1] == LANES

    def tile(i, te_ref, nxt_ref, na_ref):
        return (jnp.minimum(i, na_ref[0] - 1), 0)

    hbm = pl.BlockSpec(memory_space=pl.ANY)
    return pl.pallas_call(
        functools.partial(_experts_kernel, layer=layer),
        grid_spec=pltpu.PrefetchScalarGridSpec(
            num_scalar_prefetch=3, grid=(nt,),
            in_specs=[pl.BlockSpec((tm * ch, LANES), tile), hbm, hbm, hbm],
            out_specs=pl.BlockSpec((tm, w), lambda i, *_: (i, 0)),
            scratch_shapes=[pltpu.VMEM((2, d, f), BF16), pltpu.VMEM((2, d, f), BF16), pltpu.VMEM((2, f, d), BF16),
                            pltpu.VMEM((d // 2, f), F32), pltpu.VMEM((f // 2, d), F32),
                            pltpu.SMEM((3,), I32), pltpu.SemaphoreType.DMA(())]),
        out_shape=jax.ShapeDtypeStruct((n_slots, w), U32),
        compiler_params=_params("arbitrary"),
        name=name,
    )(te, nxt, na, xs, wg, wu, wd)


def _combine_kernel(idx_ref, nidx_ref, x_ref, rw_ref, g_ref, b_ref, ys_hbm, o_ref, ob_ref, buf_a, buf_b, sem, *,
                    tm, alpha):
    i = pl.program_id(0)
    n = pl.num_programs(0)

    def start_rows(ref, buf, slot_sem, r0, rows):
        for s in range(r0, r0 + rows):
            _row_copy(ys_hbm, ref[0, s], buf, s, slot_sem).start(priority=0)
            _row_copy(ys_hbm, ref[1, s], buf, tm + s, slot_sem).start(priority=1)

    def wait_tile(buf, slot_sem):
        pltpu.make_async_copy(ys_hbm.at[pl.ds(0, 2 * tm), :], buf, slot_sem).wait()

    @pl.when(i == 0)
    def _():
        def body(s, c):
            _row_copy(ys_hbm, idx_ref[0, s], buf_a, s, sem.at[0]).start()
            _row_copy(ys_hbm, idx_ref[1, s], buf_a, tm + s, sem.at[0]).start()
            return c

        lax.fori_loop(0, tm, body, 0, unroll=DMA_UNROLL)

    for parity, (cur, nxt) in enumerate(((buf_a, buf_b), (buf_b, buf_a))):
        @pl.when(i % 2 == parity)
        def _(cur=cur, nxt=nxt, parity=parity):
            wait_tile(cur, sem.at[parity])
            _combine_rows(cur, nidx_ref, nxt, sem.at[1 - parity], start_rows, x_ref, rw_ref, g_ref, b_ref,
                          o_ref, ob_ref, tm=tm, alpha=alpha)

            @pl.when(i == n - 1)
            def _():
                wait_tile(nxt, sem.at[1 - parity])


def _combine_rows(buf, nidx_ref, nxt_buf, nxt_sem, start_rows, x_ref, rw_ref, g_ref, b_ref, o_ref, ob_ref, *,
                  tm, alpha):
    rc = LN_ROWS
    half = buf.shape[-1]
    d = 2 * half

    for r0 in range(0, tm, rc):
        start_rows(nidx_ref, nxt_buf, nxt_sem, r0, rc)
        sl = pl.ds(r0, rc)
        hi1, lo1 = _unpack_halves(buf[sl, :])
        hi2, lo2 = _unpack_halves(buf[pl.ds(tm + r0, rc), :])
        w1 = rw_ref[sl, 0:1]
        w2 = rw_ref[sl, 1:2]
        s_hi = alpha * x_ref[sl, :half] + (w1 * hi1 + w2 * hi2)
        s_lo = alpha * x_ref[sl, half:] + (w1 * lo1 + w2 * lo2)
        mu = (jnp.sum(s_hi, axis=-1, keepdims=True) + jnp.sum(s_lo, axis=-1, keepdims=True)) / d
        c_hi, c_lo = s_hi - mu, s_lo - mu
        var = (jnp.sum(c_hi * c_hi, axis=-1, keepdims=True) + jnp.sum(c_lo * c_lo, axis=-1, keepdims=True)) / d
        inv = lax.rsqrt(var + LN_EPS)
        y_hi = c_hi * inv * g_ref[:, :half] + b_ref[:, :half]
        y_lo = c_lo * inv * g_ref[:, half:] + b_ref[:, half:]
        o_ref[sl, :half] = y_hi
        o_ref[sl, half:] = y_lo
        ob_ref[sl, :half] = y_hi.astype(BF16)
        ob_ref[sl, half:] = y_lo.astype(BF16)


def _combine(x2, rw, pos, ys, g, b, layer, alpha, *, tm, name):
    t, d = x2.shape
    tm = _pick(t, tm)
    nsteps = t // tm
    w = ys.shape[1]
    assert tm % LN_ROWS == 0
    row = lambda width: pl.BlockSpec((tm, width), lambda i: (i, 0))
    vec = pl.BlockSpec((None, 1, d), lambda i: (layer, 0, 0))
    return pl.pallas_call(
        functools.partial(_combine_kernel, tm=tm, alpha=alpha),
        grid=(nsteps,),
        in_specs=[pl.BlockSpec((ROUTE_ROWS, tm), lambda i: (0, i), memory_space=pltpu.SMEM),
                  pl.BlockSpec((ROUTE_ROWS, tm), lambda i: (0, jnp.minimum(i + 1, nsteps - 1)),
                               memory_space=pltpu.SMEM),
                  row(d), row(LANES), vec, vec,
                  pl.BlockSpec(memory_space=pl.ANY)],
        out_specs=[row(d), row(d)],
        out_shape=[jax.ShapeDtypeStruct((t, d), F32), jax.ShapeDtypeStruct((t, d), BF16)],
        scratch_shapes=[pltpu.VMEM((2 * tm, w), U32), pltpu.VMEM((2 * tm, w), U32), pltpu.SemaphoreType.DMA((2,))],
        compiler_params=_params("arbitrary"),
        name=name,
    )(pos, pos, x2, rw, g, b, ys)


def _expert_layout(counts, ridx, n_tiles):
    cnt = counts.reshape(-1).astype(I32)
    padded = (cnt + EXPERT_TILE - 1) // EXPERT_TILE * EXPERT_TILE
    ends = jnp.cumsum(padded)
    off = ends - padded
    tile_ends = ends // EXPERT_TILE
    na = tile_ends[-1:]
    tiles = jnp.minimum(jnp.arange(n_tiles, dtype=I32), na[0] - 1)
    expert_of = lambda tile: jnp.sum((tile[:, None] >= tile_ends[None, :]).astype(I32), axis=1)
    te = expert_of(tiles)
    seg_end = jnp.sum(jnp.where(te[:, None] == jnp.arange(cnt.shape[0], dtype=I32)[None, :], tile_ends[None, :], 0), axis=1)
    nxt = jnp.where(seg_end < na[0], expert_of(seg_end), -1)
    e, rank = ridx[0:2], ridx[2:4]
    base = jnp.sum(jnp.where(e[:, None, :] == jnp.arange(cnt.shape[0], dtype=I32)[None, :, None], off[None, :, None], 0),
                   axis=1)
    pos = jnp.concatenate([base + rank, jnp.zeros((ROUTE_ROWS - 2, ridx.shape[1]), I32)], axis=0)
    return off, cnt, padded, te, nxt, na, pos


def kernel(x, mem, w_in, conv_w, w_conv_proj, w_attn_proj, w_out, rel_bias, ln1_g, ln1_b, w_mem_q, w_mem_kv,
           w_mem_o, ln2_g, ln2_b, w_router, b_router, w_gate, w_up, w_down, ln3_g, ln3_b):
    batch, seq, d = x.shape
    depth = w_in.shape[0]
    t = batch * seq
    alpha = (2 * depth) ** 0.25
    cc = conv_w.shape[-1]
    n_groups = len(ATTN_GROUPS)
    aw = w_attn_proj.shape[1]
    hpg = aw // HEAD_DIM
    n_e = w_router.shape[1]
    assert rel_bias.shape[1] == n_groups * hpg
    qkv_w = 3 * n_groups * aw
    n_tiles = -(-(2 * t) // EXPERT_TILE) + n_e
    n_slots = n_tiles * EXPERT_TILE

    bf = lambda a: a.astype(BF16)
    wc_b, wa_b = bf(w_conv_proj), bf(w_attn_proj)
    wrt_b = bf(w_router.T)
    br = b_router.reshape(1, -1).astype(F32)
    vec3 = lambda a: a.reshape(depth, 1, d)
    ln1_g, ln1_b, ln2_g, ln2_b, ln3_g, ln3_b = map(vec3, (ln1_g, ln1_b, ln2_g, ln2_b, ln3_g, ln3_b))
    tbl = rel_bias.T.astype(F32)
    mem_b = bf(mem).reshape(batch * mem.shape[1], d)

    xf = x.reshape(t, d)
    xb = bf(xf)
    for l in range(depth):
        cin = _matmul(xb, w_in, l, BF16, tm=1024, tn=768, col0=0, ncols=3 * cc, name=f"inproj_conv{l}")
        qkv = _matmul(xb, w_in, l, F32, tm=1024, tn=768, col0=3 * cc, ncols=qkv_w, name=f"inproj_qkv{l}")
        gates = _matmul(xb, w_in, l, BF16, tm=1024, tn=1024, col0=3 * cc + qkv_w, ncols=2 * d, sigmoid=True,
                        name=f"inproj_gate{l}")
        others = []
        for g, (window, r) in enumerate(ATTN_GROUPS):
            last = g == n_groups - 1
            res = _dilated_attention(qkv, tbl[g * hpg:(g + 1) * hpg], g, n_groups, hpg, batch, seq, window, r,
                                     others if last else [], name=f"dattn{l}_{g}")
            if not last:
                others += list(res)
        merged = _merge(cin, gates, res, conv_w, wc_b, wa_b, l, seq, d, tm=1024, tn=512, name=f"merge{l}")
        s1 = _matmul(merged, w_out, l, F32, tm=512, tn=1024, resid=xf, alpha=alpha, name=f"outproj{l}")
        kv = _matmul(mem_b, w_mem_kv, l, BF16, tm=512, tn=1024, name=f"memkv{l}")
        kv = kv.reshape(batch, mem.shape[1], kv.shape[-1])
        x2, x2b, rw, ridx, counts = _xattn(s1, kv, ln1_g, ln1_b, w_mem_q, w_mem_o, ln2_g, ln2_b, wrt_b, br, l, seq, alpha,
                                           tm=256, name=f"xattn{l}")
        off, cnt, padded, te, nxt, na, pos = _expert_layout(counts, ridx, n_tiles)
        xs = _dispatch(x2b, pos, off, cnt, padded, na, n_slots, tm=1024, name=f"dispatch{l}")
        ys = _experts(xs, te, nxt, na, w_gate, w_up, w_down, l, name=f"experts{l}")
        xf, xb = _combine(x2, rw, pos, ys, ln3_g, ln3_b, l, alpha, tm=256, name=f"combine{l}")
    return xf.reshape(batch, seq, d)
```
